```python
import jax
import jax.numpy as jnp
from jax import lax
import numpy as np


D_MODEL = 1024
BATCH = 4
SEQ = 8192
DEPTH = 2

GRID_W = 64
CTX_LEN = 256
EPS = 1e-6
NA_HEADS = 8
NA_HEAD_DIM = 64
NA_WIDTH = NA_HEADS * NA_HEAD_DIM
NA_WIN_R = 8
NA_WIN_C = 16
NA_QBLOCK_C = 16
NA_KBAND_C = NA_QBLOCK_C + NA_WIN_C
POOL_WINDOWS = (2, 4, 8, 16)
POOL_GROUP = 128
POOL_WIDTH = POOL_GROUP * len(POOL_WINDOWS)
MLA_HEADS = 8
MLA_NOPE = 64
MLA_ROPE = 32
MLA_V = 64
MLA_Q_RANK = 512
MLA_KV_RANK = 256
MLA_QBLOCK = 128
ROPE_BASE = 10000.0
N_BRANCH = 3
BRANCH_W = 512
D_FF = 2816
CONV_W = 3
IN_SIZES = (NA_WIDTH, NA_WIDTH, NA_WIDTH, POOL_WIDTH, MLA_Q_RANK, MLA_KV_RANK, MLA_ROPE, N_BRANCH * D_MODEL)
IN_COLS = 3 * NA_WIDTH + POOL_WIDTH + MLA_Q_RANK + MLA_KV_RANK + MLA_ROPE + N_BRANCH * D_MODEL

kernel_name = 'hybrid_dit_na_pool_mla_block'


def rms_norm(x, g):
    xf = x.astype(jnp.float32)
    y = xf * lax.rsqrt(jnp.mean(xf * xf, axis=-1, keepdims=True) + EPS)
    return (y * g.astype(jnp.float32)).astype(x.dtype)


def modulate(h, shift, scale):
    return h * (1 + scale) + shift


def split_cols(z, sizes):
    outs, o = [], 0
    for s in sizes:
        outs.append(z[..., o:o + s])
        o += s
    return outs


def axial_angles(n_tok, dim):
    n_freq = dim // 4
    inv = ROPE_BASE ** (-jnp.arange(n_freq, dtype=jnp.float32) / n_freq)
    t = jnp.arange(n_tok, dtype=jnp.int32)
    row = (t // GRID_W).astype(jnp.float32)
    col = (t % GRID_W).astype(jnp.float32)
    return row[:, None] * inv[None, :], col[:, None] * inv[None, :]


def _rot_half(x, ang):
    x1, x2 = jnp.split(x, 2, axis=-1)
    cos, sin = jnp.cos(ang), jnp.sin(ang)
    return jnp.concatenate([x1 * cos - x2 * sin, x1 * sin + x2 * cos], axis=-1)


def apply_axial_rope(x, ang_r, ang_c):
    xf = x.astype(jnp.float32)
    xr, xc = jnp.split(xf, 2, axis=-1)
    return jnp.concatenate([_rot_half(xr, ang_r), _rot_half(xc, ang_c)], axis=-1).astype(x.dtype)


def dwconv_centred(u, w, b):
    L = u.shape[1]
    pad = CONV_W // 2
    up = jnp.pad(u, ((0, 0), (pad, pad), (0, 0)))
    y = b
    for j in range(CONV_W):
        y = y + up[:, j:j + L] * w[j]
    return y


def pool_mixer(u, w_grp, scale):
    B, L, _ = u.shape
    uf = u.astype(jnp.float32)
    cs = jnp.concatenate([jnp.zeros((B, 1, POOL_WIDTH), jnp.float32), jnp.cumsum(uf, axis=1)], axis=1)
    t = jnp.arange(L)
    diffs = []
    for gi, win in enumerate(POOL_WINDOWS):
        lo = jnp.clip(t - win // 2, 0, L)
        hi = jnp.clip(t + win // 2, 0, L)
        sl = slice(gi * POOL_GROUP, (gi + 1) * POOL_GROUP)
        csg = cs[..., sl]
        cnt = (hi - lo).astype(jnp.float32)[None, :, None]
        diffs.append((csg[:, hi] - csg[:, lo]) / cnt - uf[..., sl])
    d = jnp.stack(diffs, axis=2)
    y = jnp.einsum('blgi,gio->blgo', d, w_grp.astype(jnp.float32)).reshape(B, L, POOL_WIDTH)
    return (y * scale.astype(jnp.float32)).astype(u.dtype)


def dense_attn(q, k, v):
    s = jnp.einsum('bqhd,bkhd->bhqk', q, k, preferred_element_type=jnp.float32) * (q.shape[-1] ** -0.5)
    p = jax.nn.softmax(s, axis=-1).astype(v.dtype)
    return jnp.einsum('bhqk,bkhd->bqhd', p, v)


def na_latent(q, k, v, k_ctx, v_ctx, rpb):
    B, S, H, dh = q.shape
    rows = S // GRID_W
    kr = min(NA_WIN_R, rows)
    ncb = GRID_W // NA_QBLOCK_C
    scale = dh ** -0.5
    qcol = jnp.arange(GRID_W).reshape(ncb, NA_QBLOCK_C)
    win0 = jnp.clip(qcol - NA_WIN_C // 2, 0, GRID_W - NA_WIN_C)
    band0 = jnp.clip(jnp.arange(ncb) * NA_QBLOCK_C - NA_WIN_C // 2, 0, GRID_W - NA_KBAND_C)
    kcol = band0[:, None] + jnp.arange(NA_KBAND_C)[None, :]
    kc3 = kcol[:, None, :]
    in_win = (kc3 >= win0[..., None]) & (kc3 < win0[..., None] + NA_WIN_C)
    rel_c = jnp.clip(kc3 - qcol[..., None] + NA_WIN_C - 1, 0, 2 * NA_WIN_C - 2)
    kg = k.reshape(B, rows, GRID_W, H, dh)
    vg = v.reshape(B, rows, GRID_W, H, dh)
    qg = jnp.moveaxis(q.reshape(B, rows, ncb, NA_QBLOCK_C, H, dh), 1, 0)
    n_win = kr * NA_KBAND_C

    def one_row(args):
        r, q_r = args
        r0 = jnp.clip(r - kr // 2, 0, rows - kr)
        k_rows = lax.dynamic_slice_in_dim(kg, r0, kr, axis=1)
        v_rows = lax.dynamic_slice_in_dim(vg, r0, kr, axis=1)
        k_nb = k_rows[:, :, kcol]
        v_nb = v_rows[:, :, kcol]
        rel_r = r0 + jnp.arange(kr) - r + NA_WIN_R - 1
        bias = rpb[:, rel_r[None, None, :, None], rel_c[:, :, None, :]]
        s_win = jnp.einsum('bjqhd,bkjchd->bhjqkc', q_r, k_nb, preferred_element_type=jnp.float32) * scale
        s_win = jnp.where(in_win[:, :, None, :], s_win + bias.astype(jnp.float32), -jnp.inf)
        s_ctx = jnp.einsum('bjqhd,bnhd->bhjqn', q_r, k_ctx, preferred_element_type=jnp.float32) * scale
        s = jnp.concatenate([s_win.reshape(B, H, ncb, NA_QBLOCK_C, n_win), s_ctx], axis=-1)
        p = jax.nn.softmax(s, axis=-1).astype(v.dtype)
        p_win = p[..., :n_win].reshape(B, H, ncb, NA_QBLOCK_C, kr, NA_KBAND_C)
        p_ctx = p[..., n_win:]
        return (jnp.einsum('bhjqkc,bkjchd->bjqhd', p_win, v_nb)
                + jnp.einsum('bhjqn,bnhd->bjqhd', p_ctx, v_ctx))

    out = lax.map(one_row, (jnp.arange(rows), qg))
    return jnp.moveaxis(out, 0, 1).reshape(B, S, H, dh)


def mla_attend(q_nope, q_rope, k_nope, k_rope, v):
    B, L, H, _ = q_nope.shape
    nb = L // MLA_QBLOCK
    scale = (MLA_NOPE + MLA_ROPE) ** -0.5

    def blk(args):
        qn, qr = args
        s = (jnp.einsum('bqhd,bthd->bhqt', qn, k_nope, preferred_element_type=jnp.float32)
             + jnp.einsum('bqhd,btd->bhqt', qr, k_rope, preferred_element_type=jnp.float32)) * scale
        p = jax.nn.softmax(s, axis=-1).astype(v.dtype)
        return jnp.einsum('bhqt,bthd->bqhd', p, v)

    def to_blocks(a):
        return jnp.moveaxis(a.reshape(B, nb, MLA_QBLOCK, *a.shape[2:]), 1, 0)

    out = lax.map(blk, (to_blocks(q_nope), to_blocks(q_rope)))
    return jnp.moveaxis(out, 0, 1).reshape(B, L, H, MLA_V)


def project_stream(h, lp, rope_angles):
    B, L, _ = h.shape
    q, k, v, u, cq, ckv, kr, g = split_cols(h @ lp['w_in'], IN_SIZES)
    qm = (rms_norm(cq, lp['mla_q_norm']) @ lp['w_uq']).reshape(B, L, MLA_HEADS, MLA_NOPE + MLA_ROPE)
    kvm = (rms_norm(ckv, lp['mla_kv_norm']) @ lp['w_ukv']).reshape(B, L, MLA_HEADS, MLA_NOPE + MLA_V)
    q_rope = qm[..., MLA_NOPE:]
    if rope_angles is not None:
        ang_r, ang_c = rope_angles
        q_rope = apply_axial_rope(q_rope, ang_r[:, None, :], ang_c[:, None, :])
        kr = apply_axial_rope(kr, ang_r, ang_c)
    return {
        'na_q': q.reshape(B, L, NA_HEADS, NA_HEAD_DIM),
        'na_k': k.reshape(B, L, NA_HEADS, NA_HEAD_DIM),
        'na_v': v.reshape(B, L, NA_HEADS, NA_HEAD_DIM),
        'pool_u': u,
        'q_nope': qm[..., :MLA_NOPE],
        'q_rope': q_rope,
        'k_nope': kvm[..., :MLA_NOPE],
        'v': kvm[..., MLA_NOPE:],
        'k_rope': kr,
        'gates': g,
    }


def merge_branches(o_na, o_pool, o_mla, gate_logits, lp):
    B, L, _ = o_pool.shape
    br = jnp.stack([o_na.reshape(B, L, BRANCH_W), o_pool, o_mla.reshape(B, L, BRANCH_W)], axis=2)
    proj = jnp.einsum('blki,kid->blkd', br, lp['w_branch'])
    gates = jax.nn.sigmoid(gate_logits.astype(jnp.float32)).astype(proj.dtype).reshape(B, L, N_BRANCH, D_MODEL)
    return jnp.sum(gates * proj, axis=2) @ lp['w_o']


def conv_ffn(h, lp):
    u = dwconv_centred(h @ lp['w_up'], lp['conv_w'], lp['conv_b'])
    a, b = jnp.split(u, 2, axis=-1)
    return (jax.nn.gelu(a, approximate=True) * b) @ lp['w_down']


def hybrid_layer(x, xc, c, c_ctx, lp, rope_angles, last):
    mod = (jax.nn.silu(c) @ lp['w_ada'] + lp['b_ada'])[:, None, :]
    mod_c = jax.nn.silu(c_ctx) @ lp['w_ada'] + lp['b_ada']
    sh1, sc1, g1, sh2, sc2, g2 = jnp.split(mod, 6, axis=-1)
    csh1, csc1, cg1, csh2, csc2, cg2 = jnp.split(mod_c, 6, axis=-1)

    h = modulate(rms_norm(x, lp['norm_pre1']), sh1, sc1)
    hc = modulate(rms_norm(xc, lp['norm_pre1']), csh1, csc1)
    P = project_stream(h, lp, rope_angles)
    C = project_stream(hc, lp, None)

    o_na = na_latent(P['na_q'], P['na_k'], P['na_v'], C['na_k'], C['na_v'], lp['na_rpb'])
    o_pool = pool_mixer(P['pool_u'], lp['pool_w'], lp['pool_scale'])
    k_nope_all = jnp.concatenate([C['k_nope'], P['k_nope']], axis=1)
    k_rope_all = jnp.concatenate([C['k_rope'], P['k_rope']], axis=1)
    v_all = jnp.concatenate([C['v'], P['v']], axis=1)
    o_mla = mla_attend(P['q_nope'], P['q_rope'], k_nope_all, k_rope_all, v_all)
    y = merge_branches(o_na, o_pool, o_mla, P['gates'], lp)
    x = x + g1 * rms_norm(y, lp['norm_post1'])

    h2 = modulate(rms_norm(x, lp['norm_pre2']), sh2, sc2)
    x = x + g2 * rms_norm(conv_ffn(h2, lp), lp['norm_post2'])

    if last:
        return x, None
    oc_na = dense_attn(C['na_q'], C['na_k'], C['na_v'])
    oc_pool = pool_mixer(C['pool_u'], lp['pool_w'], lp['pool_scale'])
    oc_mla = mla_attend(C['q_nope'], C['q_rope'], C['k_nope'], C['k_rope'], C['v'])
    yc = merge_branches(oc_na, oc_pool, oc_mla, C['gates'], lp)
    xc = xc + cg1 * rms_norm(yc, lp['norm_post1'])
    hc2 = modulate(rms_norm(xc, lp['norm_pre2']), csh2, csc2)
    xc = xc + cg2 * rms_norm(conv_ffn(hc2, lp), lp['norm_post2'])
    return x, xc


def setup_inputs(seed: int = 0) -> dict:
    key = jax.random.key(seed)
    ks = jax.random.split(key, 24)
    f32 = jnp.float32

    def nrm(k, shape, s):
        return jax.random.normal(k, shape, f32) * s

    def gain(k, n):
        return 1.0 + 0.05 * jax.random.normal(k, (DEPTH, n), f32)

    return {
        'x': nrm(ks[0], (BATCH, SEQ, D_MODEL), 1.0),
        'c': nrm(ks[1], (BATCH, D_MODEL), 1.0),
        'ctx': nrm(ks[2], (BATCH, CTX_LEN, D_MODEL), 1.0),
        'c_ctx': nrm(ks[3], (D_MODEL,), 1.0),
        'w_ada': nrm(ks[4], (DEPTH, D_MODEL, 6 * D_MODEL), 0.5 * D_MODEL ** -0.5),
        'b_ada': nrm(ks[5], (DEPTH, 6 * D_MODEL), 0.02),
        'norm_pre1': gain(ks[6], D_MODEL),
        'norm_post1': gain(ks[7], D_MODEL),
        'norm_pre2': gain(ks[8], D_MODEL),
        'norm_post2': gain(ks[9], D_MODEL),
        'w_in': nrm(ks[10], (DEPTH, D_MODEL, IN_COLS), D_MODEL ** -0.5),
        'na_rpb': nrm(ks[11], (DEPTH, NA_HEADS, 2 * NA_WIN_R - 1, 2 * NA_WIN_C - 1), 0.1),
        'pool_w': nrm(ks[12], (DEPTH, len(POOL_WINDOWS), POOL_GROUP, POOL_GROUP), POOL_GROUP ** -0.5),
        'pool_scale': 1.0 + 0.1 * jax.random.normal(ks[13], (DEPTH, POOL_WIDTH), f32),
        'mla_q_norm': gain(ks[14], MLA_Q_RANK),
        'w_uq': nrm(ks[15], (DEPTH, MLA_Q_RANK, MLA_HEADS * (MLA_NOPE + MLA_ROPE)), MLA_Q_RANK ** -0.5),
        'mla_kv_norm': gain(ks[16], MLA_KV_RANK),
        'w_ukv': nrm(ks[17], (DEPTH, MLA_KV_RANK, MLA_HEADS * (MLA_NOPE + MLA_V)), MLA_KV_RANK ** -0.5),
        'w_branch': nrm(ks[18], (DEPTH, N_BRANCH, BRANCH_W, D_MODEL), BRANCH_W ** -0.5),
        'w_o': nrm(ks[19], (DEPTH, D_MODEL, D_MODEL), D_MODEL ** -0.5),
        'w_up': nrm(ks[20], (DEPTH, D_MODEL, 2 * D_FF), D_MODEL ** -0.5),
        'conv_w': nrm(ks[21], (DEPTH, CONV_W, 2 * D_FF), CONV_W ** -0.5),
        'conv_b': nrm(ks[22], (DEPTH, 2 * D_FF), 0.02),
        'w_down': nrm(ks[23], (DEPTH, D_FF, D_MODEL), D_FF ** -0.5),
    }


def reference(x, c, ctx, c_ctx, w_ada, b_ada, norm_pre1, norm_post1, norm_pre2, norm_post2,
              w_in, na_rpb, pool_w, pool_scale, mla_q_norm, w_uq, mla_kv_norm, w_ukv,
              w_branch, w_o, w_up, conv_w, conv_b, w_down):
    rope_angles = axial_angles(x.shape[1], MLA_ROPE)
    xc = ctx
    for l in range(DEPTH):
        lp = {
            'w_ada': w_ada[l], 'b_ada': b_ada[l],
            'norm_pre1': norm_pre1[l], 'norm_post1': norm_post1[l],
            'norm_pre2': norm_pre2[l], 'norm_post2': norm_post2[l],
            'w_in': w_in[l], 'na_rpb': na_rpb[l],
            'pool_w': pool_w[l], 'pool_scale': pool_scale[l],
            'mla_q_norm': mla_q_norm[l], 'w_uq': w_uq[l],
            'mla_kv_norm': mla_kv_norm[l], 'w_ukv': w_ukv[l],
            'w_branch': w_branch[l], 'w_o': w_o[l],
            'w_up': w_up[l], 'conv_w': conv_w[l], 'conv_b': conv_b[l], 'w_down': w_down[l],
        }
        x, xc = hybrid_layer(x, xc, c, c_ctx, lp, rope_angles, l == DEPTH - 1)
    return x
```

```python
import functools

import jax
import jax.numpy as jnp
import numpy as np
from jax import lax
from jax.experimental import pallas as pl
from jax.experimental.pallas import tpu as pltpu

F32 = jnp.float32
BF16 = jnp.bfloat16

D_MODEL = 1024
GRID_W = 64
EPS = 1e-6
NA_HEADS = 8
NA_HEAD_DIM = 64
NA_WIDTH = NA_HEADS * NA_HEAD_DIM
NA_WIN_R = 8
NA_WIN_C = 16
POOL_WINDOWS = (2, 4, 8, 16)
POOL_GROUP = 128
POOL_WIDTH = POOL_GROUP * len(POOL_WINDOWS)
POOL_HALO = 16
MLA_HEADS = 8
MLA_NOPE = 64
MLA_ROPE = 32
MLA_V = 64
MLA_Q_RANK = 512
MLA_KV_RANK = 256
ROPE_BASE = 10000.0
N_BRANCH = 3
BRANCH_W = 512
D_FF = 2816
CONV_HALO = 8

LANES = 128
HEAD_PAD = LANES
MLA_PAD_W = MLA_HEADS * HEAD_PAD

Z_GATE = 0
Z_Q = N_BRANCH * D_MODEL
Z_K = Z_Q + NA_WIDTH
Z_V = Z_K + NA_WIDTH
Z_U = Z_V + NA_WIDTH
Z_CQ = Z_U + POOL_WIDTH
Z_KV = Z_CQ + MLA_Q_RANK
Z_COLS = Z_KV + 512
CB = 512

NEG = -1e30
VMEM_LIMIT = 56 * 1024 * 1024


def _cparams(sem):
    return pltpu.CompilerParams(dimension_semantics=sem, vmem_limit_bytes=VMEM_LIMIT)


def _rms(x, gain):
    return x * lax.rsqrt(jnp.mean(x * x, axis=-1, keepdims=True) + EPS) * gain


def _ada_kernel(c_ref, w_ref, b_ref, o_ref):
    c = c_ref[...]
    s = c * jax.nn.sigmoid(c)
    o_ref[0] = jnp.dot(s.astype(BF16), w_ref[0].astype(BF16), preferred_element_type=F32) + b_ref[0]


def _ada(cvec, w_ada, b_ada):
    depth, d, n = w_ada.shape
    rows = cvec.shape[0]
    tn = 1536
    return pl.pallas_call(
        _ada_kernel,
        grid=(depth, n // tn),
        in_specs=[
            pl.BlockSpec((rows, d), lambda l, j: (0, 0)),
            pl.BlockSpec((1, d, tn), lambda l, j: (l, 0, j)),
            pl.BlockSpec((1, 1, tn), lambda l, j: (l, 0, j)),
        ],
        out_specs=pl.BlockSpec((1, rows, tn), lambda l, j: (l, 0, j)),
        out_shape=jax.ShapeDtypeStruct((depth, rows, n), F32),
        compiler_params=_cparams(("arbitrary", "arbitrary")),
        name="ada",
    )(cvec, w_ada, b_ada.reshape(depth, 1, n))


def _in_proj_kernel(x_ref, gain_ref, sc_ref, sh_ref, w_ref, o_ref, h_scr, *, n_gate_blocks):
    j = pl.program_id(2)

    @pl.when(j == 0)
    def _():
        h = _rms(x_ref[0], gain_ref[...]) * (1.0 + sc_ref[0]) + sh_ref[0]
        h_scr[...] = h.astype(BF16)

    z = jnp.dot(h_scr[...], w_ref[...], preferred_element_type=F32)

    @pl.when(j < n_gate_blocks)
    def _():
        o_ref[0] = jax.nn.sigmoid(z).astype(BF16)

    @pl.when(j >= n_gate_blocks)
    def _():
        o_ref[0] = z.astype(BF16)


def _in_proj(x, gain, sc, sh, w, tm, tn):
    b, l, d = x.shape
    tm = min(tm, l)
    per_batch = sc.shape[0] > 1
    mod_map = (lambda bi, i, j: (bi, 0, 0)) if per_batch else (lambda bi, i, j: (0, 0, 0))
    return pl.pallas_call(
        functools.partial(_in_proj_kernel, n_gate_blocks=(N_BRANCH * D_MODEL) // tn),
        grid=(b, l // tm, Z_COLS // tn),
        in_specs=[
            pl.BlockSpec((1, tm, d), lambda bi, i, j: (bi, i, 0)),
            pl.BlockSpec((1, d), lambda bi, i, j: (0, 0)),
            pl.BlockSpec((1, 1, d), mod_map),
            pl.BlockSpec((1, 1, d), mod_map),
            pl.BlockSpec((d, tn), lambda bi, i, j: (0, j)),
        ],
        out_specs=pl.BlockSpec((1, tm, tn), lambda bi, i, j: (bi, i, j)),
        out_shape=jax.ShapeDtypeStruct((b, l, Z_COLS), BF16),
        scratch_shapes=[pltpu.VMEM((tm, d), BF16)],
        compiler_params=_cparams(("parallel", "parallel", "arbitrary")),
        name="in_proj",
    )(x, gain, sc, sh, w)


def _mla_up_kernel(cq_ref, kv_ref, qg_ref, kvg_ref, ct_ref, st_ref, wq_ref, wqp_ref, wk_ref, wv_ref,
                   ones_ref, q_out, k_out, v_out, *, scale):
    cqn = _rms(cq_ref[0].astype(F32), qg_ref[...]).astype(BF16)
    kvb = kv_ref[0].astype(F32)
    kvn = _rms(kvb[:, :MLA_KV_RANK], kvg_ref[...]).astype(BF16)
    ct = ct_ref[...]
    st = st_ref[...]
    kr = kvb[:, MLA_KV_RANK:MLA_KV_RANK + LANES] * ct + kvb[:, MLA_KV_RANK + LANES:] * st
    qm = jnp.dot(cqn, wq_ref[...], preferred_element_type=F32)
    qp = jnp.dot(cqn, wqp_ref[...], preferred_element_type=F32)
    kn = jnp.dot(kvn, wk_ref[...], preferred_element_type=F32)
    cts = ct * scale
    sts = st * scale
    for h in range(MLA_HEADS):
        hs = slice(h * HEAD_PAD, (h + 1) * HEAD_PAD)
        q_out[0, :, hs] = (qm[:, hs] * cts + qp[:, hs] * sts).astype(BF16)
        k_out[0, :, hs] = (kn[:, hs] + kr).astype(BF16)
    v_out[0] = (jnp.dot(kvn, wv_ref[...], preferred_element_type=F32) + ones_ref[...]).astype(BF16)


def _mla_up(z, qg, kvg, ct, st, wq, wqp, wk, wv, ones, tm):
    b, l, _ = z.shape
    tm = min(tm, l)
    full = lambda shape: pl.BlockSpec(shape, lambda bi, i: (0,) * len(shape))
    out = jax.ShapeDtypeStruct((b, l, MLA_PAD_W), BF16)
    ospec = pl.BlockSpec((1, tm, MLA_PAD_W), lambda bi, i: (bi, i, 0))
    return pl.pallas_call(
        functools.partial(_mla_up_kernel, scale=float((MLA_NOPE + MLA_ROPE) ** -0.5)),
        grid=(b, l // tm),
        in_specs=[
            pl.BlockSpec((1, tm, CB), lambda bi, i: (bi, i, Z_CQ // CB)),
            pl.BlockSpec((1, tm, CB), lambda bi, i: (bi, i, Z_KV // CB)),
            full((1, MLA_Q_RANK)),
            full((1, MLA_KV_RANK)),
            pl.BlockSpec((tm, LANES), lambda bi, i: (i, 0)),
            pl.BlockSpec((tm, LANES), lambda bi, i: (i, 0)),
            full((MLA_Q_RANK, MLA_PAD_W)),
            full((MLA_Q_RANK, MLA_PAD_W)),
            full((MLA_KV_RANK, MLA_PAD_W)),
            full((MLA_KV_RANK, MLA_PAD_W)),
            full((1, MLA_PAD_W)),
        ],
        out_specs=[ospec, ospec, ospec],
        out_shape=[out, out, out],
        compiler_params=_cparams(("parallel", "parallel")),
        name="mla_up",
    )(z, z, qg, kvg, ct, st, wq, wqp, wk, wv, ones)


def _flash_kernel(*refs, has_latent):
    if has_latent:
        q_ref, kc_ref, vc_ref, k_ref, v_ref, o_ref, m_scr, acc_scr = refs
    else:
        q_ref, kc_ref, vc_ref, o_ref, m_scr, acc_scr = refs
        k_ref = v_ref = None
    kk = pl.program_id(2)
    nk = pl.num_programs(2)

    def attend(h, kblk, vblk):
        hs = slice(h * HEAD_PAD, (h + 1) * HEAD_PAD)
        q = q_ref[0, :, hs]
        s = lax.dot_general(q, kblk[0, :, hs], (((1,), (1,)), ((), ())), preferred_element_type=F32)
        m_prev = m_scr[h]
        m_new = jnp.maximum(m_prev, jnp.max(s, axis=-1, keepdims=True))
        alpha = jnp.exp(m_prev - m_new)
        p = jnp.exp(s - m_new[:, :1])
        acc_scr[h] = alpha * acc_scr[h] + jnp.dot(p.astype(BF16), vblk[0, :, hs], preferred_element_type=F32)
        m_scr[h] = m_new

    @pl.when(kk == 0)
    def _():
        m_scr[...] = jnp.full(m_scr.shape, NEG, F32)
        acc_scr[...] = jnp.zeros(acc_scr.shape, F32)
        for h in range(MLA_HEADS):
            attend(h, kc_ref, vc_ref)

    if has_latent:
        for h in range(MLA_HEADS):
            attend(h, k_ref, v_ref)

    @pl.when(kk == nk - 1)
    def _():
        for h in range(MLA_HEADS):
            acc = acc_scr[h]
            o_ref[0, :, h * MLA_V:(h + 1) * MLA_V] = (acc[:, :MLA_V] / acc[:, MLA_V:MLA_V + 1]).astype(BF16)


def _flash(q, kc, vc, k, v, tq, tk):
    b, lq, _ = q.shape
    lc = kc.shape[1]
    tq = min(tq, lq)
    has_latent = k is not None
    in_specs = [
        pl.BlockSpec((1, tq, MLA_PAD_W), lambda bi, i, kk: (bi, i, 0)),
        pl.BlockSpec((1, lc, MLA_PAD_W), lambda bi, i, kk: (bi, 0, 0)),
        pl.BlockSpec((1, lc, MLA_PAD_W), lambda bi, i, kk: (bi, 0, 0)),
    ]
    args = [q, kc, vc]
    nk = 1
    if has_latent:
        tk = min(tk, k.shape[1])
        nk = k.shape[1] // tk
        in_specs += [pl.BlockSpec((1, tk, MLA_PAD_W), lambda bi, i, kk: (bi, kk, 0))] * 2
        args += [k, v]
    return pl.pallas_call(
        functools.partial(_flash_kernel, has_latent=has_latent),
        grid=(b, lq // tq, nk),
        in_specs=in_specs,
        out_specs=pl.BlockSpec((1, tq, MLA_HEADS * MLA_V), lambda bi, i, kk: (bi, i, 0)),
        out_shape=jax.ShapeDtypeStruct((b, lq, MLA_HEADS * MLA_V), BF16),
        scratch_shapes=[pltpu.VMEM((MLA_HEADS, tq, LANES), F32), pltpu.VMEM((MLA_HEADS, tq, LANES), F32)],
        compiler_params=_cparams(("parallel", "parallel", "arbitrary")),
        name="mla_flash" if has_latent else "mla_ctx",
    )(*args)


NA_GROUP = 8
NA_GTOK = NA_GROUP * GRID_W


def _pair_softmax_pv(qp, k_list, v_list, bias_fn):
    lane = lax.broadcasted_iota(jnp.int32, qp.shape, 1)
    outs = []
    for sub in range(2):
        in_head = (lane >= sub * NA_HEAD_DIM) & (lane < (sub + 1) * NA_HEAD_DIM)
        qm = jnp.where(in_head, qp, jnp.zeros_like(qp))
        s_list = []
        for idx, kb in enumerate(k_list):
            s = lax.dot_general(qm, kb, (((1,), (1,)), ((), ())), preferred_element_type=F32)
            s_list.append(bias_fn(sub, idx, s))
        m = s_list[0].max(axis=-1, keepdims=True)
        for s in s_list[1:]:
            m = jnp.maximum(m, s.max(axis=-1, keepdims=True))
        o = None
        l = None
        for s, vb in zip(s_list, v_list):
            p = jnp.exp(s - m)
            ls = p.sum(axis=-1, keepdims=True)
            os_ = jnp.dot(p.astype(BF16), vb, preferred_element_type=F32)
            o = os_ if o is None else o + os_
            l = ls if l is None else l + ls
        outs.append(o / l)
    return jnp.where(lane < NA_HEAD_DIM, outs[0], outs[1])


def _na_kernel(q_ref, kp_ref, kc_ref, kn_ref, vp_ref, vc_ref, vn_ref, kx_ref, vx_ref, bias_ref, o_ref,
               kcat, vcat, *, rows):
    g = pl.program_id(1)
    kcat[0:NA_GTOK] = kp_ref[0]
    kcat[NA_GTOK:2 * NA_GTOK] = kc_ref[0]
    kcat[2 * NA_GTOK:3 * NA_GTOK] = kn_ref[0]
    vcat[0:NA_GTOK] = vp_ref[0]
    vcat[NA_GTOK:2 * NA_GTOK] = vc_ref[0]
    vcat[2 * NA_GTOK:3 * NA_GTOK] = vn_ref[0]
    kr = min(NA_WIN_R, rows)

    def row(i, carry):
        r = g * NA_GROUP + i
        r0 = jnp.clip(r - kr // 2, 0, rows - kr)
        start = pl.multiple_of((r0 - g * NA_GROUP + NA_GROUP) * GRID_W, GRID_W)
        bidx = r - r0
        q_i = q_ref[0, pl.ds(pl.multiple_of(i * GRID_W, GRID_W), GRID_W), :]
        for pr in range(NA_HEADS // 2):
            ps = slice(pr * LANES, (pr + 1) * LANES)
            k_w = kcat[pl.ds(start, kr * GRID_W), ps]
            v_w = vcat[pl.ds(start, kr * GRID_W), ps]

            def bias_fn(sub, idx, s, pr=pr):
                return s + bias_ref[bidx, 2 * pr + sub] if idx == 0 else s

            o = _pair_softmax_pv(q_i[:, ps], [k_w, kx_ref[0, :, ps]], [v_w, vx_ref[0, :, ps]], bias_fn)
            o_ref[0, pl.ds(pl.multiple_of(i * GRID_W, GRID_W), GRID_W), ps] = o.astype(BF16)
        return carry

    lax.fori_loop(0, NA_GROUP, row, 0)


def _na(z, zc, bias):
    b, s, _ = z.shape
    lc = zc.shape[1]
    rows = s // GRID_W
    ng = rows // NA_GROUP
    blk = (1, NA_GTOK, CB)

    def spec(col, off):
        return pl.BlockSpec(blk, lambda bi, g: (bi, jnp.clip(g + off, 0, ng - 1), col // CB))

    return pl.pallas_call(
        functools.partial(_na_kernel, rows=rows),
        grid=(b, ng),
        in_specs=[
            spec(Z_Q, 0),
            spec(Z_K, -1), spec(Z_K, 0), spec(Z_K, 1),
            spec(Z_V, -1), spec(Z_V, 0), spec(Z_V, 1),
            pl.BlockSpec((1, lc, CB), lambda bi, g: (bi, 0, Z_K // CB)),
            pl.BlockSpec((1, lc, CB), lambda bi, g: (bi, 0, Z_V // CB)),
            pl.BlockSpec(bias.shape, lambda bi, g: (0, 0, 0, 0)),
        ],
        out_specs=pl.BlockSpec(blk, lambda bi, g: (bi, g, 0)),
        out_shape=jax.ShapeDtypeStruct((b, s, NA_WIDTH), BF16),
        scratch_shapes=[pltpu.VMEM((3 * NA_GTOK, CB), BF16), pltpu.VMEM((3 * NA_GTOK, CB), BF16)],
        compiler_params=_cparams(("parallel", "arbitrary")),
        name="na_attn",
    )(z, z, z, z, z, z, z, zc, zc, bias)


def _ctx_na_kernel(q_ref, k_ref, v_ref, o_ref):
    for pr in range(NA_HEADS // 2):
        ps = slice(pr * LANES, (pr + 1) * LANES)
        o = _pair_softmax_pv(q_ref[0, :, ps], [k_ref[0, :, ps]], [v_ref[0, :, ps]], lambda sub, idx, s: s)
        o_ref[0, :, ps] = o.astype(BF16)


def _ctx_na(zc):
    b, lc, _ = zc.shape
    spec = lambda col: pl.BlockSpec((1, lc, CB), lambda bi: (bi, 0, col // CB))
    return pl.pallas_call(
        _ctx_na_kernel,
        grid=(b,),
        in_specs=[spec(Z_Q), spec(Z_K), spec(Z_V)],
        out_specs=pl.BlockSpec((1, lc, NA_WIDTH), lambda bi: (bi, 0, 0)),
        out_shape=jax.ShapeDtypeStruct((b, lc, NA_WIDTH), BF16),
        compiler_params=_cparams(("parallel",)),
        name="ctx_na",
    )(zc, zc, zc)


def _pool_kernel(up_ref, u_ref, un_ref, w_ref, sc_ref, o_ref, *, seq_len):
    i = pl.program_id(1)
    tm = u_ref.shape[1]
    ext = jnp.concatenate([up_ref[0], u_ref[0], un_ref[0]], axis=0).astype(F32)
    n_ext = tm + 2 * POOL_HALO
    tg = i * tm - POOL_HALO + lax.broadcasted_iota(jnp.int32, (n_ext, 1), 0)
    ext = jnp.where((tg >= 0) & (tg < seq_len), ext, 0.0)
    t = i * tm + lax.broadcasted_iota(jnp.int32, (tm, 1), 0)

    def shifted(a, k):
        return pltpu.roll(a, (-k) % n_ext, 0)

    for gi, win in enumerate(POOL_WINDOWS):
        gs = slice(gi * POOL_GROUP, (gi + 1) * POOL_GROUP)
        a = ext[:, gs]
        wsum = a + shifted(a, -1)
        half = 1
        while 2 * half < win:
            wsum = shifted(wsum, -half) + shifted(wsum, half)
            half *= 2
        cnt = (jnp.minimum(t + win // 2, seq_len) - jnp.maximum(t - win // 2, 0)).astype(F32)
        d = wsum[POOL_HALO:POOL_HALO + tm] / cnt - a[POOL_HALO:POOL_HALO + tm]
        y = jnp.dot(d.astype(BF16), w_ref[gi], preferred_element_type=F32)
        o_ref[0, :, gs] = (y * sc_ref[:, gs]).astype(BF16)


def _pool(z, w, sc, tm):
    b, l, _ = z.shape
    tm = min(tm, l)
    hb = tm // POOL_HALO
    nhb = l // POOL_HALO
    col = Z_U // CB
    return pl.pallas_call(
        functools.partial(_pool_kernel, seq_len=l),
        grid=(b, l // tm),
        in_specs=[
            pl.BlockSpec((1, POOL_HALO, CB), lambda bi, i: (bi, jnp.maximum(i * hb - 1, 0), col)),
            pl.BlockSpec((1, tm, CB), lambda bi, i: (bi, i, col)),
            pl.BlockSpec((1, POOL_HALO, CB), lambda bi, i: (bi, jnp.minimum((i + 1) * hb, nhb - 1), col)),
            pl.BlockSpec(w.shape, lambda bi, i: (0, 0, 0)),
            pl.BlockSpec((1, POOL_WIDTH), lambda bi, i: (0, 0)),
        ],
        out_specs=pl.BlockSpec((1, tm, POOL_WIDTH), lambda bi, i: (bi, i, 0)),
        out_shape=jax.ShapeDtypeStruct((b, l, POOL_WIDTH), BF16),
        compiler_params=_cparams(("parallel", "parallel")),
        name="pool",
    )(z, z, z, w, sc)


def _merge_kernel(ona_ref, opool_ref, omla_ref, g0_ref, g1_ref, g2_ref, wb_ref, wo_ref, x_ref, gate_ref,
                  gain_ref, o_ref):
    m = None
    for br_ref, g_ref, k in ((ona_ref, g0_ref, 0), (opool_ref, g1_ref, 1), (omla_ref, g2_ref, 2)):
        proj = jnp.dot(br_ref[0], wb_ref[k], preferred_element_type=F32)
        term = g_ref[0].astype(F32) * proj
        m = term if m is None else m + term
    y = jnp.dot(m.astype(BF16), wo_ref[...], preferred_element_type=F32)
    o_ref[0] = x_ref[0] + gate_ref[0] * _rms(y, gain_ref[...])


def _merge(o_na, o_pool, o_mla, z, wb, wo, x, gate, gain, tm):
    b, l, d = x.shape
    tm = min(tm, l)
    per_batch = gate.shape[0] > 1
    mod_map = (lambda bi, i: (bi, 0, 0)) if per_batch else (lambda bi, i: (0, 0, 0))
    br = pl.BlockSpec((1, tm, BRANCH_W), lambda bi, i: (bi, i, 0))
    gspec = lambda k: pl.BlockSpec((1, tm, d), lambda bi, i: (bi, i, k))
    return pl.pallas_call(
        _merge_kernel,
        grid=(b, l // tm),
        in_specs=[
            br, br, br, gspec(0), gspec(1), gspec(2),
            pl.BlockSpec(wb.shape, lambda bi, i: (0, 0, 0)),
            pl.BlockSpec(wo.shape, lambda bi, i: (0, 0)),
            pl.BlockSpec((1, tm, d), lambda bi, i: (bi, i, 0)),
            pl.BlockSpec((1, 1, d), mod_map),
            pl.BlockSpec((1, d), lambda bi, i: (0, 0)),
        ],
        out_specs=pl.BlockSpec((1, tm, d), lambda bi, i: (bi, i, 0)),
        out_shape=jax.ShapeDtypeStruct((b, l, d), F32),
        compiler_params=_cparams(("parallel", "parallel")),
        name="merge",
    )(o_na, o_pool, o_mla, z, z, z, wb, wo, x, gate, gain)


def _gelu_tanh(x):
    return 0.5 * x * (1.0 + jnp.tanh(np.float32(np.sqrt(2.0 / np.pi)) * (x + np.float32(0.044715) * (x * x * x))))


def _ffn_kernel(xp_ref, x_ref, xn_ref, gain_ref, sc_ref, sh_ref, wa_ref, wb_ref, cwa_ref, cwb_ref, cba_ref,
                cbb_ref, wd_ref, gate_ref, gpost_ref, o_ref, h_scr, acc_scr):
    i = pl.program_id(1)
    j = pl.program_id(2)
    ni = pl.num_programs(1)
    nj = pl.num_programs(2)
    tm = x_ref.shape[1]
    n_ext = tm + 2 * CONV_HALO

    @pl.when(j == 0)
    def _():
        def norm_mod(xv):
            return _rms(xv, gain_ref[...]) * (1.0 + sc_ref[0]) + sh_ref[0]

        hp = jnp.where(i > 0, norm_mod(xp_ref[0]), 0.0)
        hn = jnp.where(i < ni - 1, norm_mod(xn_ref[0]), 0.0)
        h_scr[0:CONV_HALO] = hp.astype(BF16)
        h_scr[CONV_HALO:CONV_HALO + tm] = norm_mod(x_ref[0]).astype(BF16)
        h_scr[CONV_HALO + tm:n_ext] = hn.astype(BF16)
        acc_scr[...] = jnp.zeros(acc_scr.shape, F32)

    h = h_scr[...]

    def conv_half(w_ref, cw_ref, cb_ref):
        u = jnp.dot(h, w_ref[...], preferred_element_type=F32)
        prev = pltpu.roll(u, 1, 0)[CONV_HALO:CONV_HALO + tm]
        nxt = pltpu.roll(u, n_ext - 1, 0)[CONV_HALO:CONV_HALO + tm]
        cur = u[CONV_HALO:CONV_HALO + tm]
        return cb_ref[...] + prev * cw_ref[0:1] + cur * cw_ref[1:2] + nxt * cw_ref[2:3]

    a = conv_half(wa_ref, cwa_ref, cba_ref)
    bgate = conv_half(wb_ref, cwb_ref, cbb_ref)
    act = (_gelu_tanh(a) * bgate).astype(BF16)
    acc_scr[...] += jnp.dot(act, wd_ref[...], preferred_element_type=F32)

    @pl.when(j == nj - 1)
    def _():
        o_ref[0] = x_ref[0] + gate_ref[0] * _rms(acc_scr[...], gpost_ref[...])


def _ffn(x, gain, sc, sh, w_up, conv_w, conv_b, w_down, gate, gpost, tm, tn):
    b, l, d = x.shape
    tm = min(tm, l)
    nch = D_FF // tn
    hb = tm // CONV_HALO
    nhb = l // CONV_HALO
    per_batch = sc.shape[0] > 1
    mod_map = (lambda bi, i, j: (bi, 0, 0)) if per_batch else (lambda bi, i, j: (0, 0, 0))
    mod = pl.BlockSpec((1, 1, d), mod_map)
    vec = pl.BlockSpec((1, d), lambda bi, i, j: (0, 0))
    return pl.pallas_call(
        _ffn_kernel,
        grid=(b, l // tm, nch),
        in_specs=[
            pl.BlockSpec((1, CONV_HALO, d), lambda bi, i, j: (bi, jnp.maximum(i * hb - 1, 0), 0)),
            pl.BlockSpec((1, tm, d), lambda bi, i, j: (bi, i, 0)),
            pl.BlockSpec((1, CONV_HALO, d), lambda bi, i, j: (bi, jnp.minimum((i + 1) * hb, nhb - 1), 0)),
            vec, mod, mod,
            pl.BlockSpec((d, tn), lambda bi, i, j: (0, j)),
            pl.BlockSpec((d, tn), lambda bi, i, j: (0, nch + j)),
            pl.BlockSpec((3, tn), lambda bi, i, j: (0, j)),
            pl.BlockSpec((3, tn), lambda bi, i, j: (0, nch + j)),
            pl.BlockSpec((1, tn), lambda bi, i, j: (0, j)),
            pl.BlockSpec((1, tn), lambda bi, i, j: (0, nch + j)),
            pl.BlockSpec((tn, d), lambda bi, i, j: (j, 0)),
            mod, vec,
        ],
        out_specs=pl.BlockSpec((1, tm, d), lambda bi, i, j: (bi, i, 0)),
        out_shape=jax.ShapeDtypeStruct((b, l, d), F32),
        scratch_shapes=[pltpu.VMEM((tm + 2 * CONV_HALO, d), BF16), pltpu.VMEM((tm, d), F32)],
        compiler_params=_cparams(("parallel", "parallel", "arbitrary")),
        name="ffn",
    )(x, x, x, gain, sc, sh, w_up, w_up, conv_w, conv_w, conv_b, conv_b, w_down, gate, gpost)


_ROPE_PERM = np.concatenate([np.arange(8, 16), np.arange(0, 8), np.arange(24, 32), np.arange(16, 24)])


def _rope_tables(n_tok, rotate):
    ones = jnp.ones((n_tok, MLA_NOPE), F32)
    zeros = jnp.zeros((n_tok, MLA_NOPE), F32)
    pad1 = jnp.ones((n_tok, HEAD_PAD - MLA_NOPE - MLA_ROPE), F32)
    pad0 = jnp.zeros((n_tok, HEAD_PAD - MLA_NOPE - MLA_ROPE), F32)
    if not rotate:
        return (jnp.concatenate([ones, jnp.ones((n_tok, MLA_ROPE), F32), pad1], axis=1),
                jnp.concatenate([zeros, jnp.zeros((n_tok, MLA_ROPE), F32), pad0], axis=1))
    n_freq = MLA_ROPE // 4
    inv = ROPE_BASE ** (-jnp.arange(n_freq, dtype=F32) / n_freq)
    t = jnp.arange(n_tok, dtype=jnp.int32)
    ang_r = (t // GRID_W).astype(F32)[:, None] * inv[None, :]
    ang_c = (t % GRID_W).astype(F32)[:, None] * inv[None, :]
    cr, sr, cc, sn = jnp.cos(ang_r), jnp.sin(ang_r), jnp.cos(ang_c), jnp.sin(ang_c)
    ct = jnp.concatenate([ones, cr, cr, cc, cc, pad1], axis=1)
    st = jnp.concatenate([zeros, -sr, sr, -sn, sn, pad0], axis=1)
    return ct, st


def _prep_layer(w_in, w_uq, w_ukv, na_rpb, rows):
    d = w_in.shape[0]
    o = 0
    parts = []
    for s in (NA_WIDTH, NA_WIDTH, NA_WIDTH, POOL_WIDTH, MLA_Q_RANK, MLA_KV_RANK, MLA_ROPE, N_BRANCH * D_MODEL):
        parts.append(w_in[:, o:o + s])
        o += s
    wq, wk, wv, wu, wcq, wckv, wkr, wg = parts
    z64 = jnp.zeros((d, MLA_NOPE), F32)
    z32 = jnp.zeros((d, HEAD_PAD - MLA_NOPE - MLA_ROPE), F32)
    slab_a = jnp.concatenate([z64, wkr, z32], axis=1)
    slab_b = jnp.concatenate([z64, wkr[:, _ROPE_PERM], z32], axis=1)
    w_z = jnp.concatenate([wg, wq * (NA_HEAD_DIM ** -0.5), wk, wv, wu, wcq, wckv, slab_a, slab_b], axis=1)

    uq = w_uq.reshape(MLA_Q_RANK, MLA_HEADS, MLA_NOPE + MLA_ROPE)
    qz64 = jnp.zeros((MLA_Q_RANK, MLA_HEADS, MLA_NOPE), F32)
    qz32 = jnp.zeros((MLA_Q_RANK, MLA_HEADS, HEAD_PAD - MLA_NOPE - MLA_ROPE), F32)
    wq_pad = jnp.concatenate([uq[..., :MLA_NOPE], uq[..., MLA_NOPE:], qz32], axis=-1)
    wq_perm = jnp.concatenate([qz64, uq[..., MLA_NOPE:][..., _ROPE_PERM], qz32], axis=-1)
    ukv = w_ukv.reshape(MLA_KV_RANK, MLA_HEADS, MLA_NOPE + MLA_V)
    kz = jnp.zeros((MLA_KV_RANK, MLA_HEADS, HEAD_PAD - MLA_NOPE), F32)
    wk_pad = jnp.concatenate([ukv[..., :MLA_NOPE], kz], axis=-1)
    wv_pad = jnp.concatenate([ukv[..., MLA_NOPE:], jnp.zeros((MLA_KV_RANK, MLA_HEADS, HEAD_PAD - MLA_V), F32)], axis=-1)

    kr = min(NA_WIN_R, rows)
    qc = jnp.arange(GRID_W)[:, None]
    kc = jnp.arange(GRID_W)[None, :]
    win0 = jnp.clip(qc - NA_WIN_C // 2, 0, GRID_W - NA_WIN_C)
    in_win = (kc >= win0) & (kc < win0 + NA_WIN_C)
    rel_c = jnp.clip(kc - qc + NA_WIN_C - 1, 0, 2 * NA_WIN_C - 2)
    off = jnp.arange(NA_WIN_R)[:, None]
    rel_r = jnp.clip(jnp.arange(kr)[None, :] - off + NA_WIN_R - 1, 0, 2 * NA_WIN_R - 2)
    tbl = na_rpb[:, rel_r[:, :, None, None], rel_c[None, None, :, :]]
    tbl = jnp.where(in_win[None, None, None], tbl, NEG)
    bias = jnp.transpose(tbl, (1, 0, 3, 2, 4)).reshape(NA_WIN_R, NA_HEADS, GRID_W, kr * GRID_W)

    return dict(
        w_z=w_z.astype(BF16),
        wq_pad=wq_pad.reshape(MLA_Q_RANK, MLA_PAD_W).astype(BF16),
        wq_perm=wq_perm.reshape(MLA_Q_RANK, MLA_PAD_W).astype(BF16),
        wk_pad=wk_pad.reshape(MLA_KV_RANK, MLA_PAD_W).astype(BF16),
        wv_pad=wv_pad.reshape(MLA_KV_RANK, MLA_PAD_W).astype(BF16),
        bias=bias.astype(F32),
    )


TM_PROJ = 1024
TN_PROJ = 1024
TM_UP = 512
TQ_FLASH = 512
TK_FLASH = 512
TM_POOL = 1024
TM_MERGE = 512
TM_FFN = 512
TN_FFN = 1408


def kernel(x, c, ctx, c_ctx, w_ada, b_ada, norm_pre1, norm_post1, norm_pre2, norm_post2, w_in, na_rpb, pool_w,
           pool_scale, mla_q_norm, w_uq, mla_kv_norm, w_ukv, w_branch, w_o, w_up, conv_w, conv_b, w_down):
    b, s, d = x.shape
    lc = ctx.shape[1]
    depth = w_ada.shape[0]
    rows = s // GRID_W
    assert d == D_MODEL and s % NA_GTOK == 0 and lc % POOL_HALO == 0

    n_mod = -(-(b + 1) // 8) * 8
    cvec = jnp.zeros((n_mod, d), F32).at[:b].set(c).at[b].set(c_ctx)
    mod = _ada(cvec, w_ada, b_ada)

    ct_l, st_l = _rope_tables(s, True)
    ct_c, st_c = _rope_tables(lc, False)
    ones = jnp.zeros((MLA_HEADS, HEAD_PAD), F32).at[:, MLA_V].set(1.0).reshape(1, MLA_PAD_W)

    xc = ctx
    for l in range(depth):
        last = l == depth - 1
        p = _prep_layer(w_in[l], w_uq[l], w_ukv[l], na_rpb[l], rows)
        m6 = mod[l].reshape(n_mod, 6, d)
        lat = [m6[:b, k][:, None, :] for k in range(6)]
        cx = [m6[b:b + 1, k][:, None, :] for k in range(6)]
        row = lambda v: v.reshape(1, -1)
        g_pre1, g_post1, g_pre2, g_post2 = row(norm_pre1[l]), row(norm_post1[l]), row(norm_pre2[l]), row(norm_post2[l])
        qg, kvg = row(mla_q_norm[l]), row(mla_kv_norm[l])
        pw = pool_w[l].astype(BF16)
        psc = row(pool_scale[l])
        wb = w_branch[l].astype(BF16)
        wo = w_o[l].astype(BF16)
        wup = w_up[l].astype(BF16)
        wdn = w_down[l].astype(BF16)
        cw = conv_w[l]
        cb = row(conv_b[l])

        z = _in_proj(x, g_pre1, lat[1], lat[0], p["w_z"], TM_PROJ, TN_PROJ)
        zc = _in_proj(xc, g_pre1, cx[1], cx[0], p["w_z"], TM_PROJ, TN_PROJ)
        mla_w = (p["wq_pad"], p["wq_perm"], p["wk_pad"], p["wv_pad"], ones)
        q_l, k_l, v_l = _mla_up(z, qg, kvg, ct_l, st_l, *mla_w, TM_UP)
        q_c, k_c, v_c = _mla_up(zc, qg, kvg, ct_c, st_c, *mla_w, TM_UP)

        o_na = _na(z, zc, p["bias"])
        o_pool = _pool(z, pw, psc, TM_POOL)
        o_mla = _flash(q_l, k_c, v_c, k_l, v_l, TQ_FLASH, TK_FLASH)
        x = _merge(o_na, o_pool, o_mla, z, wb, wo, x, lat[2], g_post1, TM_MERGE)
        x = _ffn(x, g_pre2, lat[4], lat[3], wup, cw, cb, wdn, lat[5], g_post2, TM_FFN, TN_FFN)

        if not last:
            oc_na = _ctx_na(zc)
            oc_pool = _pool(zc, pw, psc, TM_POOL)
            oc_mla = _flash(q_c, k_c, v_c, None, None, TQ_FLASH, TK_FLASH)
            xc = _merge(oc_na, oc_pool, oc_mla, zc, wb, wo, xc, cx[2], g_post1, TM_MERGE)
            xc = _ffn(xc, g_pre2, cx[4], cx[3], wup, cw, cb, wdn, cx[5], g_post2, TM_FFN, TN_FFN)
    return x
```

```python
import functools

import jax
import jax.numpy as jnp
import numpy as np
from jax import lax
from jax.experimental import pallas as pl
from jax.experimental.pallas import tpu as pltpu

F32 = jnp.float32
BF16 = jnp.bfloat16

D_MODEL = 1024
GRID_W = 64
EPS = 1e-6
NA_HEADS = 8
NA_HEAD_DIM = 64
NA_WIDTH = NA_HEADS * NA_HEAD_DIM
NA_WIN_R = 8
NA_WIN_C = 16
POOL_WINDOWS = (2, 4, 8, 16)
POOL_GROUP = 128
POOL_WIDTH = POOL_GROUP * len(POOL_WINDOWS)
POOL_HALO = 16
MLA_HEADS = 8
MLA_NOPE = 64
MLA_ROPE = 32
MLA_V = 64
MLA_Q_RANK = 512
MLA_KV_RANK = 256
ROPE_BASE = 10000.0
N_BRANCH = 3
BRANCH_W = 512
D_FF = 2816
CONV_HALO = 8

LANES = 128
HEAD_PAD = LANES
MLA_PAD_W = MLA_HEADS * HEAD_PAD

Z_Q = 0
Z_K = Z_Q + NA_WIDTH
Z_V = Z_K + NA_WIDTH
Z_U = Z_V + NA_WIDTH
Z_CQ = Z_U + POOL_WIDTH
Z_KV = Z_CQ + MLA_Q_RANK
Z_COLS = Z_KV + 512
G_COLS = N_BRANCH * D_MODEL
assert G_COLS == Z_COLS
CB = 512

NEG = -1e30
VMEM_LIMIT = 56 * 1024 * 1024


def _cparams(sem):
    return pltpu.CompilerParams(dimension_semantics=sem, vmem_limit_bytes=VMEM_LIMIT)


def _rms(x, gain):
    return x * lax.rsqrt(jnp.mean(x * x, axis=-1, keepdims=True) + EPS) * gain


def _ada_kernel(c_ref, w_ref, b_ref, o_ref):
    c = c_ref[...]
    s = c * jax.nn.sigmoid(c)
    o_ref[0] = jnp.dot(s.astype(BF16), w_ref[0].astype(BF16), preferred_element_type=F32) + b_ref[0]


def _ada(cvec, w_ada, b_ada):
    depth, d, n = w_ada.shape
    rows = cvec.shape[0]
    tn = 1536
    return pl.pallas_call(
        _ada_kernel,
        grid=(depth, n // tn),
        in_specs=[
            pl.BlockSpec((rows, d), lambda l, j: (0, 0)),
            pl.BlockSpec((1, d, tn), lambda l, j: (l, 0, j)),
            pl.BlockSpec((1, 1, tn), lambda l, j: (l, 0, j)),
        ],
        out_specs=pl.BlockSpec((1, rows, tn), lambda l, j: (l, 0, j)),
        out_shape=jax.ShapeDtypeStruct((depth, rows, n), F32),
        compiler_params=_cparams(("arbitrary", "arbitrary")),
        name="ada",
    )(cvec, w_ada, b_ada.reshape(depth, 1, n))


def _in_proj_kernel(x_ref, gain_ref, sc_ref, sh_ref, wz_ref, wg_ref, z_ref, g_ref, h_scr):
    j = pl.program_id(2)

    @pl.when(j == 0)
    def _():
        h = _rms(x_ref[0], gain_ref[...]) * (1.0 + sc_ref[0]) + sh_ref[0]
        h_scr[...] = h.astype(BF16)

    h = h_scr[...]
    g_ref[0] = jax.nn.sigmoid(jnp.dot(h, wg_ref[...], preferred_element_type=F32)).astype(BF16)
    z_ref[0] = jnp.dot(h, wz_ref[...], preferred_element_type=F32).astype(BF16)


def _in_proj(x, gain, sc, sh, wz, wg, tm, tn):
    b, l, d = x.shape
    tm = min(tm, l)
    per_batch = sc.shape[0] > 1
    mod_map = (lambda bi, i, j: (bi, 0, 0)) if per_batch else (lambda bi, i, j: (0, 0, 0))
    wspec = pl.BlockSpec((d, tn), lambda bi, i, j: (0, j))
    ospec = pl.BlockSpec((1, tm, tn), lambda bi, i, j: (bi, i, j))
    oshape = jax.ShapeDtypeStruct((b, l, Z_COLS), BF16)
    return pl.pallas_call(
        _in_proj_kernel,
        grid=(b, l // tm, Z_COLS // tn),
        in_specs=[
            pl.BlockSpec((1, tm, d), lambda bi, i, j: (bi, i, 0)),
            pl.BlockSpec((1, d), lambda bi, i, j: (0, 0)),
            pl.BlockSpec((1, 1, d), mod_map),
            pl.BlockSpec((1, 1, d), mod_map),
            wspec, wspec,
        ],
        out_specs=[ospec, ospec],
        out_shape=[oshape, oshape],
        scratch_shapes=[pltpu.VMEM((tm, d), BF16)],
        compiler_params=_cparams(("parallel", "parallel", "arbitrary")),
        name="in_proj",
    )(x, gain, sc, sh, wz, wg)


def _mla_up_kernel(cq_ref, kv_ref, qg_ref, kvg_ref, ct_ref, st_ref, wq_ref, wqp_ref, wk_ref, wv_ref,
                   ones_ref, q_out, k_out, v_out, *, scale):
    cqn = _rms(cq_ref[0].astype(F32), qg_ref[...]).astype(BF16)
    kvb = kv_ref[0].astype(F32)
    kvn = _rms(kvb[:, :MLA_KV_RANK], kvg_ref[...]).astype(BF16)
    ct = ct_ref[...]
    st = st_ref[...]
    kr = kvb[:, MLA_KV_RANK:MLA_KV_RANK + LANES] * ct + kvb[:, MLA_KV_RANK + LANES:] * st
    qm = jnp.dot(cqn, wq_ref[...], preferred_element_type=F32)
    qp = jnp.dot(cqn, wqp_ref[...], preferred_element_type=F32)
    kn = jnp.dot(kvn, wk_ref[...], preferred_element_type=F32)
    cts = ct * scale
    sts = st * scale
    for h in range(MLA_HEADS):
        hs = slice(h * HEAD_PAD, (h + 1) * HEAD_PAD)
        q_out[0, :, hs] = (qm[:, hs] * cts + qp[:, hs] * sts).astype(BF16)
        k_out[0, :, hs] = (kn[:, hs] + kr).astype(BF16)
    v_out[0] = (jnp.dot(kvn, wv_ref[...], preferred_element_type=F32) + ones_ref[...]).astype(BF16)


def _mla_up(z, qg, kvg, ct, st, wq, wqp, wk, wv, ones, tm):
    b, l, _ = z.shape
    tm = min(tm, l)
    full = lambda shape: pl.BlockSpec(shape, lambda bi, i: (0,) * len(shape))
    out = jax.ShapeDtypeStruct((b, l, MLA_PAD_W), BF16)
    ospec = pl.BlockSpec((1, tm, MLA_PAD_W), lambda bi, i: (bi, i, 0))
    return pl.pallas_call(
        functools.partial(_mla_up_kernel, scale=float((MLA_NOPE + MLA_ROPE) ** -0.5 * np.log2(np.e))),
        grid=(b, l // tm),
        in_specs=[
            pl.BlockSpec((1, tm, CB), lambda bi, i: (bi, i, Z_CQ // CB)),
            pl.BlockSpec((1, tm, CB), lambda bi, i: (bi, i, Z_KV // CB)),
            full((1, MLA_Q_RANK)),
            full((1, MLA_KV_RANK)),
            pl.BlockSpec((tm, LANES), lambda bi, i: (i, 0)),
            pl.BlockSpec((tm, LANES), lambda bi, i: (i, 0)),
            full((MLA_Q_RANK, MLA_PAD_W)),
            full((MLA_Q_RANK, MLA_PAD_W)),
            full((MLA_KV_RANK, MLA_PAD_W)),
            full((MLA_KV_RANK, MLA_PAD_W)),
            full((1, MLA_PAD_W)),
        ],
        out_specs=[ospec, ospec, ospec],
        out_shape=[out, out, out],
        compiler_params=_cparams(("parallel", "parallel")),
        name="mla_up",
    )(z, z, qg, kvg, ct, st, wq, wqp, wk, wv, ones)


def _flash_kernel(*refs, has_latent):
    if has_latent:
        q_ref, kc_ref, vc_ref, k_ref, v_ref, o_ref, m_scr, acc_scr = refs
    else:
        q_ref, kc_ref, vc_ref, o_ref, m_scr, acc_scr = refs
        k_ref = v_ref = None
    kk = pl.program_id(2)
    nk = pl.num_programs(2)

    def attend(h, kblk, vblk):
        hs = slice(h * HEAD_PAD, (h + 1) * HEAD_PAD)
        q = q_ref[0, :, hs]
        s = lax.dot_general(q, kblk[0, :, hs], (((1,), (1,)), ((), ())), preferred_element_type=F32)
        m_prev = m_scr[h]
        m_new = jnp.maximum(m_prev, jnp.max(s, axis=-1, keepdims=True))
        alpha = jnp.exp2(m_prev - m_new)
        p = jnp.exp2(s - jnp.concatenate([m_new] * (s.shape[1] // LANES), axis=1))
        acc_scr[h] = alpha * acc_scr[h] + jnp.dot(p.astype(BF16), vblk[0, :, hs], preferred_element_type=F32)
        m_scr[h] = m_new

    @pl.when(kk == 0)
    def _():
        m_scr[...] = jnp.full(m_scr.shape, NEG, F32)
        acc_scr[...] = jnp.zeros(acc_scr.shape, F32)
        for h in range(MLA_HEADS):
            attend(h, kc_ref, vc_ref)

    if has_latent:
        for h in range(MLA_HEADS):
            attend(h, k_ref, v_ref)

    @pl.when(kk == nk - 1)
    def _():
        for h in range(MLA_HEADS):
            acc = acc_scr[h]
            o_ref[0, :, h * MLA_V:(h + 1) * MLA_V] = (acc[:, :MLA_V] / acc[:, MLA_V:MLA_V + 1]).astype(BF16)


def _flash(q, kc, vc, k, v, tq, tk):
    b, lq, _ = q.shape
    lc = kc.shape[1]
    tq = min(tq, lq)
    has_latent = k is not None
    in_specs = [
        pl.BlockSpec((1, tq, MLA_PAD_W), lambda bi, i, kk: (bi, i, 0)),
        pl.BlockSpec((1, lc, MLA_PAD_W), lambda bi, i, kk: (bi, 0, 0)),
        pl.BlockSpec((1, lc, MLA_PAD_W), lambda bi, i, kk: (bi, 0, 0)),
    ]
    args = [q, kc, vc]
    nk = 1
    if has_latent:
        tk = min(tk, k.shape[1])
        nk = k.shape[1] // tk
        in_specs += [pl.BlockSpec((1, tk, MLA_PAD_W), lambda bi, i, kk: (bi, kk, 0))] * 2
        args += [k, v]
    return pl.pallas_call(
        functools.partial(_flash_kernel, has_latent=has_latent),
        grid=(b, lq // tq, nk),
        in_specs=in_specs,
        out_specs=pl.BlockSpec((1, tq, MLA_HEADS * MLA_V), lambda bi, i, kk: (bi, i, 0)),
        out_shape=jax.ShapeDtypeStruct((b, lq, MLA_HEADS * MLA_V), BF16),
        scratch_shapes=[pltpu.VMEM((MLA_HEADS, tq, LANES), F32), pltpu.VMEM((MLA_HEADS, tq, LANES), F32)],
        compiler_params=_cparams(("parallel", "parallel", "arbitrary")),
        name="mla_flash" if has_latent else "mla_ctx",
    )(*args)


NA_GROUP = 8
NA_GTOK = NA_GROUP * GRID_W


def _pair_softmax_pv(qp, k_list, v_list, bias_fn):
    lane = lax.broadcasted_iota(jnp.int32, qp.shape, 1)
    outs = []
    for sub in range(2):
        in_head = (lane >= sub * NA_HEAD_DIM) & (lane < (sub + 1) * NA_HEAD_DIM)
        qm = jnp.where(in_head, qp, jnp.zeros_like(qp))
        s_list = []
        for idx, kb in enumerate(k_list):
            s = lax.dot_general(qm, kb, (((1,), (1,)), ((), ())), preferred_element_type=F32)
            s_list.append(bias_fn(sub, idx, s))
        m = s_list[0].max(axis=-1, keepdims=True)
        for s in s_list[1:]:
            m = jnp.maximum(m, s.max(axis=-1, keepdims=True))
        o = None
        l = None
        for s, vb in zip(s_list, v_list):
            p = jnp.exp(s - m)
            ls = p.sum(axis=-1, keepdims=True)
            os_ = jnp.dot(p.astype(BF16), vb, preferred_element_type=F32)
            o = os_ if o is None else o + os_
            l = ls if l is None else l + ls
        outs.append(o / l)
    return jnp.where(lane < NA_HEAD_DIM, outs[0], outs[1])


NA_KROWS = 2 * NA_GROUP
NA_KTOK = NA_KROWS * GRID_W


def _bias_kernel(rpb_ref, o_ref):
    n = GRID_W * GRID_W
    j = lax.broadcasted_iota(jnp.int32, (1, n), 1)
    qc = j // GRID_W
    kc = j % GRID_W
    win0 = jnp.clip(qc - NA_WIN_C // 2, 0, GRID_W - NA_WIN_C)
    in_win = (kc >= win0) & (kc < win0 + NA_WIN_C)
    rel = jnp.clip(kc - qc + NA_WIN_C - 1, 0, 2 * NA_WIN_C - 2)
    acc = jnp.zeros(o_ref.shape, F32)
    for b in range(2 * NA_WIN_C - 1):
        acc = jnp.where(rel == b, rpb_ref[:, b:b + 1], acc)
    o_ref[...] = jnp.where(in_win, acc, NEG)


def _bias_cols(rpb2):
    n_rows = rpb2.shape[0]
    return pl.pallas_call(
        _bias_kernel,
        grid=(1,),
        in_specs=[pl.BlockSpec(rpb2.shape, lambda i: (0, 0))],
        out_specs=pl.BlockSpec((n_rows, GRID_W * GRID_W), lambda i: (0, 0)),
        out_shape=jax.ShapeDtypeStruct((n_rows, GRID_W * GRID_W), F32),
        compiler_params=_cparams(("arbitrary",)),
        name="na_bias",
    )(rpb2)


def _na_group_geometry(variant, rows):
    ng = rows // NA_GROUP
    g = {0: 0, 1: 1, 2: ng - 1}[variant]
    strip0 = int(np.clip(g * NA_GROUP - NA_WIN_R // 2, 0, rows - NA_KROWS))
    r0 = [int(np.clip(g * NA_GROUP + i - NA_WIN_R // 2, 0, rows - NA_WIN_R)) for i in range(NA_GROUP)]
    return strip0 - g * NA_GROUP, [r - g * NA_GROUP for r in r0]


def _na_bias_tables(tc, rows):
    neg = jnp.full((NA_HEADS, GRID_W, GRID_W), NEG, F32)
    tables = []
    for variant in range(3):
        strip_rel, r0_rel = _na_group_geometry(variant, rows)
        q_rows = []
        for i in range(NA_GROUP):
            blocks = []
            for j in range(NA_KROWS):
                key_rel = strip_rel + j
                a = key_rel - i + NA_WIN_R - 1
                valid = r0_rel[i] <= key_rel < r0_rel[i] + NA_WIN_R
                blocks.append(tc[:, a] if valid else neg)
            q_rows.append(jnp.concatenate(blocks, axis=-1))
        tables.append(jnp.concatenate(q_rows, axis=1))
    return jnp.stack(tables)


def _na_kernel(q_ref, kp_ref, kc_ref, kn_ref, vp_ref, vc_ref, vn_ref, kx_ref, vx_ref, bias_ref, o_ref,
               kcat, vcat, *, rows):
    g = pl.program_id(1)
    kcat[0:NA_GTOK] = kp_ref[0]
    kcat[NA_GTOK:2 * NA_GTOK] = kc_ref[0]
    kcat[2 * NA_GTOK:3 * NA_GTOK] = kn_ref[0]
    vcat[0:NA_GTOK] = vp_ref[0]
    vcat[NA_GTOK:2 * NA_GTOK] = vc_ref[0]
    vcat[2 * NA_GTOK:3 * NA_GTOK] = vn_ref[0]
    strip0 = jnp.clip(g * NA_GROUP - NA_WIN_R // 2, 0, rows - NA_KROWS)
    start = pl.multiple_of((strip0 - g * NA_GROUP + NA_GROUP) * GRID_W, GRID_W)
    for pr in range(NA_HEADS // 2):
        ps = slice(pr * LANES, (pr + 1) * LANES)
        k_w = kcat[pl.ds(start, NA_KTOK), ps]
        v_w = vcat[pl.ds(start, NA_KTOK), ps]

        def bias_fn(sub, idx, s, pr=pr):
            return s + bias_ref[0, 2 * pr + sub] if idx == 0 else s

        o = _pair_softmax_pv(q_ref[0, :, ps], [k_w, kx_ref[0, :, ps]], [v_w, vx_ref[0, :, ps]], bias_fn)
        o_ref[0, :, ps] = o.astype(BF16)


def _na(z, zc, bias):
    b, s, _ = z.shape
    lc = zc.shape[1]
    rows = s // GRID_W
    ng = rows // NA_GROUP
    blk = (1, NA_GTOK, CB)

    def spec(col, off):
        return pl.BlockSpec(blk, lambda bi, g: (bi, jnp.clip(g + off, 0, ng - 1), col // CB))

    def bias_map(bi, g):
        return (jnp.where(g == 0, 0, jnp.where(g == ng - 1, 2, 1)), 0, 0, 0)

    return pl.pallas_call(
        functools.partial(_na_kernel, rows=rows),
        grid=(b, ng),
        in_specs=[
            spec(Z_Q, 0),
            spec(Z_K, -1), spec(Z_K, 0), spec(Z_K, 1),
            spec(Z_V, -1), spec(Z_V, 0), spec(Z_V, 1),
            pl.BlockSpec((1, lc, CB), lambda bi, g: (bi, 0, Z_K // CB)),
            pl.BlockSpec((1, lc, CB), lambda bi, g: (bi, 0, Z_V // CB)),
            pl.BlockSpec((1,) + bias.shape[1:], bias_map, pipeline_mode=pl.Buffered(1)),
        ],
        out_specs=pl.BlockSpec(blk, lambda bi, g: (bi, g, 0)),
        out_shape=jax.ShapeDtypeStruct((b, s, NA_WIDTH), BF16),
        scratch_shapes=[pltpu.VMEM((3 * NA_GTOK, CB), BF16), pltpu.VMEM((3 * NA_GTOK, CB), BF16)],
        compiler_params=_cparams(("parallel", "arbitrary")),
        name="na_attn",
    )(z, z, z, z, z, z, z, zc, zc, bias)


def _ctx_na_kernel(q_ref, k_ref, v_ref, o_ref):
    for pr in range(NA_HEADS // 2):
        ps = slice(pr * LANES, (pr + 1) * LANES)
        o = _pair_softmax_pv(q_ref[0, :, ps], [k_ref[0, :, ps]], [v_ref[0, :, ps]], lambda sub, idx, s: s)
        o_ref[0, :, ps] = o.astype(BF16)


def _ctx_na(zc):
    b, lc, _ = zc.shape
    spec = lambda col: pl.BlockSpec((1, lc, CB), lambda bi: (bi, 0, col // CB))
    return pl.pallas_call(
        _ctx_na_kernel,
        grid=(b,),
        in_specs=[spec(Z_Q), spec(Z_K), spec(Z_V)],
        out_specs=pl.BlockSpec((1, lc, NA_WIDTH), lambda bi: (bi, 0, 0)),
        out_shape=jax.ShapeDtypeStruct((b, lc, NA_WIDTH), BF16),
        compiler_params=_cparams(("parallel",)),
        name="ctx_na",
    )(zc, zc, zc)


def _pool_kernel(up_ref, u_ref, un_ref, w_ref, sc_ref, o_ref, *, seq_len):
    i = pl.program_id(1)
    tm = u_ref.shape[1]
    ext = jnp.concatenate([up_ref[0], u_ref[0], un_ref[0]], axis=0).astype(F32)
    n_ext = tm + 2 * POOL_HALO
    tg = i * tm - POOL_HALO + lax.broadcasted_iota(jnp.int32, (n_ext, 1), 0)
    ext = jnp.where((tg >= 0) & (tg < seq_len), ext, 0.0)
    t = i * tm + lax.broadcasted_iota(jnp.int32, (tm, 1), 0)

    def shifted(a, k):
        return pltpu.roll(a, (-k) % n_ext, 0)

    for gi, win in enumerate(POOL_WINDOWS):
        gs = slice(gi * POOL_GROUP, (gi + 1) * POOL_GROUP)
        a = ext[:, gs]
        wsum = a + shifted(a, -1)
        half = 1
        while 2 * half < win:
            wsum = shifted(wsum, -half) + shifted(wsum, half)
            half *= 2
        cnt = (jnp.minimum(t + win // 2, seq_len) - jnp.maximum(t - win // 2, 0)).astype(F32)
        d = wsum[POOL_HALO:POOL_HALO + tm] / cnt - a[POOL_HALO:POOL_HALO + tm]
        y = jnp.dot(d.astype(BF16), w_ref[gi], preferred_element_type=F32)
        o_ref[0, :, gs] = (y * sc_ref[:, gs]).astype(BF16)


def _pool(z, w, sc, tm):
    b, l, _ = z.shape
    tm = min(tm, l)
    hb = tm // POOL_HALO
    nhb = l // POOL_HALO
    col = Z_U // CB
    return pl.pallas_call(
        functools.partial(_pool_kernel, seq_len=l),
        grid=(b, l // tm),
        in_specs=[
            pl.BlockSpec((1, POOL_HALO, CB), lambda bi, i: (bi, jnp.maximum(i * hb - 1, 0), col)),
            pl.BlockSpec((1, tm, CB), lambda bi, i: (bi, i, col)),
            pl.BlockSpec((1, POOL_HALO, CB), lambda bi, i: (bi, jnp.minimum((i + 1) * hb, nhb - 1), col)),
            pl.BlockSpec(w.shape, lambda bi, i: (0, 0, 0)),
            pl.BlockSpec((1, POOL_WIDTH), lambda bi, i: (0, 0)),
        ],
        out_specs=pl.BlockSpec((1, tm, POOL_WIDTH), lambda bi, i: (bi, i, 0)),
        out_shape=jax.ShapeDtypeStruct((b, l, POOL_WIDTH), BF16),
        compiler_params=_cparams(("parallel", "parallel")),
        name="pool",
    )(z, z, z, w, sc)


def _merge_kernel(ona_ref, opool_ref, omla_ref, g0_ref, g1_ref, g2_ref, wb_ref, wo_ref, x_ref, gate_ref,
                  gain_ref, o_ref):
    m = None
    for br_ref, g_ref, k in ((ona_ref, g0_ref, 0), (opool_ref, g1_ref, 1), (omla_ref, g2_ref, 2)):
        proj = jnp.dot(br_ref[0], wb_ref[k], preferred_element_type=F32)
        term = g_ref[0].astype(F32) * proj
        m = term if m is None else m + term
    y = jnp.dot(m.astype(BF16), wo_ref[...], preferred_element_type=F32)
    o_ref[0] = x_ref[0] + gate_ref[0] * _rms(y, gain_ref[...])


def _merge(o_na, o_pool, o_mla, z, wb, wo, x, gate, gain, tm):
    b, l, d = x.shape
    tm = min(tm, l)
    per_batch = gate.shape[0] > 1
    mod_map = (lambda bi, i: (bi, 0, 0)) if per_batch else (lambda bi, i: (0, 0, 0))
    br = pl.BlockSpec((1, tm, BRANCH_W), lambda bi, i: (bi, i, 0))
    gspec = lambda k: pl.BlockSpec((1, tm, d), lambda bi, i: (bi, i, k))
    return pl.pallas_call(
        _merge_kernel,
        grid=(b, l // tm),
        in_specs=[
            br, br, br, gspec(0), gspec(1), gspec(2),
            pl.BlockSpec(wb.shape, lambda bi, i: (0, 0, 0)),
            pl.BlockSpec(wo.shape, lambda bi, i: (0, 0)),
            pl.BlockSpec((1, tm, d), lambda bi, i: (bi, i, 0)),
            pl.BlockSpec((1, 1, d), mod_map),
            pl.BlockSpec((1, d), lambda bi, i: (0, 0)),
        ],
        out_specs=pl.BlockSpec((1, tm, d), lambda bi, i: (bi, i, 0)),
        out_shape=jax.ShapeDtypeStruct((b, l, d), F32),
        compiler_params=_cparams(("parallel", "parallel")),
        name="merge",
    )(o_na, o_pool, o_mla, z, z, z, wb, wo, x, gate, gain)


def _gelu_tanh(x):
    return 0.5 * x * (1.0 + jnp.tanh(np.float32(np.sqrt(2.0 / np.pi)) * (x + np.float32(0.044715) * (x * x * x))))


def _ffn_kernel(xp_ref, x_ref, xn_ref, gain_ref, sc_ref, sh_ref, wa_ref, wb_ref, cwa_ref, cwb_ref, cba_ref,
                cbb_ref, wd_ref, gate_ref, gpost_ref, o_ref, h_scr, acc_scr):
    i = pl.program_id(1)
    j = pl.program_id(2)
    ni = pl.num_programs(1)
    nj = pl.num_programs(2)
    tm = x_ref.shape[1]
    n_ext = tm + 2 * CONV_HALO

    @pl.when(j == 0)
    def _():
        def norm_mod(xv):
            return _rms(xv, gain_ref[...]) * (1.0 + sc_ref[0]) + sh_ref[0]

        hp = jnp.where(i > 0, norm_mod(xp_ref[0]), 0.0)
        hn = jnp.where(i < ni - 1, norm_mod(xn_ref[0]), 0.0)
        h_scr[0:CONV_HALO] = hp.astype(BF16)
        h_scr[CONV_HALO:CONV_HALO + tm] = norm_mod(x_ref[0]).astype(BF16)
        h_scr[CONV_HALO + tm:n_ext] = hn.astype(BF16)
        acc_scr[...] = jnp.zeros(acc_scr.shape, F32)

    h = h_scr[...]

    def conv_half(w_ref, cw_ref, cb_ref):
        u = jnp.dot(h, w_ref[...], preferred_element_type=F32)
        prev = pltpu.roll(u, 1, 0)[CONV_HALO:CONV_HALO + tm]
        nxt = pltpu.roll(u, n_ext - 1, 0)[CONV_HALO:CONV_HALO + tm]
        cur = u[CONV_HALO:CONV_HALO + tm]
        return cb_ref[...] + prev * cw_ref[0:1] + cur * cw_ref[1:2] + nxt * cw_ref[2:3]

    a = conv_half(wa_ref, cwa_ref, cba_ref)
    bgate = conv_half(wb_ref, cwb_ref, cbb_ref)
    act = (_gelu_tanh(a) * bgate).astype(BF16)
    acc_scr[...] += jnp.dot(act, wd_ref[...], preferred_element_type=F32)

    @pl.when(j == nj - 1)
    def _():
        o_ref[0] = x_ref[0] + gate_ref[0] * _rms(acc_scr[...], gpost_ref[...])


def _ffn(x, gain, sc, sh, w_up, conv_w, conv_b, w_down, gate, gpost, tm, tn):
    b, l, d = x.shape
    tm = min(tm, l)
    nch = D_FF // tn
    hb = tm // CONV_HALO
    nhb = l // CONV_HALO
    per_batch = sc.shape[0] > 1
    mod_map = (lambda bi, i, j: (bi, 0, 0)) if per_batch else (lambda bi, i, j: (0, 0, 0))
    mod = pl.BlockSpec((1, 1, d), mod_map)
    vec = pl.BlockSpec((1, d), lambda bi, i, j: (0, 0))
    return pl.pallas_call(
        _ffn_kernel,
        grid=(b, l // tm, nch),
        in_specs=[
            pl.BlockSpec((1, CONV_HALO, d), lambda bi, i, j: (bi, jnp.maximum(i * hb - 1, 0), 0)),
            pl.BlockSpec((1, tm, d), lambda bi, i, j: (bi, i, 0)),
            pl.BlockSpec((1, CONV_HALO, d), lambda bi, i, j: (bi, jnp.minimum((i + 1) * hb, nhb - 1), 0)),
            vec, mod, mod,
            pl.BlockSpec((d, tn), lambda bi, i, j: (0, j)),
            pl.BlockSpec((d, tn), lambda bi, i, j: (0, nch + j)),
            pl.BlockSpec((3, tn), lambda bi, i, j: (0, j)),
            pl.BlockSpec((3, tn), lambda bi, i, j: (0, nch + j)),
            pl.BlockSpec((1, tn), lambda bi, i, j: (0, j)),
            pl.BlockSpec((1, tn), lambda bi, i, j: (0, nch + j)),
            pl.BlockSpec((tn, d), lambda bi, i, j: (j, 0)),
            mod, vec,
        ],
        out_specs=pl.BlockSpec((1, tm, d), lambda bi, i, j: (bi, i, 0)),
        out_shape=jax.ShapeDtypeStruct((b, l, d), F32),
        scratch_shapes=[pltpu.VMEM((tm + 2 * CONV_HALO, d), BF16), pltpu.VMEM((tm, d), F32)],
        compiler_params=_cparams(("parallel", "parallel", "arbitrary")),
        name="ffn",
    )(x, x, x, gain, sc, sh, w_up, w_up, conv_w, conv_w, conv_b, conv_b, w_down, gate, gpost)


_ROPE_PERM = np.concatenate([np.arange(8, 16), np.arange(0, 8), np.arange(24, 32), np.arange(16, 24)])


def _rope_tables(n_tok, rotate):
    ones = jnp.ones((n_tok, MLA_NOPE), F32)
    zeros = jnp.zeros((n_tok, MLA_NOPE), F32)
    pad1 = jnp.ones((n_tok, HEAD_PAD - MLA_NOPE - MLA_ROPE), F32)
    pad0 = jnp.zeros((n_tok, HEAD_PAD - MLA_NOPE - MLA_ROPE), F32)
    if not rotate:
        return (jnp.concatenate([ones, jnp.ones((n_tok, MLA_ROPE), F32), pad1], axis=1),
                jnp.concatenate([zeros, jnp.zeros((n_tok, MLA_ROPE), F32), pad0], axis=1))
    n_freq = MLA_ROPE // 4
    inv = ROPE_BASE ** (-jnp.arange(n_freq, dtype=F32) / n_freq)
    t = jnp.arange(n_tok, dtype=jnp.int32)
    ang_r = (t // GRID_W).astype(F32)[:, None] * inv[None, :]
    ang_c = (t % GRID_W).astype(F32)[:, None] * inv[None, :]
    cr, sr, cc, sn = jnp.cos(ang_r), jnp.sin(ang_r), jnp.cos(ang_c), jnp.sin(ang_c)
    ct = jnp.concatenate([ones, cr, cr, cc, cc, pad1], axis=1)
    st = jnp.concatenate([zeros, -sr, sr, -sn, sn, pad0], axis=1)
    return ct, st


def _prep_layer(w_in, w_uq, w_ukv, na_rpb, rows):
    d = w_in.shape[0]
    o = 0
    parts = []
    for s in (NA_WIDTH, NA_WIDTH, NA_WIDTH, POOL_WIDTH, MLA_Q_RANK, MLA_KV_RANK, MLA_ROPE, N_BRANCH * D_MODEL):
        parts.append(w_in[:, o:o + s])
        o += s
    wq, wk, wv, wu, wcq, wckv, wkr, wg = parts
    z64 = jnp.zeros((d, MLA_NOPE), F32)
    z32 = jnp.zeros((d, HEAD_PAD - MLA_NOPE - MLA_ROPE), F32)
    slab_a = jnp.concatenate([z64, wkr, z32], axis=1)
    slab_b = jnp.concatenate([z64, wkr[:, _ROPE_PERM], z32], axis=1)
    w_z = jnp.concatenate([wq * (NA_HEAD_DIM ** -0.5), wk, wv, wu, wcq, wckv, slab_a, slab_b], axis=1)

    uq = w_uq.reshape(MLA_Q_RANK, MLA_HEADS, MLA_NOPE + MLA_ROPE)
    qz64 = jnp.zeros((MLA_Q_RANK, MLA_HEADS, MLA_NOPE), F32)
    qz32 = jnp.zeros((MLA_Q_RANK, MLA_HEADS, HEAD_PAD - MLA_NOPE - MLA_ROPE), F32)
    wq_pad = jnp.concatenate([uq[..., :MLA_NOPE], uq[..., MLA_NOPE:], qz32], axis=-1)
    wq_perm = jnp.concatenate([qz64, uq[..., MLA_NOPE:][..., _ROPE_PERM], qz32], axis=-1)
    ukv = w_ukv.reshape(MLA_KV_RANK, MLA_HEADS, MLA_NOPE + MLA_V)
    kz = jnp.zeros((MLA_KV_RANK, MLA_HEADS, HEAD_PAD - MLA_NOPE), F32)
    wk_pad = jnp.concatenate([ukv[..., :MLA_NOPE], kz], axis=-1)
    wv_pad = jnp.concatenate([ukv[..., MLA_NOPE:], jnp.zeros((MLA_KV_RANK, MLA_HEADS, HEAD_PAD - MLA_V), F32)], axis=-1)

    n_rel_r = 2 * NA_WIN_R - 1
    rpb2 = jnp.zeros((LANES, LANES), F32).at[:NA_HEADS * n_rel_r, :2 * NA_WIN_C - 1].set(
        na_rpb.reshape(NA_HEADS * n_rel_r, 2 * NA_WIN_C - 1))
    tc = _bias_cols(rpb2)[:NA_HEADS * n_rel_r].reshape(NA_HEADS, n_rel_r, GRID_W, GRID_W)
    bias = _na_bias_tables(tc, rows)

    return dict(
        w_z=w_z.astype(BF16),
        w_g=wg.astype(BF16),
        wq_pad=wq_pad.reshape(MLA_Q_RANK, MLA_PAD_W).astype(BF16),
        wq_perm=wq_perm.reshape(MLA_Q_RANK, MLA_PAD_W).astype(BF16),
        wk_pad=wk_pad.reshape(MLA_KV_RANK, MLA_PAD_W).astype(BF16),
        wv_pad=wv_pad.reshape(MLA_KV_RANK, MLA_PAD_W).astype(BF16),
        bias=bias.astype(F32),
    )


TM_PROJ = 1024
TN_PROJ = 1024
TM_UP = 512
TQ_FLASH = 1024
TK_FLASH = 1024
TM_POOL = 1024
TM_MERGE = 512
TM_FFN = 512
TN_FFN = 1408


def kernel(x, c, ctx, c_ctx, w_ada, b_ada, norm_pre1, norm_post1, norm_pre2, norm_post2, w_in, na_rpb, pool_w,
           pool_scale, mla_q_norm, w_uq, mla_kv_norm, w_ukv, w_branch, w_o, w_up, conv_w, conv_b, w_down):
    b, s, d = x.shape
    lc = ctx.shape[1]
    depth = w_ada.shape[0]
    rows = s // GRID_W
    assert d == D_MODEL and s % NA_KTOK == 0 and lc % POOL_HALO == 0

    n_mod = -(-(b + 1) // 8) * 8
    cvec = jnp.zeros((n_mod, d), F32).at[:b].set(c).at[b].set(c_ctx)
    mod = _ada(cvec, w_ada, b_ada)

    ct_l, st_l = _rope_tables(s, True)
    ct_c, st_c = _rope_tables(lc, False)
    ones = jnp.zeros((MLA_HEADS, HEAD_PAD), F32).at[:, MLA_V].set(1.0).reshape(1, MLA_PAD_W)

    xc = ctx
    for l in range(depth):
        last = l == depth - 1
        p = _prep_layer(w_in[l], w_uq[l], w_ukv[l], na_rpb[l], rows)
        m6 = mod[l].reshape(n_mod, 6, d)
        lat = [m6[:b, k][:, None, :] for k in range(6)]
        cx = [m6[b:b + 1, k][:, None, :] for k in range(6)]
        row = lambda v: v.reshape(1, -1)
        g_pre1, g_post1, g_pre2, g_post2 = row(norm_pre1[l]), row(norm_post1[l]), row(norm_pre2[l]), row(norm_post2[l])
        qg, kvg = row(mla_q_norm[l]), row(mla_kv_norm[l])
        pw = pool_w[l].astype(BF16)
        psc = row(pool_scale[l])
        wb = w_branch[l].astype(BF16)
        wo = w_o[l].astype(BF16)
        wup = w_up[l].astype(BF16)
        wdn = w_down[l].astype(BF16)
        cw = conv_w[l]
        cb = row(conv_b[l])

        z, gz = _in_proj(x, g_pre1, lat[1], lat[0], p["w_z"], p["w_g"], TM_PROJ, TN_PROJ)
        zc, gzc = _in_proj(xc, g_pre1, cx[1], cx[0], p["w_z"], p["w_g"], TM_PROJ, TN_PROJ)
        mla_w = (p["wq_pad"], p["wq_perm"], p["wk_pad"], p["wv_pad"], ones)
        q_l, k_l, v_l = _mla_up(z, qg, kvg, ct_l, st_l, *mla_w, TM_UP)
        q_c, k_c, v_c = _mla_up(zc, qg, kvg, ct_c, st_c, *mla_w, TM_UP)

        o_na = _na(z, zc, p["bias"])
        o_pool = _pool(z, pw, psc, TM_POOL)
        o_mla = _flash(q_l, k_c, v_c, k_l, v_l, TQ_FLASH, TK_FLASH)
        x = _merge(o_na, o_pool, o_mla, gz, wb, wo, x, lat[2], g_post1, TM_MERGE)
        x = _ffn(x, g_pre2, lat[4], lat[3], wup, cw, cb, wdn, lat[5], g_post2, TM_FFN, TN_FFN)

        if not last:
            oc_na = _ctx_na(zc)
            oc_pool = _pool(zc, pw, psc, TM_POOL)
            oc_mla = _flash(q_c, k_c, v_c, None, None, TQ_FLASH, TK_FLASH)
            xc = _merge(oc_na, oc_pool, oc_mla, gzc, wb, wo, xc, cx[2], g_post1, TM_MERGE)
            xc = _ffn(xc, g_pre2, cx[4], cx[3], wup, cw, cb, wdn, cx[5], g_post2, TM_FFN, TN_FFN)
    return x
```

```python
import functools

import jax
import jax.numpy as jnp
import numpy as np
from jax import lax
from jax.experimental import pallas as pl
from jax.experimental.pallas import tpu as pltpu

F32 = jnp.float32
BF16 = jnp.bfloat16

D_MODEL = 1024
GRID_W = 64
EPS = 1e-6
NA_HEADS = 8
NA_HEAD_DIM = 64
NA_WIDTH = NA_HEADS * NA_HEAD_DIM
NA_WIN_R = 8
NA_WIN_C = 16
POOL_WINDOWS = (2, 4, 8, 16)
POOL_GROUP = 128
POOL_WIDTH = POOL_GROUP * len(POOL_WINDOWS)
POOL_HALO = 16
MLA_HEADS = 8
MLA_NOPE = 64
MLA_ROPE = 32
MLA_V = 64
MLA_Q_RANK = 512
MLA_KV_RANK = 256
ROPE_BASE = 10000.0
N_BRANCH = 3
BRANCH_W = 512
D_FF = 2816
CONV_HALO = 8

LANES = 128
HEAD_PAD = LANES
MLA_PAD_W = MLA_HEADS * HEAD_PAD

Z_Q = 0
Z_K = Z_Q + NA_WIDTH
Z_V = Z_K + NA_WIDTH
Z_U = Z_V + NA_WIDTH
Z_CQ = Z_U + POOL_WIDTH
Z_KV = Z_CQ + MLA_Q_RANK
Z_COLS = Z_KV + 512
G_COLS = N_BRANCH * D_MODEL
assert G_COLS == Z_COLS
CB = 512

NEG = -1e30
VMEM_LIMIT = 56 * 1024 * 1024


def _cparams(sem):
    return pltpu.CompilerParams(dimension_semantics=sem, vmem_limit_bytes=VMEM_LIMIT)


def _rms(x, gain):
    return x * lax.rsqrt(jnp.mean(x * x, axis=-1, keepdims=True) + EPS) * gain


def _ada_kernel(c_ref, w_ref, b_ref, o_ref):
    c = c_ref[...]
    s = c * jax.nn.sigmoid(c)
    o_ref[0] = jnp.dot(s.astype(BF16), w_ref[0].astype(BF16), preferred_element_type=F32) + b_ref[0]


def _ada(cvec, w_ada, b_ada):
    depth, d, n = w_ada.shape
    rows = cvec.shape[0]
    tn = 1536
    return pl.pallas_call(
        _ada_kernel,
        grid=(depth, n // tn),
        in_specs=[
            pl.BlockSpec((rows, d), lambda l, j: (0, 0)),
            pl.BlockSpec((1, d, tn), lambda l, j: (l, 0, j)),
            pl.BlockSpec((1, 1, tn), lambda l, j: (l, 0, j)),
        ],
        out_specs=pl.BlockSpec((1, rows, tn), lambda l, j: (l, 0, j)),
        out_shape=jax.ShapeDtypeStruct((depth, rows, n), F32),
        compiler_params=_cparams(("arbitrary", "arbitrary")),
        name="ada",
    )(cvec, w_ada, b_ada.reshape(depth, 1, n))


def _in_proj_kernel(x_ref, gain_ref, sc_ref, sh_ref, wz_ref, wg_ref, z_ref, g_ref, h_scr):
    j = pl.program_id(2)

    @pl.when(j == 0)
    def _():
        h = _rms(x_ref[0], gain_ref[...]) * (1.0 + sc_ref[0]) + sh_ref[0]
        h_scr[...] = h.astype(BF16)

    h = h_scr[...]
    g_ref[0] = jax.nn.sigmoid(jnp.dot(h, wg_ref[...], preferred_element_type=F32)).astype(BF16)
    z_ref[0] = jnp.dot(h, wz_ref[...], preferred_element_type=F32).astype(BF16)


def _in_proj(x, gain, sc, sh, wz, wg, tm, tn):
    b, l, d = x.shape
    tm = min(tm, l)
    per_batch = sc.shape[0] > 1
    mod_map = (lambda bi, i, j: (bi, 0, 0)) if per_batch else (lambda bi, i, j: (0, 0, 0))
    wspec = pl.BlockSpec((d, tn), lambda bi, i, j: (0, j))
    ospec = pl.BlockSpec((1, tm, tn), lambda bi, i, j: (bi, i, j))
    oshape = jax.ShapeDtypeStruct((b, l, Z_COLS), BF16)
    return pl.pallas_call(
        _in_proj_kernel,
        grid=(b, l // tm, Z_COLS // tn),
        in_specs=[
            pl.BlockSpec((1, tm, d), lambda bi, i, j: (bi, i, 0)),
            pl.BlockSpec((1, d), lambda bi, i, j: (0, 0)),
            pl.BlockSpec((1, 1, d), mod_map),
            pl.BlockSpec((1, 1, d), mod_map),
            wspec, wspec,
        ],
        out_specs=[ospec, ospec],
        out_shape=[oshape, oshape],
        scratch_shapes=[pltpu.VMEM((tm, d), BF16)],
        compiler_params=_cparams(("parallel", "parallel", "arbitrary")),
        name="in_proj",
    )(x, gain, sc, sh, wz, wg)


def _mla_up_kernel(cq_ref, kv_ref, qg_ref, kvg_ref, ct_ref, st_ref, wq_ref, wqp_ref, wk_ref, wv_ref,
                   ones_ref, q_out, k_out, v_out, *, scale):
    cqn = _rms(cq_ref[0].astype(F32), qg_ref[...]).astype(BF16)
    kvb = kv_ref[0].astype(F32)
    kvn = _rms(kvb[:, :MLA_KV_RANK], kvg_ref[...]).astype(BF16)
    ct = ct_ref[...]
    st = st_ref[...]
    kr = kvb[:, MLA_KV_RANK:MLA_KV_RANK + LANES] * ct + kvb[:, MLA_KV_RANK + LANES:] * st
    qm = jnp.dot(cqn, wq_ref[...], preferred_element_type=F32)
    qp = jnp.dot(cqn, wqp_ref[...], preferred_element_type=F32)
    kn = jnp.dot(kvn, wk_ref[...], preferred_element_type=F32)
    cts = ct * scale
    sts = st * scale
    for h in range(MLA_HEADS):
        hs = slice(h * HEAD_PAD, (h + 1) * HEAD_PAD)
        q_out[0, :, hs] = (qm[:, hs] * cts + qp[:, hs] * sts).astype(BF16)
        k_out[0, :, hs] = (kn[:, hs] + kr).astype(BF16)
    v_out[0] = (jnp.dot(kvn, wv_ref[...], preferred_element_type=F32) + ones_ref[...]).astype(BF16)


def _mla_up(z, qg, kvg, ct, st, wq, wqp, wk, wv, ones, tm):
    b, l, _ = z.shape
    tm = min(tm, l)
    full = lambda shape: pl.BlockSpec(shape, lambda bi, i: (0,) * len(shape))
    out = jax.ShapeDtypeStruct((b, l, MLA_PAD_W), BF16)
    ospec = pl.BlockSpec((1, tm, MLA_PAD_W), lambda bi, i: (bi, i, 0))
    return pl.pallas_call(
        functools.partial(_mla_up_kernel, scale=float((MLA_NOPE + MLA_ROPE) ** -0.5 * np.log2(np.e))),
        grid=(b, l // tm),
        in_specs=[
            pl.BlockSpec((1, tm, CB), lambda bi, i: (bi, i, Z_CQ // CB)),
            pl.BlockSpec((1, tm, CB), lambda bi, i: (bi, i, Z_KV // CB)),
            full((1, MLA_Q_RANK)),
            full((1, MLA_KV_RANK)),
            pl.BlockSpec((tm, LANES), lambda bi, i: (i, 0)),
            pl.BlockSpec((tm, LANES), lambda bi, i: (i, 0)),
            full((MLA_Q_RANK, MLA_PAD_W)),
            full((MLA_Q_RANK, MLA_PAD_W)),
            full((MLA_KV_RANK, MLA_PAD_W)),
            full((MLA_KV_RANK, MLA_PAD_W)),
            full((1, MLA_PAD_W)),
        ],
        out_specs=[ospec, ospec, ospec],
        out_shape=[out, out, out],
        compiler_params=_cparams(("parallel", "parallel")),
        name="mla_up",
    )(z, z, qg, kvg, ct, st, wq, wqp, wk, wv, ones)


def _flash_kernel(*refs, has_latent):
    if has_latent:
        q_ref, kc_ref, vc_ref, k_ref, v_ref, o_ref, m_scr, acc_scr = refs
    else:
        q_ref, kc_ref, vc_ref, o_ref, m_scr, acc_scr = refs
        k_ref = v_ref = None
    kk = pl.program_id(2)
    nk = pl.num_programs(2)

    def attend(h, kblk, vblk):
        hs = slice(h * HEAD_PAD, (h + 1) * HEAD_PAD)
        q = q_ref[0, :, hs]
        s = lax.dot_general(q, kblk[0, :, hs], (((1,), (1,)), ((), ())), preferred_element_type=F32)
        m_prev = m_scr[h]
        m_new = jnp.maximum(m_prev, jnp.max(s, axis=-1, keepdims=True))
        alpha = jnp.exp2(m_prev - m_new)
        p = jnp.exp2(s - jnp.concatenate([m_new] * (s.shape[1] // LANES), axis=1))
        acc_scr[h] = alpha * acc_scr[h] + jnp.dot(p.astype(BF16), vblk[0, :, hs], preferred_element_type=F32)
        m_scr[h] = m_new

    @pl.when(kk == 0)
    def _():
        m_scr[...] = jnp.full(m_scr.shape, NEG, F32)
        acc_scr[...] = jnp.zeros(acc_scr.shape, F32)
        for h in range(MLA_HEADS):
            attend(h, kc_ref, vc_ref)

    if has_latent:
        for h in range(MLA_HEADS):
            attend(h, k_ref, v_ref)

    @pl.when(kk == nk - 1)
    def _():
        for h in range(MLA_HEADS):
            acc = acc_scr[h]
            o_ref[0, :, h * MLA_V:(h + 1) * MLA_V] = (acc[:, :MLA_V] / acc[:, MLA_V:MLA_V + 1]).astype(BF16)


def _flash(q, kc, vc, k, v, tq, tk):
    b, lq, _ = q.shape
    lc = kc.shape[1]
    tq = min(tq, lq)
    has_latent = k is not None
    in_specs = [
        pl.BlockSpec((1, tq, MLA_PAD_W), lambda bi, i, kk: (bi, i, 0)),
        pl.BlockSpec((1, lc, MLA_PAD_W), lambda bi, i, kk: (bi, 0, 0)),
        pl.BlockSpec((1, lc, MLA_PAD_W), lambda bi, i, kk: (bi, 0, 0)),
    ]
    args = [q, kc, vc]
    nk = 1
    if has_latent:
        tk = min(tk, k.shape[1])
        nk = k.shape[1] // tk
        in_specs += [pl.BlockSpec((1, tk, MLA_PAD_W), lambda bi, i, kk: (bi, kk, 0))] * 2
        args += [k, v]
    return pl.pallas_call(
        functools.partial(_flash_kernel, has_latent=has_latent),
        grid=(b, lq // tq, nk),
        in_specs=in_specs,
        out_specs=pl.BlockSpec((1, tq, MLA_HEADS * MLA_V), lambda bi, i, kk: (bi, i, 0)),
        out_shape=jax.ShapeDtypeStruct((b, lq, MLA_HEADS * MLA_V), BF16),
        scratch_shapes=[pltpu.VMEM((MLA_HEADS, tq, LANES), F32), pltpu.VMEM((MLA_HEADS, tq, LANES), F32)],
        compiler_params=_cparams(("parallel", "parallel", "arbitrary")),
        name="mla_flash" if has_latent else "mla_ctx",
    )(*args)


NA_GROUP = 8
NA_GTOK = NA_GROUP * GRID_W


def _pair_softmax_pv(qp, k_list, v_list, bias_fn):
    lane = lax.broadcasted_iota(jnp.int32, qp.shape, 1)
    outs = []
    for sub in range(2):
        in_head = (lane >= sub * NA_HEAD_DIM) & (lane < (sub + 1) * NA_HEAD_DIM)
        qm = jnp.where(in_head, qp, jnp.zeros_like(qp))
        s_list = []
        for idx, kb in enumerate(k_list):
            s = lax.dot_general(qm, kb, (((1,), (1,)), ((), ())), preferred_element_type=F32)
            s_list.append(bias_fn(sub, idx, s))
        m = s_list[0].max(axis=-1, keepdims=True)
        for s in s_list[1:]:
            m = jnp.maximum(m, s.max(axis=-1, keepdims=True))
        o = None
        l = None
        for s, vb in zip(s_list, v_list):
            p = jnp.exp(s - m)
            ls = p.sum(axis=-1, keepdims=True)
            os_ = jnp.dot(p.astype(BF16), vb, preferred_element_type=F32)
            o = os_ if o is None else o + os_
            l = ls if l is None else l + ls
        outs.append(o / l)
    return jnp.where(lane < NA_HEAD_DIM, outs[0], outs[1])


NA_KROWS = 2 * NA_GROUP
NA_KTOK = NA_KROWS * GRID_W
NA_QCHUNK = 512


def _bias_kernel(rpb_ref, o_ref):
    n = GRID_W * GRID_W
    j = lax.broadcasted_iota(jnp.int32, (1, n), 1)
    qc = j // GRID_W
    kc = j % GRID_W
    win0 = jnp.clip(qc - NA_WIN_C // 2, 0, GRID_W - NA_WIN_C)
    in_win = (kc >= win0) & (kc < win0 + NA_WIN_C)
    rel = jnp.clip(kc - qc + NA_WIN_C - 1, 0, 2 * NA_WIN_C - 2)
    acc = jnp.zeros(o_ref.shape, F32)
    for b in range(2 * NA_WIN_C - 1):
        acc = jnp.where(rel == b, rpb_ref[:, b:b + 1], acc)
    o_ref[...] = jnp.where(in_win, acc, NEG)


def _bias_cols(rpb2):
    n_rows = rpb2.shape[0]
    return pl.pallas_call(
        _bias_kernel,
        grid=(1,),
        in_specs=[pl.BlockSpec(rpb2.shape, lambda i: (0, 0))],
        out_specs=pl.BlockSpec((n_rows, GRID_W * GRID_W), lambda i: (0, 0)),
        out_shape=jax.ShapeDtypeStruct((n_rows, GRID_W * GRID_W), F32),
        compiler_params=_cparams(("arbitrary",)),
        name="na_bias",
    )(rpb2)


def _na_group_geometry(variant, rows):
    ng = rows // NA_GROUP
    g = {0: 0, 1: 1, 2: ng - 1}[variant]
    strip0 = int(np.clip(g * NA_GROUP - NA_WIN_R // 2, 0, rows - NA_KROWS))
    r0 = [int(np.clip(g * NA_GROUP + i - NA_WIN_R // 2, 0, rows - NA_WIN_R)) for i in range(NA_GROUP)]
    return strip0 - g * NA_GROUP, [r - g * NA_GROUP for r in r0]


def _na_kernel(q_ref, kp_ref, kc_ref, kn_ref, vp_ref, vc_ref, vn_ref, kx_ref, vx_ref, bias_ref, o_ref,
               kcat, vcat, *, rows):
    g = pl.program_id(1)
    kcat[0:NA_GTOK] = kp_ref[0]
    kcat[NA_GTOK:2 * NA_GTOK] = kc_ref[0]
    kcat[2 * NA_GTOK:3 * NA_GTOK] = kn_ref[0]
    vcat[0:NA_GTOK] = vp_ref[0]
    vcat[NA_GTOK:2 * NA_GTOK] = vc_ref[0]
    vcat[2 * NA_GTOK:3 * NA_GTOK] = vn_ref[0]
    strip0 = jnp.clip(g * NA_GROUP - NA_WIN_R // 2, 0, rows - NA_KROWS)
    start = pl.multiple_of((strip0 - g * NA_GROUP + NA_GROUP) * GRID_W, GRID_W)
    for pr in range(NA_HEADS // 2):
        ps = slice(pr * LANES, (pr + 1) * LANES)
        k_w = kcat[pl.ds(start, NA_KTOK), ps]
        v_w = vcat[pl.ds(start, NA_KTOK), ps]
        for c in range(NA_GTOK // NA_QCHUNK):
            rs = slice(c * NA_QCHUNK, (c + 1) * NA_QCHUNK)

            def bias_fn(sub, idx, s, pr=pr, rs=rs):
                return s + bias_ref[0, 2 * pr + sub, rs, :] if idx == 0 else s

            o = _pair_softmax_pv(q_ref[0, rs, ps], [k_w, kx_ref[0, :, ps]], [v_w, vx_ref[0, :, ps]], bias_fn)
            o_ref[0, rs, ps] = o.astype(BF16)


def _na(z, zc, bias):
    b, s, _ = z.shape
    lc = zc.shape[1]
    rows = s // GRID_W
    ng = rows // NA_GROUP
    blk = (1, NA_GTOK, CB)

    def spec(col, off):
        return pl.BlockSpec(blk, lambda bi, g: (bi, jnp.clip(g + off, 0, ng - 1), col // CB))

    def bias_map(bi, g):
        return (jnp.where(g == 0, 0, jnp.where(g == ng - 1, 2, 1)), 0, 0, 0)

    return pl.pallas_call(
        functools.partial(_na_kernel, rows=rows),
        grid=(b, ng),
        in_specs=[
            spec(Z_Q, 0),
            spec(Z_K, -1), spec(Z_K, 0), spec(Z_K, 1),
            spec(Z_V, -1), spec(Z_V, 0), spec(Z_V, 1),
            pl.BlockSpec((1, lc, CB), lambda bi, g: (bi, 0, Z_K // CB)),
            pl.BlockSpec((1, lc, CB), lambda bi, g: (bi, 0, Z_V // CB)),
            pl.BlockSpec((1,) + bias.shape[1:], bias_map, pipeline_mode=pl.Buffered(1)),
        ],
        out_specs=pl.BlockSpec(blk, lambda bi, g: (bi, g, 0)),
        out_shape=jax.ShapeDtypeStruct((b, s, NA_WIDTH), BF16),
        scratch_shapes=[pltpu.VMEM((3 * NA_GTOK, CB), BF16), pltpu.VMEM((3 * NA_GTOK, CB), BF16)],
        compiler_params=_cparams(("parallel", "arbitrary")),
        name="na_attn",
    )(z, z, z, z, z, z, z, zc, zc, bias)


def _ctx_na_kernel(q_ref, k_ref, v_ref, o_ref):
    for pr in range(NA_HEADS // 2):
        ps = slice(pr * LANES, (pr + 1) * LANES)
        o = _pair_softmax_pv(q_ref[0, :, ps], [k_ref[0, :, ps]], [v_ref[0, :, ps]], lambda sub, idx, s: s)
        o_ref[0, :, ps] = o.astype(BF16)


def _ctx_na(zc):
    b, lc, _ = zc.shape
    spec = lambda col: pl.BlockSpec((1, lc, CB), lambda bi: (bi, 0, col // CB))
    return pl.pallas_call(
        _ctx_na_kernel,
        grid=(b,),
        in_specs=[spec(Z_Q), spec(Z_K), spec(Z_V)],
        out_specs=pl.BlockSpec((1, lc, NA_WIDTH), lambda bi: (bi, 0, 0)),
        out_shape=jax.ShapeDtypeStruct((b, lc, NA_WIDTH), BF16),
        compiler_params=_cparams(("parallel",)),
        name="ctx_na",
    )(zc, zc, zc)


def _pool_kernel(up_ref, u_ref, un_ref, w_ref, sc_ref, o_ref, *, seq_len):
    i = pl.program_id(1)
    tm = u_ref.shape[1]
    ext = jnp.concatenate([up_ref[0], u_ref[0], un_ref[0]], axis=0).astype(F32)
    n_ext = tm + 2 * POOL_HALO
    tg = i * tm - POOL_HALO + lax.broadcasted_iota(jnp.int32, (n_ext, 1), 0)
    ext = jnp.where((tg >= 0) & (tg < seq_len), ext, 0.0)
    t = i * tm + lax.broadcasted_iota(jnp.int32, (tm, 1), 0)

    def shifted(a, k):
        return pltpu.roll(a, (-k) % n_ext, 0)

    for gi, win in enumerate(POOL_WINDOWS):
        gs = slice(gi * POOL_GROUP, (gi + 1) * POOL_GROUP)
        a = ext[:, gs]
        wsum = a + shifted(a, -1)
        half = 1
        while 2 * half < win:
            wsum = shifted(wsum, -half) + shifted(wsum, half)
            half *= 2
        cnt = (jnp.minimum(t + win // 2, seq_len) - jnp.maximum(t - win // 2, 0)).astype(F32)
        d = wsum[POOL_HALO:POOL_HALO + tm] / cnt - a[POOL_HALO:POOL_HALO + tm]
        y = jnp.dot(d.astype(BF16), w_ref[gi], preferred_element_type=F32)
        o_ref[0, :, gs] = (y * sc_ref[:, gs]).astype(BF16)


def _pool(z, w, sc, tm):
    b, l, _ = z.shape
    tm = min(tm, l)
    hb = tm // POOL_HALO
    nhb = l // POOL_HALO
    col = Z_U // CB
    return pl.pallas_call(
        functools.partial(_pool_kernel, seq_len=l),
        grid=(b, l // tm),
        in_specs=[
            pl.BlockSpec((1, POOL_HALO, CB), lambda bi, i: (bi, jnp.maximum(i * hb - 1, 0), col)),
            pl.BlockSpec((1, tm, CB), lambda bi, i: (bi, i, col)),
            pl.BlockSpec((1, POOL_HALO, CB), lambda bi, i: (bi, jnp.minimum((i + 1) * hb, nhb - 1), col)),
            pl.BlockSpec(w.shape, lambda bi, i: (0, 0, 0)),
            pl.BlockSpec((1, POOL_WIDTH), lambda bi, i: (0, 0)),
        ],
        out_specs=pl.BlockSpec((1, tm, POOL_WIDTH), lambda bi, i: (bi, i, 0)),
        out_shape=jax.ShapeDtypeStruct((b, l, POOL_WIDTH), BF16),
        compiler_params=_cparams(("parallel", "parallel")),
        name="pool",
    )(z, z, z, w, sc)


def _merge_kernel(ona_ref, opool_ref, omla_ref, g0_ref, g1_ref, g2_ref, wb_ref, wo_ref, x_ref, gate_ref,
                  gain_ref, o_ref):
    m = None
    for br_ref, g_ref, k in ((ona_ref, g0_ref, 0), (opool_ref, g1_ref, 1), (omla_ref, g2_ref, 2)):
        proj = jnp.dot(br_ref[0], wb_ref[k], preferred_element_type=F32)
        term = g_ref[0].astype(F32) * proj
        m = term if m is None else m + term
    y = jnp.dot(m.astype(BF16), wo_ref[...], preferred_element_type=F32)
    o_ref[0] = x_ref[0] + gate_ref[0] * _rms(y, gain_ref[...])


def _merge(o_na, o_pool, o_mla, z, wb, wo, x, gate, gain, tm):
    b, l, d = x.shape
    tm = min(tm, l)
    per_batch = gate.shape[0] > 1
    mod_map = (lambda bi, i: (bi, 0, 0)) if per_batch else (lambda bi, i: (0, 0, 0))
    br = pl.BlockSpec((1, tm, BRANCH_W), lambda bi, i: (bi, i, 0))
    gspec = lambda k: pl.BlockSpec((1, tm, d), lambda bi, i: (bi, i, k))
    return pl.pallas_call(
        _merge_kernel,
        grid=(b, l // tm),
        in_specs=[
            br, br, br, gspec(0), gspec(1), gspec(2),
            pl.BlockSpec(wb.shape, lambda bi, i: (0, 0, 0)),
            pl.BlockSpec(wo.shape, lambda bi, i: (0, 0)),
            pl.BlockSpec((1, tm, d), lambda bi, i: (bi, i, 0)),
            pl.BlockSpec((1, 1, d), mod_map),
            pl.BlockSpec((1, d), lambda bi, i: (0, 0)),
        ],
        out_specs=pl.BlockSpec((1, tm, d), lambda bi, i: (bi, i, 0)),
        out_shape=jax.ShapeDtypeStruct((b, l, d), F32),
        compiler_params=_cparams(("parallel", "parallel")),
        name="merge",
    )(o_na, o_pool, o_mla, z, z, z, wb, wo, x, gate, gain)


def _gelu_tanh(x):
    return 0.5 * x * (1.0 + jnp.tanh(np.float32(np.sqrt(2.0 / np.pi)) * (x + np.float32(0.044715) * (x * x * x))))


def _ffn_kernel(xp_ref, x_ref, xn_ref, gain_ref, sc_ref, sh_ref, wa_ref, wb_ref, cwa_ref, cwb_ref, cba_ref,
                cbb_ref, wd_ref, gate_ref, gpost_ref, o_ref, h_scr, acc_scr):
    i = pl.program_id(1)
    j = pl.program_id(2)
    ni = pl.num_programs(1)
    nj = pl.num_programs(2)
    tm = x_ref.shape[1]
    n_ext = tm + 2 * CONV_HALO

    @pl.when(j == 0)
    def _():
        def norm_mod(xv):
            return _rms(xv, gain_ref[...]) * (1.0 + sc_ref[0]) + sh_ref[0]

        hp = jnp.where(i > 0, norm_mod(xp_ref[0]), 0.0)
        hn = jnp.where(i < ni - 1, norm_mod(xn_ref[0]), 0.0)
        h_scr[0:CONV_HALO] = hp.astype(BF16)
        h_scr[CONV_HALO:CONV_HALO + tm] = norm_mod(x_ref[0]).astype(BF16)
        h_scr[CONV_HALO + tm:n_ext] = hn.astype(BF16)
        acc_scr[...] = jnp.zeros(acc_scr.shape, F32)

    h = h_scr[...]

    def conv_half(w_ref, cw_ref, cb_ref):
        u = jnp.dot(h, w_ref[...], preferred_element_type=F32)
        prev = pltpu.roll(u, 1, 0)[CONV_HALO:CONV_HALO + tm]
        nxt = pltpu.roll(u, n_ext - 1, 0)[CONV_HALO:CONV_HALO + tm]
        cur = u[CONV_HALO:CONV_HALO + tm]
        return cb_ref[...] + prev * cw_ref[0:1] + cur * cw_ref[1:2] + nxt * cw_ref[2:3]

    a = conv_half(wa_ref, cwa_ref, cba_ref)
    bgate = conv_half(wb_ref, cwb_ref, cbb_ref)
    act = (_gelu_tanh(a) * bgate).astype(BF16)
    acc_scr[...] += jnp.dot(act, wd_ref[...], preferred_element_type=F32)

    @pl.when(j == nj - 1)
    def _():
        o_ref[0] = x_ref[0] + gate_ref[0] * _rms(acc_scr[...], gpost_ref[...])


def _ffn(x, gain, sc, sh, w_up, conv_w, conv_b, w_down, gate, gpost, tm, tn):
    b, l, d = x.shape
    tm = min(tm, l)
    nch = D_FF // tn
    hb = tm // CONV_HALO
    nhb = l // CONV_HALO
    per_batch = sc.shape[0] > 1
    mod_map = (lambda bi, i, j: (bi, 0, 0)) if per_batch else (lambda bi, i, j: (0, 0, 0))
    mod = pl.BlockSpec((1, 1, d), mod_map)
    vec = pl.BlockSpec((1, d), lambda bi, i, j: (0, 0))
    return pl.pallas_call(
        _ffn_kernel,
        grid=(b, l // tm, nch),
        in_specs=[
            pl.BlockSpec((1, CONV_HALO, d), lambda bi, i, j: (bi, jnp.maximum(i * hb - 1, 0), 0)),
            pl.BlockSpec((1, tm, d), lambda bi, i, j: (bi, i, 0)),
            pl.BlockSpec((1, CONV_HALO, d), lambda bi, i, j: (bi, jnp.minimum((i + 1) * hb, nhb - 1), 0)),
            vec, mod, mod,
            pl.BlockSpec((d, tn), lambda bi, i, j: (0, j)),
            pl.BlockSpec((d, tn), lambda bi, i, j: (0, nch + j)),
            pl.BlockSpec((3, tn), lambda bi, i, j: (0, j)),
            pl.BlockSpec((3, tn), lambda bi, i, j: (0, nch + j)),
            pl.BlockSpec((1, tn), lambda bi, i, j: (0, j)),
            pl.BlockSpec((1, tn), lambda bi, i, j: (0, nch + j)),
            pl.BlockSpec((tn, d), lambda bi, i, j: (j, 0)),
            mod, vec,
        ],
        out_specs=pl.BlockSpec((1, tm, d), lambda bi, i, j: (bi, i, 0)),
        out_shape=jax.ShapeDtypeStruct((b, l, d), F32),
        scratch_shapes=[pltpu.VMEM((tm + 2 * CONV_HALO, d), BF16), pltpu.VMEM((tm, d), F32)],
        compiler_params=_cparams(("parallel", "parallel", "arbitrary")),
        name="ffn",
    )(x, x, x, gain, sc, sh, w_up, w_up, conv_w, conv_w, conv_b, conv_b, w_down, gate, gpost)


_ROPE_PERM = np.concatenate([np.arange(8, 16), np.arange(0, 8), np.arange(24, 32), np.arange(16, 24)])


def _rope_tables(n_tok, rotate):
    ones = jnp.ones((n_tok, MLA_NOPE), F32)
    zeros = jnp.zeros((n_tok, MLA_NOPE), F32)
    pad1 = jnp.ones((n_tok, HEAD_PAD - MLA_NOPE - MLA_ROPE), F32)
    pad0 = jnp.zeros((n_tok, HEAD_PAD - MLA_NOPE - MLA_ROPE), F32)
    if not rotate:
        return (jnp.concatenate([ones, jnp.ones((n_tok, MLA_ROPE), F32), pad1], axis=1),
                jnp.concatenate([zeros, jnp.zeros((n_tok, MLA_ROPE), F32), pad0], axis=1))
    n_freq = MLA_ROPE // 4
    inv = ROPE_BASE ** (-jnp.arange(n_freq, dtype=F32) / n_freq)
    t = jnp.arange(n_tok, dtype=jnp.int32)
    ang_r = (t // GRID_W).astype(F32)[:, None] * inv[None, :]
    ang_c = (t % GRID_W).astype(F32)[:, None] * inv[None, :]
    cr, sr, cc, sn = jnp.cos(ang_r), jnp.sin(ang_r), jnp.cos(ang_c), jnp.sin(ang_c)
    ct = jnp.concatenate([ones, cr, cr, cc, cc, pad1], axis=1)
    st = jnp.concatenate([zeros, -sr, sr, -sn, sn, pad0], axis=1)
    return ct, st


def _prep_all(w_in, w_uq, w_ukv, na_rpb, rows):
    depth, d, _ = w_in.shape
    o = 0
    parts = []
    for s in (NA_WIDTH, NA_WIDTH, NA_WIDTH, POOL_WIDTH, MLA_Q_RANK, MLA_KV_RANK, MLA_ROPE, N_BRANCH * D_MODEL):
        parts.append(w_in[..., o:o + s])
        o += s
    wq, wk, wv, wu, wcq, wckv, wkr, wg = parts
    z64 = jnp.zeros((depth, d, MLA_NOPE), F32)
    z32 = jnp.zeros((depth, d, HEAD_PAD - MLA_NOPE - MLA_ROPE), F32)
    w_z = jnp.concatenate([wq * (NA_HEAD_DIM ** -0.5), wk, wv, wu, wcq, wckv,
                           z64, wkr, z32, z64, wkr[..., _ROPE_PERM], z32], axis=-1)

    uq = w_uq.reshape(depth, MLA_Q_RANK, MLA_HEADS, MLA_NOPE + MLA_ROPE)
    qz64 = jnp.zeros((depth, MLA_Q_RANK, MLA_HEADS, MLA_NOPE), F32)
    qz32 = jnp.zeros((depth, MLA_Q_RANK, MLA_HEADS, HEAD_PAD - MLA_NOPE - MLA_ROPE), F32)
    wq_pad = jnp.concatenate([uq, qz32], axis=-1)
    wq_perm = jnp.concatenate([qz64, uq[..., MLA_NOPE:][..., _ROPE_PERM], qz32], axis=-1)
    ukv = w_ukv.reshape(depth, MLA_KV_RANK, MLA_HEADS, MLA_NOPE + MLA_V)
    kz = jnp.zeros((depth, MLA_KV_RANK, MLA_HEADS, HEAD_PAD - MLA_NOPE), F32)
    wk_pad = jnp.concatenate([ukv[..., :MLA_NOPE], kz], axis=-1)
    wv_pad = jnp.concatenate([ukv[..., MLA_NOPE:], kz], axis=-1)

    n_rel_r = 2 * NA_WIN_R - 1
    n_rel_c = 2 * NA_WIN_C - 1
    n_tab = depth * NA_HEADS * n_rel_r
    n_pad = -(-n_tab // 8) * 8
    rpb2 = jnp.pad(na_rpb.reshape(n_tab, n_rel_c), ((0, n_pad - n_tab), (0, LANES - n_rel_c)))
    tc = _bias_cols(rpb2)[:n_tab].reshape(depth, NA_HEADS, n_rel_r, GRID_W, GRID_W)
    a_idx = np.zeros((3, NA_GROUP, NA_KROWS), np.int32)
    valid = np.zeros((3, NA_GROUP, NA_KROWS), bool)
    for variant in range(3):
        strip_rel, r0_rel = _na_group_geometry(variant, rows)
        for i in range(NA_GROUP):
            for j in range(NA_KROWS):
                key_rel = strip_rel + j
                valid[variant, i, j] = r0_rel[i] <= key_rel < r0_rel[i] + NA_WIN_R
                a_idx[variant, i, j] = np.clip(key_rel - i + NA_WIN_R - 1, 0, n_rel_r - 1)
    blocks = jnp.take(tc, jnp.asarray(a_idx.reshape(-1)), axis=2)
    blocks = blocks.reshape(depth, NA_HEADS, 3, NA_GROUP, NA_KROWS, GRID_W, GRID_W)
    blocks = jnp.where(jnp.asarray(valid)[None, None, :, :, :, None, None], blocks, NEG)
    bias = jnp.transpose(blocks, (0, 2, 1, 3, 5, 4, 6)).reshape(depth, 3, NA_HEADS, NA_GTOK, NA_KTOK)

    flat = lambda w, k: w.reshape(depth, k, MLA_PAD_W).astype(BF16)
    return dict(
        w_z=w_z.astype(BF16),
        w_g=wg.astype(BF16),
        wq_pad=flat(wq_pad, MLA_Q_RANK),
        wq_perm=flat(wq_perm, MLA_Q_RANK),
        wk_pad=flat(wk_pad, MLA_KV_RANK),
        wv_pad=flat(wv_pad, MLA_KV_RANK),
        bias=bias,
    )


TM_PROJ = 1024
TN_PROJ = 1024
TM_UP = 512
TQ_FLASH = 1024
TK_FLASH = 2048
TM_POOL = 1024
TM_MERGE = 512
TM_FFN = 1024
TN_FFN = 1408


def kernel(x, c, ctx, c_ctx, w_ada, b_ada, norm_pre1, norm_post1, norm_pre2, norm_post2, w_in, na_rpb, pool_w,
           pool_scale, mla_q_norm, w_uq, mla_kv_norm, w_ukv, w_branch, w_o, w_up, conv_w, conv_b, w_down):
    b, s, d = x.shape
    lc = ctx.shape[1]
    depth = w_ada.shape[0]
    rows = s // GRID_W
    assert d == D_MODEL and s % NA_KTOK == 0 and lc % POOL_HALO == 0

    n_mod = -(-(b + 1) // 8) * 8
    cvec = jnp.concatenate([c, c_ctx[None, :], jnp.zeros((n_mod - b - 1, d), F32)], axis=0)
    mod = _ada(cvec, w_ada, b_ada).reshape(depth, n_mod, 6, d)

    ct_l, st_l = _rope_tables(s, True)
    ct_c, st_c = _rope_tables(lc, False)
    ones_np = np.zeros((MLA_HEADS, HEAD_PAD), np.float32)
    ones_np[:, MLA_V] = 1.0
    ones = jnp.asarray(ones_np.reshape(1, MLA_PAD_W))

    prep = _prep_all(w_in, w_uq, w_ukv, na_rpb, rows)
    pw_all, wb_all, wo_all = pool_w.astype(BF16), w_branch.astype(BF16), w_o.astype(BF16)
    wup_all, wdn_all = w_up.astype(BF16), w_down.astype(BF16)

    xc = ctx
    for l in range(depth):
        last = l == depth - 1
        p = {k: v[l] for k, v in prep.items()}
        lat = [mod[l, :b, k][:, None, :] for k in range(6)]
        cx = [mod[l, b:b + 1, k][:, None, :] for k in range(6)]
        row = lambda v: v.reshape(1, -1)
        g_pre1, g_post1, g_pre2, g_post2 = row(norm_pre1[l]), row(norm_post1[l]), row(norm_pre2[l]), row(norm_post2[l])
        qg, kvg = row(mla_q_norm[l]), row(mla_kv_norm[l])
        pw = pw_all[l]
        psc = row(pool_scale[l])
        wb = wb_all[l]
        wo = wo_all[l]
        wup = wup_all[l]
        wdn = wdn_all[l]
        cw = conv_w[l]
        cb = row(conv_b[l])

        z, gz = _in_proj(x, g_pre1, lat[1], lat[0], p["w_z"], p["w_g"], TM_PROJ, TN_PROJ)
        zc, gzc = _in_proj(xc, g_pre1, cx[1], cx[0], p["w_z"], p["w_g"], TM_PROJ, TN_PROJ)
        mla_w = (p["wq_pad"], p["wq_perm"], p["wk_pad"], p["wv_pad"], ones)
        q_l, k_l, v_l = _mla_up(z, qg, kvg, ct_l, st_l, *mla_w, TM_UP)
        q_c, k_c, v_c = _mla_up(zc, qg, kvg, ct_c, st_c, *mla_w, TM_UP)

        o_na = _na(z, zc, p["bias"])
        o_pool = _pool(z, pw, psc, TM_POOL)
        o_mla = _flash(q_l, k_c, v_c, k_l, v_l, TQ_FLASH, TK_FLASH)
        x = _merge(o_na, o_pool, o_mla, gz, wb, wo, x, lat[2], g_post1, TM_MERGE)
        x = _ffn(x, g_pre2, lat[4], lat[3], wup, cw, cb, wdn, lat[5], g_post2, TM_FFN, TN_FFN)

        if not last:
            oc_na = _ctx_na(zc)
            oc_pool = _pool(zc, pw, psc, TM_POOL)
            oc_mla = _flash(q_c, k_c, v_c, None, None, TQ_FLASH, TK_FLASH)
            xc = _merge(oc_na, oc_pool, oc_mla, gzc, wb, wo, xc, cx[2], g_post1, TM_MERGE)
            xc = _ffn(xc, g_pre2, cx[4], cx[3], wup, cw, cb, wdn, cx[5], g_post2, TM_FFN, TN_FFN)
    return x
```

```python
import functools

import jax
import jax.numpy as jnp
import numpy as np
from jax import lax
from jax.experimental import pallas as pl
from jax.experimental.pallas import tpu as pltpu

F32 = jnp.float32
BF16 = jnp.bfloat16

D_MODEL = 1024
GRID_W = 64
EPS = 1e-6
NA_HEADS = 8
NA_HEAD_DIM = 64
NA_WIDTH = NA_HEADS * NA_HEAD_DIM
NA_WIN_R = 8
NA_WIN_C = 16
POOL_WINDOWS = (2, 4, 8, 16)
POOL_GROUP = 128
POOL_WIDTH = POOL_GROUP * len(POOL_WINDOWS)
POOL_HALO = 16
MLA_HEADS = 8
MLA_NOPE = 64
MLA_ROPE = 32
MLA_V = 64
MLA_Q_RANK = 512
MLA_KV_RANK = 256
ROPE_BASE = 10000.0
N_BRANCH = 3
BRANCH_W = 512
D_FF = 2816
CONV_HALO = 8

LANES = 128
HEAD_PAD = LANES
MLA_PAD_W = MLA_HEADS * HEAD_PAD

Z_Q = 0
Z_K = Z_Q + NA_WIDTH
Z_V = Z_K + NA_WIDTH
Z_U = Z_V + NA_WIDTH
Z_CQ = Z_U + POOL_WIDTH
Z_KV = Z_CQ + MLA_Q_RANK
Z_COLS = Z_KV + 512
G_COLS = N_BRANCH * D_MODEL
assert G_COLS == Z_COLS
CB = 512

NEG = -1e30
VMEM_LIMIT = 56 * 1024 * 1024


def _cparams(sem):
    return pltpu.CompilerParams(dimension_semantics=sem, vmem_limit_bytes=VMEM_LIMIT)


def _rms(x, gain):
    return x * lax.rsqrt(jnp.mean(x * x, axis=-1, keepdims=True) + EPS) * gain


def _ada_kernel(c_ref, w_ref, b_ref, o_ref):
    c = c_ref[...]
    s = c * jax.nn.sigmoid(c)
    o_ref[0] = jnp.dot(s.astype(BF16), w_ref[0].astype(BF16), preferred_element_type=F32) + b_ref[0]


def _ada(cvec, w_ada, b_ada):
    depth, d, n = w_ada.shape
    rows = cvec.shape[0]
    tn = 1536
    return pl.pallas_call(
        _ada_kernel,
        grid=(depth, n // tn),
        in_specs=[
            pl.BlockSpec((rows, d), lambda l, j: (0, 0)),
            pl.BlockSpec((1, d, tn), lambda l, j: (l, 0, j)),
            pl.BlockSpec((1, 1, tn), lambda l, j: (l, 0, j)),
        ],
        out_specs=pl.BlockSpec((1, rows, tn), lambda l, j: (l, 0, j)),
        out_shape=jax.ShapeDtypeStruct((depth, rows, n), F32),
        compiler_params=_cparams(("arbitrary", "arbitrary")),
        name="ada",
    )(cvec, w_ada, b_ada.reshape(depth, 1, n))


def _in_proj_kernel(x_ref, gain_ref, sc_ref, sh_ref, wz_ref, wg_ref, z_ref, g_ref, h_scr):
    j = pl.program_id(2)

    @pl.when(j == 0)
    def _():
        h = _rms(x_ref[0], gain_ref[...]) * (1.0 + sc_ref[0]) + sh_ref[0]
        h_scr[...] = h.astype(BF16)

    h = h_scr[...]
    g_ref[0] = jax.nn.sigmoid(jnp.dot(h, wg_ref[...], preferred_element_type=F32)).astype(BF16)
    z_ref[0] = jnp.dot(h, wz_ref[...], preferred_element_type=F32).astype(BF16)


def _in_proj(x, gain, sc, sh, wz, wg, tm, tn):
    b, l, d = x.shape
    tm = min(tm, l)
    per_batch = sc.shape[0] > 1
    mod_map = (lambda bi, i, j: (bi, 0, 0)) if per_batch else (lambda bi, i, j: (0, 0, 0))
    wspec = pl.BlockSpec((d, tn), lambda bi, i, j: (0, j))
    ospec = pl.BlockSpec((1, tm, tn), lambda bi, i, j: (bi, i, j))
    oshape = jax.ShapeDtypeStruct((b, l, Z_COLS), BF16)
    return pl.pallas_call(
        _in_proj_kernel,
        grid=(b, l // tm, Z_COLS // tn),
        in_specs=[
            pl.BlockSpec((1, tm, d), lambda bi, i, j: (bi, i, 0)),
            pl.BlockSpec((1, d), lambda bi, i, j: (0, 0)),
            pl.BlockSpec((1, 1, d), mod_map),
            pl.BlockSpec((1, 1, d), mod_map),
            wspec, wspec,
        ],
        out_specs=[ospec, ospec],
        out_shape=[oshape, oshape],
        scratch_shapes=[pltpu.VMEM((tm, d), BF16)],
        compiler_params=_cparams(("parallel", "parallel", "arbitrary")),
        name="in_proj",
    )(x, gain, sc, sh, wz, wg)


def _mla_up_kernel(cq_ref, kv_ref, qg_ref, kvg_ref, ct_ref, st_ref, wq_ref, wqp_ref, wk_ref, wv_ref,
                   ones_ref, q_out, k_out, v_out, *, scale):
    cqn = _rms(cq_ref[0].astype(F32), qg_ref[...]).astype(BF16)
    kvb = kv_ref[0].astype(F32)
    kvn = _rms(kvb[:, :MLA_KV_RANK], kvg_ref[...]).astype(BF16)
    ct = ct_ref[...]
    st = st_ref[...]
    kr = kvb[:, MLA_KV_RANK:MLA_KV_RANK + LANES] * ct + kvb[:, MLA_KV_RANK + LANES:] * st
    qm = jnp.dot(cqn, wq_ref[...], preferred_element_type=F32)
    qp = jnp.dot(cqn, wqp_ref[...], preferred_element_type=F32)
    kn = jnp.dot(kvn, wk_ref[...], preferred_element_type=F32)
    cts = ct * scale
    sts = st * scale
    for h in range(MLA_HEADS):
        hs = slice(h * HEAD_PAD, (h + 1) * HEAD_PAD)
        q_out[0, :, hs] = (qm[:, hs] * cts + qp[:, hs] * sts).astype(BF16)
        k_out[0, :, hs] = (kn[:, hs] + kr).astype(BF16)
    v_out[0] = (jnp.dot(kvn, wv_ref[...], preferred_element_type=F32) + ones_ref[...]).astype(BF16)


def _mla_up(z, qg, kvg, ct, st, wq, wqp, wk, wv, ones, tm):
    b, l, _ = z.shape
    tm = min(tm, l)
    full = lambda shape: pl.BlockSpec(shape, lambda bi, i: (0,) * len(shape))
    out = jax.ShapeDtypeStruct((b, l, MLA_PAD_W), BF16)
    ospec = pl.BlockSpec((1, tm, MLA_PAD_W), lambda bi, i: (bi, i, 0))
    return pl.pallas_call(
        functools.partial(_mla_up_kernel, scale=float((MLA_NOPE + MLA_ROPE) ** -0.5 * np.log2(np.e))),
        grid=(b, l // tm),
        in_specs=[
            pl.BlockSpec((1, tm, CB), lambda bi, i: (bi, i, Z_CQ // CB)),
            pl.BlockSpec((1, tm, CB), lambda bi, i: (bi, i, Z_KV // CB)),
            full((1, MLA_Q_RANK)),
            full((1, MLA_KV_RANK)),
            pl.BlockSpec((tm, LANES), lambda bi, i: (i, 0)),
            pl.BlockSpec((tm, LANES), lambda bi, i: (i, 0)),
            full((MLA_Q_RANK, MLA_PAD_W)),
            full((MLA_Q_RANK, MLA_PAD_W)),
            full((MLA_KV_RANK, MLA_PAD_W)),
            full((MLA_KV_RANK, MLA_PAD_W)),
            full((1, MLA_PAD_W)),
        ],
        out_specs=[ospec, ospec, ospec],
        out_shape=[out, out, out],
        compiler_params=_cparams(("parallel", "parallel")),
        name="mla_up",
    )(z, z, qg, kvg, ct, st, wq, wqp, wk, wv, ones)


def _flash_kernel(*refs, has_latent):
    if has_latent:
        q_ref, kc_ref, vc_ref, k_ref, v_ref, o_ref, m_scr, acc_scr = refs
    else:
        q_ref, kc_ref, vc_ref, o_ref, m_scr, acc_scr = refs
        k_ref = v_ref = None
    kk = pl.program_id(2)
    nk = pl.num_programs(2)

    def attend(h, kblk, vblk):
        hs = slice(h * HEAD_PAD, (h + 1) * HEAD_PAD)
        q = q_ref[0, :, hs]
        s = lax.dot_general(q, kblk[0, :, hs], (((1,), (1,)), ((), ())), preferred_element_type=F32)
        m_prev = m_scr[h]
        m_new = jnp.maximum(m_prev, jnp.max(s, axis=-1, keepdims=True))
        alpha = jnp.exp2(m_prev - m_new)
        p = jnp.exp2(s - jnp.concatenate([m_new] * (s.shape[1] // LANES), axis=1))
        acc_scr[h] = alpha * acc_scr[h] + jnp.dot(p.astype(BF16), vblk[0, :, hs], preferred_element_type=F32)
        m_scr[h] = m_new

    @pl.when(kk == 0)
    def _():
        m_scr[...] = jnp.full(m_scr.shape, NEG, F32)
        acc_scr[...] = jnp.zeros(acc_scr.shape, F32)
        for h in range(MLA_HEADS):
            attend(h, kc_ref, vc_ref)

    if has_latent:
        for h in range(MLA_HEADS):
            attend(h, k_ref, v_ref)

    @pl.when(kk == nk - 1)
    def _():
        for h in range(MLA_HEADS):
            acc = acc_scr[h]
            o_ref[0, :, h * MLA_V:(h + 1) * MLA_V] = (acc[:, :MLA_V] / acc[:, MLA_V:MLA_V + 1]).astype(BF16)


def _flash(q, kc, vc, k, v, tq, tk):
    b, lq, _ = q.shape
    lc = kc.shape[1]
    tq = min(tq, lq)
    has_latent = k is not None
    in_specs = [
        pl.BlockSpec((1, tq, MLA_PAD_W), lambda bi, i, kk: (bi, i, 0)),
        pl.BlockSpec((1, lc, MLA_PAD_W), lambda bi, i, kk: (bi, 0, 0)),
        pl.BlockSpec((1, lc, MLA_PAD_W), lambda bi, i, kk: (bi, 0, 0)),
    ]
    args = [q, kc, vc]
    nk = 1
    if has_latent:
        tk = min(tk, k.shape[1])
        nk = k.shape[1] // tk
        in_specs += [pl.BlockSpec((1, tk, MLA_PAD_W), lambda bi, i, kk: (bi, kk, 0))] * 2
        args += [k, v]
    return pl.pallas_call(
        functools.partial(_flash_kernel, has_latent=has_latent),
        grid=(b, lq // tq, nk),
        in_specs=in_specs,
        out_specs=pl.BlockSpec((1, tq, MLA_HEADS * MLA_V), lambda bi, i, kk: (bi, i, 0)),
        out_shape=jax.ShapeDtypeStruct((b, lq, MLA_HEADS * MLA_V), BF16),
        scratch_shapes=[pltpu.VMEM((MLA_HEADS, tq, LANES), F32), pltpu.VMEM((MLA_HEADS, tq, LANES), F32)],
        compiler_params=_cparams(("parallel", "parallel", "arbitrary")),
        name="mla_flash" if has_latent else "mla_ctx",
    )(*args)


NA_GROUP = 8
NA_GTOK = NA_GROUP * GRID_W


def _pair_softmax_pv(qp, k_list, v_list, bias_fn):
    lane = lax.broadcasted_iota(jnp.int32, qp.shape, 1)
    outs = []
    for sub in range(2):
        in_head = (lane >= sub * NA_HEAD_DIM) & (lane < (sub + 1) * NA_HEAD_DIM)
        qm = jnp.where(in_head, qp, jnp.zeros_like(qp))
        s_list = []
        for idx, kb in enumerate(k_list):
            s = lax.dot_general(qm, kb, (((1,), (1,)), ((), ())), preferred_element_type=F32)
            s_list.append(bias_fn(sub, idx, s))
        m = s_list[0].max(axis=-1, keepdims=True)
        for s in s_list[1:]:
            m = jnp.maximum(m, s.max(axis=-1, keepdims=True))
        o = None
        l = None
        for s, vb in zip(s_list, v_list):
            p = jnp.exp(s - m)
            ls = p.sum(axis=-1, keepdims=True)
            os_ = jnp.dot(p.astype(BF16), vb, preferred_element_type=F32)
            o = os_ if o is None else o + os_
            l = ls if l is None else l + ls
        outs.append(o / l)
    return jnp.where(lane < NA_HEAD_DIM, outs[0], outs[1])


NA_KROWS = 2 * NA_GROUP
NA_KTOK = NA_KROWS * GRID_W
NA_QCHUNK = 512


N_REL_R = 2 * NA_WIN_R - 1
N_REL_C = 2 * NA_WIN_C - 1


def _na_group_geometry(variant, rows):
    ng = rows // NA_GROUP
    g = {0: 0, 1: 1, 2: ng - 1}[variant]
    strip0 = int(np.clip(g * NA_GROUP - NA_WIN_R // 2, 0, rows - NA_KROWS))
    r0 = [int(np.clip(g * NA_GROUP + i - NA_WIN_R // 2, 0, rows - NA_WIN_R)) for i in range(NA_GROUP)]
    return strip0 - g * NA_GROUP, [r - g * NA_GROUP for r in r0]


def _bias_table_kernel(rpb_ref, o_ref, *, rows):
    layer, v, h = pl.program_id(0), pl.program_id(1), pl.program_id(2)
    shape = (GRID_W, LANES)
    qc = lax.broadcasted_iota(jnp.int32, shape, 0)
    lane = lax.broadcasted_iota(jnp.int32, shape, 1)
    kc = lane % GRID_W
    win0 = jnp.clip(qc - NA_WIN_C // 2, 0, GRID_W - NA_WIN_C)
    in_win = (kc >= win0) & (kc < win0 + NA_WIN_C)
    rel = jnp.clip(kc - qc + NA_WIN_C - 1, 0, N_REL_C - 1)
    left = lane < GRID_W
    base = (layer * NA_HEADS + h) * (N_REL_R * N_REL_C)
    neg = jnp.full(shape, NEG, F32)
    vals = []
    for a in range(N_REL_R):
        acc = jnp.zeros(shape, F32)
        for b in range(N_REL_C):
            acc = jnp.where(rel == b, rpb_ref[base + a * N_REL_C + b], acc)
        vals.append(jnp.where(in_win, acc, neg))

    for variant in range(3):
        strip_rel, r0_rel = _na_group_geometry(variant, rows)

        def half(i, j):
            key_rel = strip_rel + j
            if r0_rel[i] <= key_rel < r0_rel[i] + NA_WIN_R:
                return vals[key_rel - i + NA_WIN_R - 1]
            return neg

        @pl.when(v == variant)
        def _():
            for i in range(NA_GROUP):
                for jp in range(NA_KROWS // 2):
                    lft, rgt = half(i, 2 * jp), half(i, 2 * jp + 1)
                    blk = lft if lft is rgt else jnp.where(left, lft, rgt)
                    o_ref[0, 0, 0, i * GRID_W:(i + 1) * GRID_W, jp * LANES:(jp + 1) * LANES] = blk


def _bias_tables(na_rpb, rows):
    depth = na_rpb.shape[0]
    return pl.pallas_call(
        functools.partial(_bias_table_kernel, rows=rows),
        grid=(depth, 3, NA_HEADS),
        in_specs=[pl.BlockSpec(memory_space=pltpu.SMEM)],
        out_specs=pl.BlockSpec((1, 1, 1, NA_GTOK, NA_KTOK), lambda l, v, h: (l, v, h, 0, 0)),
        out_shape=jax.ShapeDtypeStruct((depth, 3, NA_HEADS, NA_GTOK, NA_KTOK), F32),
        compiler_params=_cparams(("arbitrary", "arbitrary", "arbitrary")),
        name="na_bias",
    )(na_rpb.reshape(-1))


def _na_kernel(q_ref, kp_ref, kc_ref, kn_ref, vp_ref, vc_ref, vn_ref, kx_ref, vx_ref, bias_ref, o_ref,
               kcat, vcat, *, rows):
    g = pl.program_id(1)
    kcat[0:NA_GTOK] = kp_ref[0]
    kcat[NA_GTOK:2 * NA_GTOK] = kc_ref[0]
    kcat[2 * NA_GTOK:3 * NA_GTOK] = kn_ref[0]
    vcat[0:NA_GTOK] = vp_ref[0]
    vcat[NA_GTOK:2 * NA_GTOK] = vc_ref[0]
    vcat[2 * NA_GTOK:3 * NA_GTOK] = vn_ref[0]
    strip0 = jnp.clip(g * NA_GROUP - NA_WIN_R // 2, 0, rows - NA_KROWS)
    start = pl.multiple_of((strip0 - g * NA_GROUP + NA_GROUP) * GRID_W, GRID_W)
    for pr in range(NA_HEADS // 2):
        ps = slice(pr * LANES, (pr + 1) * LANES)
        k_w = kcat[pl.ds(start, NA_KTOK), ps]
        v_w = vcat[pl.ds(start, NA_KTOK), ps]
        for c in range(NA_GTOK // NA_QCHUNK):
            rs = slice(c * NA_QCHUNK, (c + 1) * NA_QCHUNK)

            def bias_fn(sub, idx, s, pr=pr, rs=rs):
                return s + bias_ref[0, 0, 2 * pr + sub, rs, :] if idx == 0 else s

            o = _pair_softmax_pv(q_ref[0, rs, ps], [k_w, kx_ref[0, :, ps]], [v_w, vx_ref[0, :, ps]], bias_fn)
            o_ref[0, rs, ps] = o.astype(BF16)


def _na(z, zc, bias, layer):
    b, s, _ = z.shape
    lc = zc.shape[1]
    rows = s // GRID_W
    ng = rows // NA_GROUP
    blk = (1, NA_GTOK, CB)

    def spec(col, off):
        return pl.BlockSpec(blk, lambda bi, g: (bi, jnp.clip(g + off, 0, ng - 1), col // CB))

    def bias_map(bi, g):
        return (layer, jnp.where(g == 0, 0, jnp.where(g == ng - 1, 2, 1)), 0, 0, 0)

    return pl.pallas_call(
        functools.partial(_na_kernel, rows=rows),
        grid=(b, ng),
        in_specs=[
            spec(Z_Q, 0),
            spec(Z_K, -1), spec(Z_K, 0), spec(Z_K, 1),
            spec(Z_V, -1), spec(Z_V, 0), spec(Z_V, 1),
            pl.BlockSpec((1, lc, CB), lambda bi, g: (bi, 0, Z_K // CB)),
            pl.BlockSpec((1, lc, CB), lambda bi, g: (bi, 0, Z_V // CB)),
            pl.BlockSpec((1, 1) + bias.shape[2:], bias_map, pipeline_mode=pl.Buffered(1)),
        ],
        out_specs=pl.BlockSpec(blk, lambda bi, g: (bi, g, 0)),
        out_shape=jax.ShapeDtypeStruct((b, s, NA_WIDTH), BF16),
        scratch_shapes=[pltpu.VMEM((3 * NA_GTOK, CB), BF16), pltpu.VMEM((3 * NA_GTOK, CB), BF16)],
        compiler_params=_cparams(("parallel", "arbitrary")),
        name="na_attn",
    )(z, z, z, z, z, z, z, zc, zc, bias)


def _ctx_na_kernel(q_ref, k_ref, v_ref, o_ref):
    for pr in range(NA_HEADS // 2):
        ps = slice(pr * LANES, (pr + 1) * LANES)
        o = _pair_softmax_pv(q_ref[0, :, ps], [k_ref[0, :, ps]], [v_ref[0, :, ps]], lambda sub, idx, s: s)
        o_ref[0, :, ps] = o.astype(BF16)


def _ctx_na(zc):
    b, lc, _ = zc.shape
    spec = lambda col: pl.BlockSpec((1, lc, CB), lambda bi: (bi, 0, col // CB))
    return pl.pallas_call(
        _ctx_na_kernel,
        grid=(b,),
        in_specs=[spec(Z_Q), spec(Z_K), spec(Z_V)],
        out_specs=pl.BlockSpec((1, lc, NA_WIDTH), lambda bi: (bi, 0, 0)),
        out_shape=jax.ShapeDtypeStruct((b, lc, NA_WIDTH), BF16),
        compiler_params=_cparams(("parallel",)),
        name="ctx_na",
    )(zc, zc, zc)


def _pool_kernel(up_ref, u_ref, un_ref, w_ref, sc_ref, o_ref, *, seq_len):
    i = pl.program_id(1)
    tm = u_ref.shape[1]
    ext = jnp.concatenate([up_ref[0], u_ref[0], un_ref[0]], axis=0).astype(F32)
    n_ext = tm + 2 * POOL_HALO
    tg = i * tm - POOL_HALO + lax.broadcasted_iota(jnp.int32, (n_ext, 1), 0)
    ext = jnp.where((tg >= 0) & (tg < seq_len), ext, 0.0)
    t = i * tm + lax.broadcasted_iota(jnp.int32, (tm, 1), 0)

    def shifted(a, k):
        return pltpu.roll(a, (-k) % n_ext, 0)

    for gi, win in enumerate(POOL_WINDOWS):
        gs = slice(gi * POOL_GROUP, (gi + 1) * POOL_GROUP)
        a = ext[:, gs]
        wsum = a + shifted(a, -1)
        half = 1
        while 2 * half < win:
            wsum = shifted(wsum, -half) + shifted(wsum, half)
            half *= 2
        cnt = (jnp.minimum(t + win // 2, seq_len) - jnp.maximum(t - win // 2, 0)).astype(F32)
        d = wsum[POOL_HALO:POOL_HALO + tm] / cnt - a[POOL_HALO:POOL_HALO + tm]
        y = jnp.dot(d.astype(BF16), w_ref[gi], preferred_element_type=F32)
        o_ref[0, :, gs] = (y * sc_ref[:, gs]).astype(BF16)


def _pool(z, w, sc, tm):
    b, l, _ = z.shape
    tm = min(tm, l)
    hb = tm // POOL_HALO
    nhb = l // POOL_HALO
    col = Z_U // CB
    return pl.pallas_call(
        functools.partial(_pool_kernel, seq_len=l),
        grid=(b, l // tm),
        in_specs=[
            pl.BlockSpec((1, POOL_HALO, CB), lambda bi, i: (bi, jnp.maximum(i * hb - 1, 0), col)),
            pl.BlockSpec((1, tm, CB), lambda bi, i: (bi, i, col)),
            pl.BlockSpec((1, POOL_HALO, CB), lambda bi, i: (bi, jnp.minimum((i + 1) * hb, nhb - 1), col)),
            pl.BlockSpec(w.shape, lambda bi, i: (0, 0, 0)),
            pl.BlockSpec((1, POOL_WIDTH), lambda bi, i: (0, 0)),
        ],
        out_specs=pl.BlockSpec((1, tm, POOL_WIDTH), lambda bi, i: (bi, i, 0)),
        out_shape=jax.ShapeDtypeStruct((b, l, POOL_WIDTH), BF16),
        compiler_params=_cparams(("parallel", "parallel")),
        name="pool",
    )(z, z, z, w, sc)


def _merge_kernel(ona_ref, opool_ref, omla_ref, g0_ref, g1_ref, g2_ref, wb_ref, wo_ref, x_ref, gate_ref,
                  gain_ref, o_ref):
    m = None
    for br_ref, g_ref, k in ((ona_ref, g0_ref, 0), (opool_ref, g1_ref, 1), (omla_ref, g2_ref, 2)):
        proj = jnp.dot(br_ref[0], wb_ref[k], preferred_element_type=F32)
        term = g_ref[0].astype(F32) * proj
        m = term if m is None else m + term
    y = jnp.dot(m.astype(BF16), wo_ref[...], preferred_element_type=F32)
    o_ref[0] = x_ref[0] + gate_ref[0] * _rms(y, gain_ref[...])


def _merge(o_na, o_pool, o_mla, z, wb, wo, x, gate, gain, tm):
    b, l, d = x.shape
    tm = min(tm, l)
    per_batch = gate.shape[0] > 1
    mod_map = (lambda bi, i: (bi, 0, 0)) if per_batch else (lambda bi, i: (0, 0, 0))
    br = pl.BlockSpec((1, tm, BRANCH_W), lambda bi, i: (bi, i, 0))
    gspec = lambda k: pl.BlockSpec((1, tm, d), lambda bi, i: (bi, i, k))
    return pl.pallas_call(
        _merge_kernel,
        grid=(b, l // tm),
        in_specs=[
            br, br, br, gspec(0), gspec(1), gspec(2),
            pl.BlockSpec(wb.shape, lambda bi, i: (0, 0, 0)),
            pl.BlockSpec(wo.shape, lambda bi, i: (0, 0)),
            pl.BlockSpec((1, tm, d), lambda bi, i: (bi, i, 0)),
            pl.BlockSpec((1, 1, d), mod_map),
            pl.BlockSpec((1, d), lambda bi, i: (0, 0)),
        ],
        out_specs=pl.BlockSpec((1, tm, d), lambda bi, i: (bi, i, 0)),
        out_shape=jax.ShapeDtypeStruct((b, l, d), F32),
        compiler_params=_cparams(("parallel", "parallel")),
        name="merge",
    )(o_na, o_pool, o_mla, z, z, z, wb, wo, x, gate, gain)


def _gelu_tanh(x):
    return 0.5 * x * (1.0 + jnp.tanh(np.float32(np.sqrt(2.0 / np.pi)) * (x + np.float32(0.044715) * (x * x * x))))


def _ffn_kernel(xp_ref, x_ref, xn_ref, gain_ref, sc_ref, sh_ref, wa_ref, wb_ref, cwa_ref, cwb_ref, cba_ref,
                cbb_ref, wd_ref, gate_ref, gpost_ref, o_ref, h_scr, acc_scr):
    i = pl.program_id(1)
    j = pl.program_id(2)
    ni = pl.num_programs(1)
    nj = pl.num_programs(2)
    tm = x_ref.shape[1]
    n_ext = tm + 2 * CONV_HALO

    @pl.when(j == 0)
    def _():
        def norm_mod(xv):
            return _rms(xv, gain_ref[...]) * (1.0 + sc_ref[0]) + sh_ref[0]

        hp = jnp.where(i > 0, norm_mod(xp_ref[0]), 0.0)
        hn = jnp.where(i < ni - 1, norm_mod(xn_ref[0]), 0.0)
        h_scr[0:CONV_HALO] = hp.astype(BF16)
        h_scr[CONV_HALO:CONV_HALO + tm] = norm_mod(x_ref[0]).astype(BF16)
        h_scr[CONV_HALO + tm:n_ext] = hn.astype(BF16)
        acc_scr[...] = jnp.zeros(acc_scr.shape, F32)

    h = h_scr[...]

    def conv_half(w_ref, cw_ref, cb_ref):
        u = jnp.dot(h, w_ref[...], preferred_element_type=F32)
        prev = pltpu.roll(u, 1, 0)[CONV_HALO:CONV_HALO + tm]
        nxt = pltpu.roll(u, n_ext - 1, 0)[CONV_HALO:CONV_HALO + tm]
        cur = u[CONV_HALO:CONV_HALO + tm]
        return cb_ref[...] + prev * cw_ref[0:1] + cur * cw_ref[1:2] + nxt * cw_ref[2:3]

    a = conv_half(wa_ref, cwa_ref, cba_ref)
    bgate = conv_half(wb_ref, cwb_ref, cbb_ref)
    act = (_gelu_tanh(a) * bgate).astype(BF16)
    acc_scr[...] += jnp.dot(act, wd_ref[...], preferred_element_type=F32)

    @pl.when(j == nj - 1)
    def _():
        o_ref[0] = x_ref[0] + gate_ref[0] * _rms(acc_scr[...], gpost_ref[...])


def _ffn(x, gain, sc, sh, w_up, conv_w, conv_b, w_down, gate, gpost, tm, tn):
    b, l, d = x.shape
    tm = min(tm, l)
    nch = D_FF // tn
    hb = tm // CONV_HALO
    nhb = l // CONV_HALO
    per_batch = sc.shape[0] > 1
    mod_map = (lambda bi, i, j: (bi, 0, 0)) if per_batch else (lambda bi, i, j: (0, 0, 0))
    mod = pl.BlockSpec((1, 1, d), mod_map)
    vec = pl.BlockSpec((1, d), lambda bi, i, j: (0, 0))
    return pl.pallas_call(
        _ffn_kernel,
        grid=(b, l // tm, nch),
        in_specs=[
            pl.BlockSpec((1, CONV_HALO, d), lambda bi, i, j: (bi, jnp.maximum(i * hb - 1, 0), 0)),
            pl.BlockSpec((1, tm, d), lambda bi, i, j: (bi, i, 0)),
            pl.BlockSpec((1, CONV_HALO, d), lambda bi, i, j: (bi, jnp.minimum((i + 1) * hb, nhb - 1), 0)),
            vec, mod, mod,
            pl.BlockSpec((d, tn), lambda bi, i, j: (0, j)),
            pl.BlockSpec((d, tn), lambda bi, i, j: (0, nch + j)),
            pl.BlockSpec((3, tn), lambda bi, i, j: (0, j)),
            pl.BlockSpec((3, tn), lambda bi, i, j: (0, nch + j)),
            pl.BlockSpec((1, tn), lambda bi, i, j: (0, j)),
            pl.BlockSpec((1, tn), lambda bi, i, j: (0, nch + j)),
            pl.BlockSpec((tn, d), lambda bi, i, j: (j, 0)),
            mod, vec,
        ],
        out_specs=pl.BlockSpec((1, tm, d), lambda bi, i, j: (bi, i, 0)),
        out_shape=jax.ShapeDtypeStruct((b, l, d), F32),
        scratch_shapes=[pltpu.VMEM((tm + 2 * CONV_HALO, d), BF16), pltpu.VMEM((tm, d), F32)],
        compiler_params=_cparams(("parallel", "parallel", "arbitrary")),
        name="ffn",
    )(x, x, x, gain, sc, sh, w_up, w_up, conv_w, conv_w, conv_b, conv_b, w_down, gate, gpost)


_ROPE_PERM = np.concatenate([np.arange(8, 16), np.arange(0, 8), np.arange(24, 32), np.arange(16, 24)])


def _rope_tables(n_tok, rotate):
    ones = jnp.ones((n_tok, MLA_NOPE), F32)
    zeros = jnp.zeros((n_tok, MLA_NOPE), F32)
    pad1 = jnp.ones((n_tok, HEAD_PAD - MLA_NOPE - MLA_ROPE), F32)
    pad0 = jnp.zeros((n_tok, HEAD_PAD - MLA_NOPE - MLA_ROPE), F32)
    if not rotate:
        return (jnp.concatenate([ones, jnp.ones((n_tok, MLA_ROPE), F32), pad1], axis=1),
                jnp.concatenate([zeros, jnp.zeros((n_tok, MLA_ROPE), F32), pad0], axis=1))
    n_freq = MLA_ROPE // 4
    inv = ROPE_BASE ** (-jnp.arange(n_freq, dtype=F32) / n_freq)
    t = jnp.arange(n_tok, dtype=jnp.int32)
    ang_r = (t // GRID_W).astype(F32)[:, None] * inv[None, :]
    ang_c = (t % GRID_W).astype(F32)[:, None] * inv[None, :]
    cr, sr, cc, sn = jnp.cos(ang_r), jnp.sin(ang_r), jnp.cos(ang_c), jnp.sin(ang_c)
    ct = jnp.concatenate([ones, cr, cr, cc, cc, pad1], axis=1)
    st = jnp.concatenate([zeros, -sr, sr, -sn, sn, pad0], axis=1)
    return ct, st


def _prep_all(w_in, w_uq, w_ukv):
    depth, d, _ = w_in.shape
    o = 0
    parts = []
    for s in (NA_WIDTH, NA_WIDTH, NA_WIDTH, POOL_WIDTH, MLA_Q_RANK, MLA_KV_RANK, MLA_ROPE, N_BRANCH * D_MODEL):
        parts.append(w_in[..., o:o + s])
        o += s
    wq, wk, wv, wu, wcq, wckv, wkr, wg = parts
    z64 = jnp.zeros((depth, d, MLA_NOPE), F32)
    z32 = jnp.zeros((depth, d, HEAD_PAD - MLA_NOPE - MLA_ROPE), F32)
    w_z = jnp.concatenate([wq * (NA_HEAD_DIM ** -0.5), wk, wv, wu, wcq, wckv,
                           z64, wkr, z32, z64, wkr[..., _ROPE_PERM], z32], axis=-1)

    uq = w_uq.reshape(depth, MLA_Q_RANK, MLA_HEADS, MLA_NOPE + MLA_ROPE)
    qz64 = jnp.zeros((depth, MLA_Q_RANK, MLA_HEADS, MLA_NOPE), F32)
    qz32 = jnp.zeros((depth, MLA_Q_RANK, MLA_HEADS, HEAD_PAD - MLA_NOPE - MLA_ROPE), F32)
    wq_pad = jnp.concatenate([uq, qz32], axis=-1)
    wq_perm = jnp.concatenate([qz64, uq[..., MLA_NOPE:][..., _ROPE_PERM], qz32], axis=-1)
    ukv = w_ukv.reshape(depth, MLA_KV_RANK, MLA_HEADS, MLA_NOPE + MLA_V)
    kz = jnp.zeros((depth, MLA_KV_RANK, MLA_HEADS, HEAD_PAD - MLA_NOPE), F32)
    wk_pad = jnp.concatenate([ukv[..., :MLA_NOPE], kz], axis=-1)
    wv_pad = jnp.concatenate([ukv[..., MLA_NOPE:], kz], axis=-1)

    flat = lambda w, k: w.reshape(depth, k, MLA_PAD_W).astype(BF16)
    return dict(
        w_z=w_z.astype(BF16),
        w_g=wg.astype(BF16),
        wq_pad=flat(wq_pad, MLA_Q_RANK),
        wq_perm=flat(wq_perm, MLA_Q_RANK),
        wk_pad=flat(wk_pad, MLA_KV_RANK),
        wv_pad=flat(wv_pad, MLA_KV_RANK),
    )


TM_PROJ = 1024
TN_PROJ = 1024
TM_UP = 512
TQ_FLASH = 1024
TK_FLASH = 2048
TM_POOL = 1024
TM_MERGE = 512
TM_FFN = 1024
TN_FFN = 1408


def kernel(x, c, ctx, c_ctx, w_ada, b_ada, norm_pre1, norm_post1, norm_pre2, norm_post2, w_in, na_rpb, pool_w,
           pool_scale, mla_q_norm, w_uq, mla_kv_norm, w_ukv, w_branch, w_o, w_up, conv_w, conv_b, w_down):
    b, s, d = x.shape
    lc = ctx.shape[1]
    depth = w_ada.shape[0]
    rows = s // GRID_W
    assert d == D_MODEL and s % NA_KTOK == 0 and lc % POOL_HALO == 0

    n_mod = -(-(b + 1) // 8) * 8
    cvec = jnp.concatenate([c, c_ctx[None, :], jnp.zeros((n_mod - b - 1, d), F32)], axis=0)
    mod = _ada(cvec, w_ada, b_ada).reshape(depth, n_mod, 6, d)

    ct_l, st_l = _rope_tables(s, True)
    ct_c, st_c = _rope_tables(lc, False)
    ones_np = np.zeros((MLA_HEADS, HEAD_PAD), np.float32)
    ones_np[:, MLA_V] = 1.0
    ones = jnp.asarray(ones_np.reshape(1, MLA_PAD_W))

    prep = _prep_all(w_in, w_uq, w_ukv)
    na_bias = _bias_tables(na_rpb, rows)
    pw_all, wb_all, wo_all = pool_w.astype(BF16), w_branch.astype(BF16), w_o.astype(BF16)
    wup_all, wdn_all = w_up.astype(BF16), w_down.astype(BF16)

    xc = ctx
    for l in range(depth):
        last = l == depth - 1
        p = {k: v[l] for k, v in prep.items()}
        lat = [mod[l, :b, k][:, None, :] for k in range(6)]
        cx = [mod[l, b:b + 1, k][:, None, :] for k in range(6)]
        row = lambda v: v.reshape(1, -1)
        g_pre1, g_post1, g_pre2, g_post2 = row(norm_pre1[l]), row(norm_post1[l]), row(norm_pre2[l]), row(norm_post2[l])
        qg, kvg = row(mla_q_norm[l]), row(mla_kv_norm[l])
        pw = pw_all[l]
        psc = row(pool_scale[l])
        wb = wb_all[l]
        wo = wo_all[l]
        wup = wup_all[l]
        wdn = wdn_all[l]
        cw = conv_w[l]
        cb = row(conv_b[l])

        z, gz = _in_proj(x, g_pre1, lat[1], lat[0], p["w_z"], p["w_g"], TM_PROJ, TN_PROJ)
        zc, gzc = _in_proj(xc, g_pre1, cx[1], cx[0], p["w_z"], p["w_g"], TM_PROJ, TN_PROJ)
        mla_w = (p["wq_pad"], p["wq_perm"], p["wk_pad"], p["wv_pad"], ones)
        q_l, k_l, v_l = _mla_up(z, qg, kvg, ct_l, st_l, *mla_w, TM_UP)
        q_c, k_c, v_c = _mla_up(zc, qg, kvg, ct_c, st_c, *mla_w, TM_UP)

        o_na = _na(z, zc, na_bias, l)
        o_pool = _pool(z, pw, psc, TM_POOL)
        o_mla = _flash(q_l, k_c, v_c, k_l, v_l, TQ_FLASH, TK_FLASH)
        x = _merge(o_na, o_pool, o_mla, gz, wb, wo, x, lat[2], g_post1, TM_MERGE)
        x = _ffn(x, g_pre2, lat[4], lat[3], wup, cw, cb, wdn, lat[5], g_post2, TM_FFN, TN_FFN)

        if not last:
            oc_na = _ctx_na(zc)
            oc_pool = _pool(zc, pw, psc, TM_POOL)
            oc_mla = _flash(q_c, k_c, v_c, None, None, TQ_FLASH, TK_FLASH)
            xc = _merge(oc_na, oc_pool, oc_mla, gzc, wb, wo, xc, cx[2], g_post1, TM_MERGE)
            xc = _ffn(xc, g_pre2, cx[4], cx[3], wup, cw, cb, wdn, cx[5], g_post2, TM_FFN, TN_FFN)
    return x
```

```python
import functools

import jax
import jax.numpy as jnp
import numpy as np
from jax import lax
from jax.experimental import pallas as pl
from jax.experimental.pallas import tpu as pltpu

F32 = jnp.float32
BF16 = jnp.bfloat16

D_MODEL = 1024
GRID_W = 64
EPS = 1e-6
NA_HEADS = 8
NA_HEAD_DIM = 64
NA_WIDTH = NA_HEADS * NA_HEAD_DIM
NA_WIN_R = 8
NA_WIN_C = 16
POOL_WINDOWS = (2, 4, 8, 16)
POOL_GROUP = 128
POOL_WIDTH = POOL_GROUP * len(POOL_WINDOWS)
POOL_HALO = 16
MLA_HEADS = 8
MLA_NOPE = 64
MLA_ROPE = 32
MLA_V = 64
MLA_Q_RANK = 512
MLA_KV_RANK = 256
ROPE_BASE = 10000.0
N_BRANCH = 3
BRANCH_W = 512
D_FF = 2816
CONV_HALO = 8

LANES = 128
HEAD_PAD = LANES
MLA_PAD_W = MLA_HEADS * HEAD_PAD
MLA_SUM_LANE = 0

Z_Q = 0
Z_K = Z_Q + NA_WIDTH
Z_V = Z_K + NA_WIDTH
Z_U = Z_V + NA_WIDTH
Z_CQ = Z_U + POOL_WIDTH
Z_KV = Z_CQ + MLA_Q_RANK
Z_COLS = Z_KV + 512
G_COLS = N_BRANCH * D_MODEL
assert G_COLS == Z_COLS
CB = 512

NEG = -1e30
VMEM_LIMIT = 56 * 1024 * 1024


def _cparams(sem):
    return pltpu.CompilerParams(dimension_semantics=sem, vmem_limit_bytes=VMEM_LIMIT)


def _rms(x, gain):
    return x * lax.rsqrt(jnp.mean(x * x, axis=-1, keepdims=True) + EPS) * gain


def _ada_kernel(c_ref, w_ref, b_ref, o_ref):
    c = c_ref[...]
    s = c * jax.nn.sigmoid(c)
    o_ref[0] = jnp.dot(s.astype(BF16), w_ref[0].astype(BF16), preferred_element_type=F32) + b_ref[0]


def _ada(cvec, w_ada, b_ada):
    depth, d, n = w_ada.shape
    rows = cvec.shape[0]
    tn = 1536
    return pl.pallas_call(
        _ada_kernel,
        grid=(depth, n // tn),
        in_specs=[
            pl.BlockSpec((rows, d), lambda l, j: (0, 0)),
            pl.BlockSpec((1, d, tn), lambda l, j: (l, 0, j)),
            pl.BlockSpec((1, 1, tn), lambda l, j: (l, 0, j)),
        ],
        out_specs=pl.BlockSpec((1, rows, tn), lambda l, j: (l, 0, j)),
        out_shape=jax.ShapeDtypeStruct((depth, rows, n), F32),
        compiler_params=_cparams(("arbitrary", "arbitrary")),
        name="ada",
    )(cvec, w_ada, b_ada.reshape(depth, 1, n))


def _in_proj_kernel(x_ref, gain_ref, sc_ref, sh_ref, wz_ref, wg_ref, z_ref, g_ref, h_scr):
    j = pl.program_id(2)

    @pl.when(j == 0)
    def _():
        h = _rms(x_ref[0], gain_ref[...]) * (1.0 + sc_ref[0]) + sh_ref[0]
        h_scr[...] = h.astype(BF16)

    h = h_scr[...]
    g_ref[0] = jax.nn.sigmoid(jnp.dot(h, wg_ref[...], preferred_element_type=F32)).astype(BF16)
    z_ref[0] = jnp.dot(h, wz_ref[...], preferred_element_type=F32).astype(BF16)


def _in_proj(x, gain, sc, sh, wz, wg, layer, tm, tn):
    b, l, d = x.shape
    tm = min(tm, l)
    per_batch = sc.shape[0] > 1
    mod_map = (lambda bi, i, j: (bi, 0, 0)) if per_batch else (lambda bi, i, j: (0, 0, 0))
    wspec = pl.BlockSpec((None, d, tn), lambda bi, i, j: (layer, 0, j))
    ospec = pl.BlockSpec((1, tm, tn), lambda bi, i, j: (bi, i, j))
    oshape = jax.ShapeDtypeStruct((b, l, Z_COLS), BF16)
    return pl.pallas_call(
        _in_proj_kernel,
        grid=(b, l // tm, Z_COLS // tn),
        in_specs=[
            pl.BlockSpec((1, tm, d), lambda bi, i, j: (bi, i, 0)),
            pl.BlockSpec((1, d), lambda bi, i, j: (0, 0)),
            pl.BlockSpec((1, 1, d), mod_map),
            pl.BlockSpec((1, 1, d), mod_map),
            wspec, wspec,
        ],
        out_specs=[ospec, ospec],
        out_shape=[oshape, oshape],
        scratch_shapes=[pltpu.VMEM((tm, d), BF16)],
        compiler_params=_cparams(("parallel", "parallel", "arbitrary")),
        name="in_proj",
    )(x, gain, sc, sh, wz, wg)


def _mla_up_kernel(cq_ref, kv_ref, qg_ref, kvg_ref, ct_ref, st_ref, wq_ref, wqp_ref, wkv_ref,
                   q_out, k_out, v_out, *, scale):
    cqn = _rms(cq_ref[0].astype(F32), qg_ref[...]).astype(BF16)
    kvb = kv_ref[0].astype(F32)
    kvn = _rms(kvb[:, :MLA_KV_RANK], kvg_ref[...]).astype(BF16)
    ct = ct_ref[...]
    st = st_ref[...]
    kr = kvb[:, MLA_KV_RANK:MLA_KV_RANK + LANES] * ct + kvb[:, MLA_KV_RANK + LANES:] * st
    qm = jnp.dot(cqn, wq_ref[...], preferred_element_type=F32)
    qp = jnp.dot(cqn, wqp_ref[...], preferred_element_type=F32)
    kv = jnp.dot(kvn, wkv_ref[...], preferred_element_type=F32)
    cts = ct * scale
    sts = st * scale
    lane = lax.broadcasted_iota(jnp.int32, kr.shape, 1)
    is_k = lane < MLA_NOPE
    row_sum_col = jnp.where(lane == MLA_SUM_LANE, 1.0, 0.0)
    for h in range(MLA_HEADS):
        hs = slice(h * HEAD_PAD, (h + 1) * HEAD_PAD)
        q_out[0, :, hs] = (qm[:, hs] * cts + qp[:, hs] * sts).astype(BF16)
        k_out[0, :, hs] = jnp.where(is_k, kv[:, hs], kr).astype(BF16)
        v_out[0, :, hs] = jnp.where(is_k, row_sum_col, kv[:, hs]).astype(BF16)


def _mla_up(z, qg, kvg, ct, st, wq, wqp, wkv, layer, tm):
    b, l, _ = z.shape
    tm = min(tm, l)
    full = lambda shape: pl.BlockSpec(shape, lambda bi, i: (0,) * len(shape))
    wspec = lambda k: pl.BlockSpec((None, k, MLA_PAD_W), lambda bi, i: (layer, 0, 0))
    out = jax.ShapeDtypeStruct((b, l, MLA_PAD_W), BF16)
    ospec = pl.BlockSpec((1, tm, MLA_PAD_W), lambda bi, i: (bi, i, 0))
    return pl.pallas_call(
        functools.partial(_mla_up_kernel, scale=float((MLA_NOPE + MLA_ROPE) ** -0.5 * np.log2(np.e))),
        grid=(b, l // tm),
        in_specs=[
            pl.BlockSpec((1, tm, CB), lambda bi, i: (bi, i, Z_CQ // CB)),
            pl.BlockSpec((1, tm, CB), lambda bi, i: (bi, i, Z_KV // CB)),
            full((1, MLA_Q_RANK)),
            full((1, MLA_KV_RANK)),
            pl.BlockSpec((tm, LANES), lambda bi, i: (i, 0)),
            pl.BlockSpec((tm, LANES), lambda bi, i: (i, 0)),
            wspec(MLA_Q_RANK), wspec(MLA_Q_RANK), wspec(MLA_KV_RANK),
        ],
        out_specs=[ospec, ospec, ospec],
        out_shape=[out, out, out],
        compiler_params=_cparams(("parallel", "parallel")),
        name="mla_up",
    )(z, z, qg, kvg, ct, st, wq, wqp, wkv)


def _flash_kernel(*refs, has_latent):
    if has_latent:
        q_ref, kc_ref, vc_ref, k_ref, v_ref, o_ref, m_scr, acc_scr = refs
    else:
        q_ref, kc_ref, vc_ref, o_ref, m_scr, acc_scr = refs
        k_ref = v_ref = None
    kk = pl.program_id(2)
    nk = pl.num_programs(2)

    def attend(h, kblk, vblk):
        hs = slice(h * HEAD_PAD, (h + 1) * HEAD_PAD)
        q = q_ref[0, :, hs]
        s = lax.dot_general(q, kblk[0, :, hs], (((1,), (1,)), ((), ())), preferred_element_type=F32)
        m_prev = m_scr[h]
        m_new = jnp.maximum(m_prev, jnp.max(s, axis=-1, keepdims=True))
        alpha = jnp.exp2(m_prev - m_new)
        p = jnp.exp2(s - jnp.concatenate([m_new] * (s.shape[1] // LANES), axis=1))
        acc_scr[h] = alpha * acc_scr[h] + jnp.dot(p.astype(BF16), vblk[0, :, hs], preferred_element_type=F32)
        m_scr[h] = m_new

    @pl.when(kk == 0)
    def _():
        m_scr[...] = jnp.full(m_scr.shape, NEG, F32)
        acc_scr[...] = jnp.zeros(acc_scr.shape, F32)
        for h in range(MLA_HEADS):
            attend(h, kc_ref, vc_ref)

    if has_latent:
        for h in range(MLA_HEADS):
            attend(h, k_ref, v_ref)

    @pl.when(kk == nk - 1)
    def _():
        for h in range(MLA_HEADS):
            acc = acc_scr[h]
            o_ref[0, :, h * MLA_V:(h + 1) * MLA_V] = (
                acc[:, HEAD_PAD - MLA_V:] / acc[:, MLA_SUM_LANE:MLA_SUM_LANE + 1]).astype(BF16)


def _flash(q, kc, vc, k, v, tq, tk):
    b, lq, _ = q.shape
    lc = kc.shape[1]
    tq = min(tq, lq)
    has_latent = k is not None
    in_specs = [
        pl.BlockSpec((1, tq, MLA_PAD_W), lambda bi, i, kk: (bi, i, 0)),
        pl.BlockSpec((1, lc, MLA_PAD_W), lambda bi, i, kk: (bi, 0, 0)),
        pl.BlockSpec((1, lc, MLA_PAD_W), lambda bi, i, kk: (bi, 0, 0)),
    ]
    args = [q, kc, vc]
    nk = 1
    if has_latent:
        tk = min(tk, k.shape[1])
        nk = k.shape[1] // tk
        in_specs += [pl.BlockSpec((1, tk, MLA_PAD_W), lambda bi, i, kk: (bi, kk, 0))] * 2
        args += [k, v]
    return pl.pallas_call(
        functools.partial(_flash_kernel, has_latent=has_latent),
        grid=(b, lq // tq, nk),
        in_specs=in_specs,
        out_specs=pl.BlockSpec((1, tq, MLA_HEADS * MLA_V), lambda bi, i, kk: (bi, i, 0)),
        out_shape=jax.ShapeDtypeStruct((b, lq, MLA_HEADS * MLA_V), BF16),
        scratch_shapes=[pltpu.VMEM((MLA_HEADS, tq, LANES), F32), pltpu.VMEM((MLA_HEADS, tq, LANES), F32)],
        compiler_params=_cparams(("parallel", "parallel", "arbitrary")),
        name="mla_flash" if has_latent else "mla_ctx",
    )(*args)


NA_GROUP = 8
NA_GTOK = NA_GROUP * GRID_W


def _pair_softmax_pv(qp, k_list, v_list, bias_fn):
    lane = lax.broadcasted_iota(jnp.int32, qp.shape, 1)
    outs = []
    for sub in range(2):
        in_head = (lane >= sub * NA_HEAD_DIM) & (lane < (sub + 1) * NA_HEAD_DIM)
        qm = jnp.where(in_head, qp, jnp.zeros_like(qp))
        s_list = []
        for idx, kb in enumerate(k_list):
            s = lax.dot_general(qm, kb, (((1,), (1,)), ((), ())), preferred_element_type=F32)
            s_list.append(bias_fn(sub, idx, s))
        m = s_list[0].max(axis=-1, keepdims=True)
        for s in s_list[1:]:
            m = jnp.maximum(m, s.max(axis=-1, keepdims=True))
        o = None
        l = None
        for s, vb in zip(s_list, v_list):
            p = jnp.exp(s - m)
            ls = p.sum(axis=-1, keepdims=True)
            os_ = jnp.dot(p.astype(BF16), vb, preferred_element_type=F32)
            o = os_ if o is None else o + os_
            l = ls if l is None else l + ls
        outs.append(o / l)
    return jnp.where(lane < NA_HEAD_DIM, outs[0], outs[1])


NA_KROWS = 2 * NA_GROUP
NA_KTOK = NA_KROWS * GRID_W
NA_QCHUNK = 512


N_REL_R = 2 * NA_WIN_R - 1
N_REL_C = 2 * NA_WIN_C - 1


def _na_group_geometry(variant, rows):
    ng = rows // NA_GROUP
    g = {0: 0, 1: 1, 2: ng - 1}[variant]
    strip0 = int(np.clip(g * NA_GROUP - NA_WIN_R // 2, 0, rows - NA_KROWS))
    r0 = [int(np.clip(g * NA_GROUP + i - NA_WIN_R // 2, 0, rows - NA_WIN_R)) for i in range(NA_GROUP)]
    return strip0 - g * NA_GROUP, [r - g * NA_GROUP for r in r0]


def _bias_table_kernel(rpb_ref, o_ref, *, rows):
    layer, v, h = pl.program_id(0), pl.program_id(1), pl.program_id(2)
    shape = (GRID_W, LANES)
    qc = lax.broadcasted_iota(jnp.int32, shape, 0)
    lane = lax.broadcasted_iota(jnp.int32, shape, 1)
    kc = lane % GRID_W
    win0 = jnp.clip(qc - NA_WIN_C // 2, 0, GRID_W - NA_WIN_C)
    in_win = (kc >= win0) & (kc < win0 + NA_WIN_C)
    rel = jnp.clip(kc - qc + NA_WIN_C - 1, 0, N_REL_C - 1)
    left = lane < GRID_W
    base = (layer * NA_HEADS + h) * (N_REL_R * N_REL_C)
    neg = jnp.full(shape, NEG, F32)
    vals = []
    for a in range(N_REL_R):
        acc = jnp.zeros(shape, F32)
        for b in range(N_REL_C):
            acc = jnp.where(rel == b, rpb_ref[base + a * N_REL_C + b], acc)
        vals.append(jnp.where(in_win, acc, neg))

    for variant in range(3):
        strip_rel, r0_rel = _na_group_geometry(variant, rows)

        def half(i, j):
            key_rel = strip_rel + j
            if r0_rel[i] <= key_rel < r0_rel[i] + NA_WIN_R:
                return vals[key_rel - i + NA_WIN_R - 1]
            return neg

        @pl.when(v == variant)
        def _():
            for i in range(NA_GROUP):
                for jp in range(NA_KROWS // 2):
                    lft, rgt = half(i, 2 * jp), half(i, 2 * jp + 1)
                    blk = lft if lft is rgt else jnp.where(left, lft, rgt)
                    o_ref[0, 0, 0, i * GRID_W:(i + 1) * GRID_W, jp * LANES:(jp + 1) * LANES] = blk


def _bias_tables(na_rpb, rows):
    depth = na_rpb.shape[0]
    return pl.pallas_call(
        functools.partial(_bias_table_kernel, rows=rows),
        grid=(depth, 3, NA_HEADS),
        in_specs=[pl.BlockSpec(memory_space=pltpu.SMEM)],
        out_specs=pl.BlockSpec((1, 1, 1, NA_GTOK, NA_KTOK), lambda l, v, h: (l, v, h, 0, 0)),
        out_shape=jax.ShapeDtypeStruct((depth, 3, NA_HEADS, NA_GTOK, NA_KTOK), F32),
        compiler_params=_cparams(("arbitrary", "arbitrary", "arbitrary")),
        name="na_bias",
    )(na_rpb.reshape(-1))


def _na_kernel(q_ref, kp_ref, kc_ref, kn_ref, vp_ref, vc_ref, vn_ref, kx_ref, vx_ref, bias_ref, o_ref,
               kcat, vcat, *, rows):
    g = pl.program_id(1)
    kcat[0:NA_GTOK] = kp_ref[0]
    kcat[NA_GTOK:2 * NA_GTOK] = kc_ref[0]
    kcat[2 * NA_GTOK:3 * NA_GTOK] = kn_ref[0]
    vcat[0:NA_GTOK] = vp_ref[0]
    vcat[NA_GTOK:2 * NA_GTOK] = vc_ref[0]
    vcat[2 * NA_GTOK:3 * NA_GTOK] = vn_ref[0]
    strip0 = jnp.clip(g * NA_GROUP - NA_WIN_R // 2, 0, rows - NA_KROWS)
    start = pl.multiple_of((strip0 - g * NA_GROUP + NA_GROUP) * GRID_W, GRID_W)
    for pr in range(NA_HEADS // 2):
        ps = slice(pr * LANES, (pr + 1) * LANES)
        k_w = kcat[pl.ds(start, NA_KTOK), ps]
        v_w = vcat[pl.ds(start, NA_KTOK), ps]
        for c in range(NA_GTOK // NA_QCHUNK):
            rs = slice(c * NA_QCHUNK, (c + 1) * NA_QCHUNK)

            def bias_fn(sub, idx, s, pr=pr, rs=rs):
                return s + bias_ref[0, 0, 2 * pr + sub, rs, :] if idx == 0 else s

            o = _pair_softmax_pv(q_ref[0, rs, ps], [k_w, kx_ref[0, :, ps]], [v_w, vx_ref[0, :, ps]], bias_fn)
            o_ref[0, rs, ps] = o.astype(BF16)


def _na(z, zc, bias, layer):
    b, s, _ = z.shape
    lc = zc.shape[1]
    rows = s // GRID_W
    ng = rows // NA_GROUP
    blk = (1, NA_GTOK, CB)

    def spec(col, off):
        return pl.BlockSpec(blk, lambda bi, g: (bi, jnp.clip(g + off, 0, ng - 1), col // CB))

    def bias_map(bi, g):
        return (layer, jnp.where(g == 0, 0, jnp.where(g == ng - 1, 2, 1)), 0, 0, 0)

    return pl.pallas_call(
        functools.partial(_na_kernel, rows=rows),
        grid=(b, ng),
        in_specs=[
            spec(Z_Q, 0),
            spec(Z_K, -1), spec(Z_K, 0), spec(Z_K, 1),
            spec(Z_V, -1), spec(Z_V, 0), spec(Z_V, 1),
            pl.BlockSpec((1, lc, CB), lambda bi, g: (bi, 0, Z_K // CB)),
            pl.BlockSpec((1, lc, CB), lambda bi, g: (bi, 0, Z_V // CB)),
            pl.BlockSpec((1, 1) + bias.shape[2:], bias_map, pipeline_mode=pl.Buffered(1)),
        ],
        out_specs=pl.BlockSpec(blk, lambda bi, g: (bi, g, 0)),
        out_shape=jax.ShapeDtypeStruct((b, s, NA_WIDTH), BF16),
        scratch_shapes=[pltpu.VMEM((3 * NA_GTOK, CB), BF16), pltpu.VMEM((3 * NA_GTOK, CB), BF16)],
        compiler_params=_cparams(("parallel", "arbitrary")),
        name="na_attn",
    )(z, z, z, z, z, z, z, zc, zc, bias)


def _ctx_na_kernel(q_ref, k_ref, v_ref, o_ref):
    for pr in range(NA_HEADS // 2):
        ps = slice(pr * LANES, (pr + 1) * LANES)
        o = _pair_softmax_pv(q_ref[0, :, ps], [k_ref[0, :, ps]], [v_ref[0, :, ps]], lambda sub, idx, s: s)
        o_ref[0, :, ps] = o.astype(BF16)


def _ctx_na(zc):
    b, lc, _ = zc.shape
    spec = lambda col: pl.BlockSpec((1, lc, CB), lambda bi: (bi, 0, col // CB))
    return pl.pallas_call(
        _ctx_na_kernel,
        grid=(b,),
        in_specs=[spec(Z_Q), spec(Z_K), spec(Z_V)],
        out_specs=pl.BlockSpec((1, lc, NA_WIDTH), lambda bi: (bi, 0, 0)),
        out_shape=jax.ShapeDtypeStruct((b, lc, NA_WIDTH), BF16),
        compiler_params=_cparams(("parallel",)),
        name="ctx_na",
    )(zc, zc, zc)


def _pool_kernel(up_ref, u_ref, un_ref, w_ref, sc_ref, o_ref, *, seq_len):
    i = pl.program_id(1)
    tm = u_ref.shape[1]
    ext = jnp.concatenate([up_ref[0], u_ref[0], un_ref[0]], axis=0).astype(F32)
    n_ext = tm + 2 * POOL_HALO
    tg = i * tm - POOL_HALO + lax.broadcasted_iota(jnp.int32, (n_ext, 1), 0)
    ext = jnp.where((tg >= 0) & (tg < seq_len), ext, 0.0)
    t = i * tm + lax.broadcasted_iota(jnp.int32, (tm, 1), 0)

    def shifted(a, k):
        return pltpu.roll(a, (-k) % n_ext, 0)

    for gi, win in enumerate(POOL_WINDOWS):
        gs = slice(gi * POOL_GROUP, (gi + 1) * POOL_GROUP)
        a = ext[:, gs]
        wsum = a + shifted(a, -1)
        half = 1
        while 2 * half < win:
            wsum = shifted(wsum, -half) + shifted(wsum, half)
            half *= 2
        cnt = (jnp.minimum(t + win // 2, seq_len) - jnp.maximum(t - win // 2, 0)).astype(F32)
        d = wsum[POOL_HALO:POOL_HALO + tm] / cnt - a[POOL_HALO:POOL_HALO + tm]
        y = jnp.dot(d.astype(BF16), w_ref[gi], preferred_element_type=F32)
        o_ref[0, :, gs] = (y * sc_ref[:, gs]).astype(BF16)


def _pool(z, w, sc, layer, tm):
    b, l, _ = z.shape
    tm = min(tm, l)
    hb = tm // POOL_HALO
    nhb = l // POOL_HALO
    col = Z_U // CB
    return pl.pallas_call(
        functools.partial(_pool_kernel, seq_len=l),
        grid=(b, l // tm),
        in_specs=[
            pl.BlockSpec((1, POOL_HALO, CB), lambda bi, i: (bi, jnp.maximum(i * hb - 1, 0), col)),
            pl.BlockSpec((1, tm, CB), lambda bi, i: (bi, i, col)),
            pl.BlockSpec((1, POOL_HALO, CB), lambda bi, i: (bi, jnp.minimum((i + 1) * hb, nhb - 1), col)),
            pl.BlockSpec((None,) + w.shape[1:], lambda bi, i: (layer, 0, 0, 0)),
            pl.BlockSpec((1, POOL_WIDTH), lambda bi, i: (0, 0)),
        ],
        out_specs=pl.BlockSpec((1, tm, POOL_WIDTH), lambda bi, i: (bi, i, 0)),
        out_shape=jax.ShapeDtypeStruct((b, l, POOL_WIDTH), BF16),
        compiler_params=_cparams(("parallel", "parallel")),
        name="pool",
    )(z, z, z, w, sc)


def _merge_kernel(ona_ref, opool_ref, omla_ref, g0_ref, g1_ref, g2_ref, wb_ref, wo_ref, x_ref, gate_ref,
                  gain_ref, o_ref):
    m = None
    for br_ref, g_ref, k in ((ona_ref, g0_ref, 0), (opool_ref, g1_ref, 1), (omla_ref, g2_ref, 2)):
        proj = jnp.dot(br_ref[0], wb_ref[k], preferred_element_type=F32)
        term = g_ref[0].astype(F32) * proj
        m = term if m is None else m + term
    y = jnp.dot(m.astype(BF16), wo_ref[...], preferred_element_type=F32)
    o_ref[0] = x_ref[0] + gate_ref[0] * _rms(y, gain_ref[...])


def _merge(o_na, o_pool, o_mla, z, wb, wo, x, gate, gain, layer, tm):
    b, l, d = x.shape
    tm = min(tm, l)
    per_batch = gate.shape[0] > 1
    mod_map = (lambda bi, i: (bi, 0, 0)) if per_batch else (lambda bi, i: (0, 0, 0))
    br = pl.BlockSpec((1, tm, BRANCH_W), lambda bi, i: (bi, i, 0))
    gspec = lambda k: pl.BlockSpec((1, tm, d), lambda bi, i: (bi, i, k))
    return pl.pallas_call(
        _merge_kernel,
        grid=(b, l // tm),
        in_specs=[
            br, br, br, gspec(0), gspec(1), gspec(2),
            pl.BlockSpec((None,) + wb.shape[1:], lambda bi, i: (layer, 0, 0, 0)),
            pl.BlockSpec((None,) + wo.shape[1:], lambda bi, i: (layer, 0, 0)),
            pl.BlockSpec((1, tm, d), lambda bi, i: (bi, i, 0)),
            pl.BlockSpec((1, 1, d), mod_map),
            pl.BlockSpec((1, d), lambda bi, i: (0, 0)),
        ],
        out_specs=pl.BlockSpec((1, tm, d), lambda bi, i: (bi, i, 0)),
        out_shape=jax.ShapeDtypeStruct((b, l, d), F32),
        compiler_params=_cparams(("parallel", "parallel")),
        name="merge",
    )(o_na, o_pool, o_mla, z, z, z, wb, wo, x, gate, gain)


def _gelu_tanh(x):
    return 0.5 * x * (1.0 + jnp.tanh(np.float32(np.sqrt(2.0 / np.pi)) * (x + np.float32(0.044715) * (x * x * x))))


def _ffn_kernel(xp_ref, x_ref, xn_ref, gain_ref, sc_ref, sh_ref, wa_ref, wb_ref, cwa_ref, cwb_ref, cba_ref,
                cbb_ref, wd_ref, gate_ref, gpost_ref, o_ref, h_scr, acc_scr):
    i = pl.program_id(1)
    j = pl.program_id(2)
    ni = pl.num_programs(1)
    nj = pl.num_programs(2)
    tm = x_ref.shape[1]
    n_ext = tm + 2 * CONV_HALO

    @pl.when(j == 0)
    def _():
        def norm_mod(xv):
            return _rms(xv, gain_ref[...]) * (1.0 + sc_ref[0]) + sh_ref[0]

        hp = jnp.where(i > 0, norm_mod(xp_ref[0]), 0.0)
        hn = jnp.where(i < ni - 1, norm_mod(xn_ref[0]), 0.0)
        h_scr[0:CONV_HALO] = hp.astype(BF16)
        h_scr[CONV_HALO:CONV_HALO + tm] = norm_mod(x_ref[0]).astype(BF16)
        h_scr[CONV_HALO + tm:n_ext] = hn.astype(BF16)
        acc_scr[...] = jnp.zeros(acc_scr.shape, F32)

    h = h_scr[...]

    def conv_half(w_ref, cw_ref, cb_ref):
        u = jnp.dot(h, w_ref[...], preferred_element_type=F32)
        prev = pltpu.roll(u, 1, 0)[CONV_HALO:CONV_HALO + tm]
        nxt = pltpu.roll(u, n_ext - 1, 0)[CONV_HALO:CONV_HALO + tm]
        cur = u[CONV_HALO:CONV_HALO + tm]
        return cb_ref[...] + prev * cw_ref[0:1] + cur * cw_ref[1:2] + nxt * cw_ref[2:3]

    a = conv_half(wa_ref, cwa_ref, cba_ref)
    bgate = conv_half(wb_ref, cwb_ref, cbb_ref)
    act = (_gelu_tanh(a) * bgate).astype(BF16)
    acc_scr[...] += jnp.dot(act, wd_ref[...], preferred_element_type=F32)

    @pl.when(j == nj - 1)
    def _():
        o_ref[0] = x_ref[0] + gate_ref[0] * _rms(acc_scr[...], gpost_ref[...])


def _ffn(x, gain, sc, sh, w_up, conv_w, conv_b, w_down, gate, gpost, layer, tm, tn):
    b, l, d = x.shape
    tm = min(tm, l)
    nch = D_FF // tn
    hb = tm // CONV_HALO
    nhb = l // CONV_HALO
    per_batch = sc.shape[0] > 1
    mod_map = (lambda bi, i, j: (bi, 0, 0)) if per_batch else (lambda bi, i, j: (0, 0, 0))
    mod = pl.BlockSpec((1, 1, d), mod_map)
    vec = pl.BlockSpec((1, d), lambda bi, i, j: (0, 0))
    return pl.pallas_call(
        _ffn_kernel,
        grid=(b, l // tm, nch),
        in_specs=[
            pl.BlockSpec((1, CONV_HALO, d), lambda bi, i, j: (bi, jnp.maximum(i * hb - 1, 0), 0)),
            pl.BlockSpec((1, tm, d), lambda bi, i, j: (bi, i, 0)),
            pl.BlockSpec((1, CONV_HALO, d), lambda bi, i, j: (bi, jnp.minimum((i + 1) * hb, nhb - 1), 0)),
            vec, mod, mod,
            pl.BlockSpec((None, d, tn), lambda bi, i, j: (layer, 0, j)),
            pl.BlockSpec((None, d, tn), lambda bi, i, j: (layer, 0, nch + j)),
            pl.BlockSpec((None, 3, tn), lambda bi, i, j: (layer, 0, j)),
            pl.BlockSpec((None, 3, tn), lambda bi, i, j: (layer, 0, nch + j)),
            pl.BlockSpec((None, 1, tn), lambda bi, i, j: (layer, 0, j)),
            pl.BlockSpec((None, 1, tn), lambda bi, i, j: (layer, 0, nch + j)),
            pl.BlockSpec((None, tn, d), lambda bi, i, j: (layer, j, 0)),
            mod, vec,
        ],
        out_specs=pl.BlockSpec((1, tm, d), lambda bi, i, j: (bi, i, 0)),
        out_shape=jax.ShapeDtypeStruct((b, l, d), F32),
        scratch_shapes=[pltpu.VMEM((tm + 2 * CONV_HALO, d), BF16), pltpu.VMEM((tm, d), F32)],
        compiler_params=_cparams(("parallel", "parallel", "arbitrary")),
        name="ffn",
    )(x, x, x, gain, sc, sh, w_up, w_up, conv_w, conv_w, conv_b, conv_b, w_down, gate, gpost)


_ROPE_PERM = np.concatenate([np.arange(8, 16), np.arange(0, 8), np.arange(24, 32), np.arange(16, 24)])


def _rope_tables(n_tok, rotate):
    f32 = np.float32
    ct = np.ones((n_tok, HEAD_PAD), f32)
    st = np.zeros((n_tok, HEAD_PAD), f32)
    if rotate:
        n_freq = MLA_ROPE // 4
        inv = (f32(ROPE_BASE) ** (-np.arange(n_freq, dtype=f32) / f32(n_freq))).astype(f32)
        t = np.arange(n_tok, dtype=np.int32)
        ang_r = ((t // GRID_W).astype(f32)[:, None] * inv[None, :]).astype(f32)
        ang_c = ((t % GRID_W).astype(f32)[:, None] * inv[None, :]).astype(f32)
        cr, sr, cc, sn = np.cos(ang_r), np.sin(ang_r), np.cos(ang_c), np.sin(ang_c)
        ct[:, MLA_NOPE:MLA_NOPE + MLA_ROPE] = np.concatenate([cr, cr, cc, cc], axis=1)
        st[:, MLA_NOPE:MLA_NOPE + MLA_ROPE] = np.concatenate([-sr, sr, -sn, sn], axis=1)
    return jnp.asarray(ct), jnp.asarray(st)


def _prep_all(w_in, w_uq, w_ukv):
    depth, d, _ = w_in.shape
    o = 0
    parts = []
    for s in (NA_WIDTH, NA_WIDTH, NA_WIDTH, POOL_WIDTH, MLA_Q_RANK, MLA_KV_RANK, MLA_ROPE, N_BRANCH * D_MODEL):
        parts.append(w_in[..., o:o + s])
        o += s
    wq, wk, wv, wu, wcq, wckv, wkr, wg = parts
    z64 = jnp.zeros((depth, d, MLA_NOPE), F32)
    z32 = jnp.zeros((depth, d, HEAD_PAD - MLA_NOPE - MLA_ROPE), F32)
    w_z = jnp.concatenate([wq * (NA_HEAD_DIM ** -0.5), wk, wv, wu, wcq, wckv,
                           z64, wkr, z32, z64, wkr[..., _ROPE_PERM], z32], axis=-1)

    uq = w_uq.reshape(depth, MLA_Q_RANK, MLA_HEADS, MLA_NOPE + MLA_ROPE)
    qz64 = jnp.zeros((depth, MLA_Q_RANK, MLA_HEADS, MLA_NOPE), F32)
    qz32 = jnp.zeros((depth, MLA_Q_RANK, MLA_HEADS, HEAD_PAD - MLA_NOPE - MLA_ROPE), F32)
    wq_pad = jnp.concatenate([uq, qz32], axis=-1)
    wq_perm = jnp.concatenate([qz64, uq[..., MLA_NOPE:][..., _ROPE_PERM], qz32], axis=-1)
    assert MLA_NOPE + MLA_V == HEAD_PAD

    flat = lambda w, k: w.reshape(depth, k, MLA_PAD_W).astype(BF16)
    return dict(
        w_z=w_z.astype(BF16),
        w_g=wg.astype(BF16),
        wq_pad=flat(wq_pad, MLA_Q_RANK),
        wq_perm=flat(wq_perm, MLA_Q_RANK),
        w_kv=w_ukv.astype(BF16),
    )


TM_PROJ = 1024
TN_PROJ = 1024
TM_UP = 512
TQ_FLASH = 1024
TK_FLASH = 2048
TM_POOL = 1024
TM_MERGE = 1024
TM_FFN = 1024
TN_FFN = 1408


def kernel(x, c, ctx, c_ctx, w_ada, b_ada, norm_pre1, norm_post1, norm_pre2, norm_post2, w_in, na_rpb, pool_w,
           pool_scale, mla_q_norm, w_uq, mla_kv_norm, w_ukv, w_branch, w_o, w_up, conv_w, conv_b, w_down):
    b, s, d = x.shape
    lc = ctx.shape[1]
    depth = w_ada.shape[0]
    rows = s // GRID_W
    assert d == D_MODEL and s % NA_KTOK == 0 and lc % POOL_HALO == 0

    n_mod = -(-(b + 1) // 8) * 8
    cvec = jnp.concatenate([c, c_ctx[None, :], jnp.zeros((n_mod - b - 1, d), F32)], axis=0)
    mod = _ada(cvec, w_ada, b_ada).reshape(depth, n_mod, 6, d)

    ct_l, st_l = _rope_tables(s, True)
    ct_c, st_c = _rope_tables(lc, False)

    p = _prep_all(w_in, w_uq, w_ukv)
    na_bias = _bias_tables(na_rpb, rows)
    pw, wb, wo = pool_w.astype(BF16), w_branch.astype(BF16), w_o.astype(BF16)
    wup, wdn = w_up.astype(BF16), w_down.astype(BF16)
    cb = conv_b[:, None, :]

    xc = ctx
    for l in range(depth):
        last = l == depth - 1
        lat = [mod[l, :b, k][:, None, :] for k in range(6)]
        cx = [mod[l, b:b + 1, k][:, None, :] for k in range(6)]
        row = lambda v: v.reshape(1, -1)
        g_pre1, g_post1, g_pre2, g_post2 = row(norm_pre1[l]), row(norm_post1[l]), row(norm_pre2[l]), row(norm_post2[l])
        qg, kvg = row(mla_q_norm[l]), row(mla_kv_norm[l])
        psc = row(pool_scale[l])

        z, gz = _in_proj(x, g_pre1, lat[1], lat[0], p["w_z"], p["w_g"], l, TM_PROJ, TN_PROJ)
        zc, gzc = _in_proj(xc, g_pre1, cx[1], cx[0], p["w_z"], p["w_g"], l, TM_PROJ, TN_PROJ)
        mla_w = (p["wq_pad"], p["wq_perm"], p["w_kv"], l)
        q_l, k_l, v_l = _mla_up(z, qg, kvg, ct_l, st_l, *mla_w, TM_UP)
        q_c, k_c, v_c = _mla_up(zc, qg, kvg, ct_c, st_c, *mla_w, TM_UP)

        o_na = _na(z, zc, na_bias, l)
        o_pool = _pool(z, pw, psc, l, TM_POOL)
        o_mla = _flash(q_l, k_c, v_c, k_l, v_l, TQ_FLASH, TK_FLASH)
        x = _merge(o_na, o_pool, o_mla, gz, wb, wo, x, lat[2], g_post1, l, TM_MERGE)
        x = _ffn(x, g_pre2, lat[4], lat[3], wup, conv_w, cb, wdn, lat[5], g_post2, l, TM_FFN, TN_FFN)

        if not last:
            oc_na = _ctx_na(zc)
            oc_pool = _pool(zc, pw, psc, l, TM_POOL)
            oc_mla = _flash(q_c, k_c, v_c, None, None, TQ_FLASH, TK_FLASH)
            xc = _merge(oc_na, oc_pool, oc_mla, gzc, wb, wo, xc, cx[2], g_post1, l, TM_MERGE)
            xc = _ffn(xc, g_pre2, cx[4], cx[3], wup, conv_w, cb, wdn, cx[5], g_post2, l, TM_FFN, TN_FFN)
    return x
```

```python
import functools

import jax
import jax.numpy as jnp
import numpy as np
from jax import lax
from jax.experimental import pallas as pl
from jax.experimental.pallas import tpu as pltpu

F32 = jnp.float32
BF16 = jnp.bfloat16

D_MODEL = 1024
GRID_W = 64
EPS = 1e-6
NA_HEADS = 8
NA_HEAD_DIM = 64
NA_WIDTH = NA_HEADS * NA_HEAD_DIM
NA_WIN_R = 8
NA_WIN_C = 16
POOL_WINDOWS = (2, 4, 8, 16)
POOL_GROUP = 128
POOL_WIDTH = POOL_GROUP * len(POOL_WINDOWS)
POOL_HALO = 16
MLA_HEADS = 8
MLA_NOPE = 64
MLA_ROPE = 32
MLA_V = 64
MLA_Q_RANK = 512
MLA_KV_RANK = 256
ROPE_BASE = 10000.0
N_BRANCH = 3
BRANCH_W = 512
D_FF = 2816
CONV_HALO = 8

LANES = 128
HEAD_PAD = LANES
MLA_PAD_W = MLA_HEADS * HEAD_PAD
MLA_SUM_LANE = 0

Z_Q = 0
Z_K = Z_Q + NA_WIDTH
Z_V = Z_K + NA_WIDTH
Z_U = Z_V + NA_WIDTH
Z_CQ = Z_U + POOL_WIDTH
Z_KV = Z_CQ + MLA_Q_RANK
Z_COLS = Z_KV + 512
G_COLS = N_BRANCH * D_MODEL
assert G_COLS == Z_COLS
CB = 512

NEG = -1e30
VMEM_LIMIT = 56 * 1024 * 1024


def _cparams(sem):
    return pltpu.CompilerParams(dimension_semantics=sem, vmem_limit_bytes=VMEM_LIMIT)


def _rms(x, gain):
    return x * lax.rsqrt(jnp.mean(x * x, axis=-1, keepdims=True) + EPS) * gain


def _ada_kernel(c_ref, w_ref, b_ref, o_ref):
    c = c_ref[...]
    s = c * jax.nn.sigmoid(c)
    o_ref[0] = jnp.dot(s.astype(BF16), w_ref[0].astype(BF16), preferred_element_type=F32) + b_ref[0]


def _ada(cvec, w_ada, b_ada):
    depth, d, n = w_ada.shape
    rows = cvec.shape[0]
    tn = 1536
    return pl.pallas_call(
        _ada_kernel,
        grid=(depth, n // tn),
        in_specs=[
            pl.BlockSpec((rows, d), lambda l, j: (0, 0)),
            pl.BlockSpec((1, d, tn), lambda l, j: (l, 0, j)),
            pl.BlockSpec((1, 1, tn), lambda l, j: (l, 0, j)),
        ],
        out_specs=pl.BlockSpec((1, rows, tn), lambda l, j: (l, 0, j)),
        out_shape=jax.ShapeDtypeStruct((depth, rows, n), F32),
        compiler_params=_cparams(("arbitrary", "arbitrary")),
        name="ada",
    )(cvec, w_ada, b_ada.reshape(depth, 1, n))


def _in_proj_kernel(x_ref, gain_ref, sc_ref, sh_ref, wz_ref, wg_ref, z_ref, g_ref, h_scr):
    j = pl.program_id(2)

    @pl.when(j == 0)
    def _():
        h = _rms(x_ref[0], gain_ref[...]) * (1.0 + sc_ref[0]) + sh_ref[0]
        h_scr[...] = h.astype(BF16)

    h = h_scr[...]
    g_ref[0] = jax.nn.sigmoid(jnp.dot(h, wg_ref[...], preferred_element_type=F32)).astype(BF16)
    z_ref[0] = jnp.dot(h, wz_ref[...], preferred_element_type=F32).astype(BF16)


def _in_proj(x, gain, sc, sh, wz, wg, layer, tm, tn):
    b, l, d = x.shape
    tm = min(tm, l)
    per_batch = sc.shape[0] > 1
    mod_map = (lambda bi, i, j: (bi, 0, 0)) if per_batch else (lambda bi, i, j: (0, 0, 0))
    wspec = pl.BlockSpec((None, d, tn), lambda bi, i, j: (layer, 0, j))
    ospec = pl.BlockSpec((1, tm, tn), lambda bi, i, j: (bi, i, j))
    oshape = jax.ShapeDtypeStruct((b, l, Z_COLS), BF16)
    return pl.pallas_call(
        _in_proj_kernel,
        grid=(b, l // tm, Z_COLS // tn),
        in_specs=[
            pl.BlockSpec((1, tm, d), lambda bi, i, j: (bi, i, 0)),
            pl.BlockSpec((1, d), lambda bi, i, j: (0, 0)),
            pl.BlockSpec((1, 1, d), mod_map),
            pl.BlockSpec((1, 1, d), mod_map),
            wspec, wspec,
        ],
        out_specs=[ospec, ospec],
        out_shape=[oshape, oshape],
        scratch_shapes=[pltpu.VMEM((tm, d), BF16)],
        compiler_params=_cparams(("parallel", "parallel", "arbitrary")),
        name="in_proj",
    )(x, gain, sc, sh, wz, wg)


def _mla_up_kernel(cq_ref, kv_ref, qg_ref, kvg_ref, ct_ref, st_ref, wq_ref, wqp_ref, wkv_ref,
                   q_out, k_out, v_out, *, scale):
    cqn = _rms(cq_ref[0].astype(F32), qg_ref[...]).astype(BF16)
    kvb = kv_ref[0].astype(F32)
    kvn = _rms(kvb[:, :MLA_KV_RANK], kvg_ref[...]).astype(BF16)
    ct = ct_ref[...]
    st = st_ref[...]
    kr = kvb[:, MLA_KV_RANK:MLA_KV_RANK + LANES] * ct + kvb[:, MLA_KV_RANK + LANES:] * st
    qm = jnp.dot(cqn, wq_ref[...], preferred_element_type=F32)
    qp = jnp.dot(cqn, wqp_ref[...], preferred_element_type=F32)
    kv = jnp.dot(kvn, wkv_ref[...], preferred_element_type=F32)
    cts = ct * scale
    sts = st * scale
    lane = lax.broadcasted_iota(jnp.int32, kr.shape, 1)
    is_k = lane < MLA_NOPE
    row_sum_col = jnp.where(lane == MLA_SUM_LANE, 1.0, 0.0)
    for h in range(MLA_HEADS):
        hs = slice(h * HEAD_PAD, (h + 1) * HEAD_PAD)
        q_out[0, :, hs] = (qm[:, hs] * cts + qp[:, hs] * sts).astype(BF16)
        k_out[0, :, hs] = jnp.where(is_k, kv[:, hs], kr).astype(BF16)
        v_out[0, :, hs] = jnp.where(is_k, row_sum_col, kv[:, hs]).astype(BF16)


def _mla_up(z, qg, kvg, ct, st, wq, wqp, wkv, layer, tm):
    b, l, _ = z.shape
    tm = min(tm, l)
    full = lambda shape: pl.BlockSpec(shape, lambda bi, i: (0,) * len(shape))
    wspec = lambda k: pl.BlockSpec((None, k, MLA_PAD_W), lambda bi, i: (layer, 0, 0))
    out = jax.ShapeDtypeStruct((b, l, MLA_PAD_W), BF16)
    ospec = pl.BlockSpec((1, tm, MLA_PAD_W), lambda bi, i: (bi, i, 0))
    return pl.pallas_call(
        functools.partial(_mla_up_kernel, scale=float((MLA_NOPE + MLA_ROPE) ** -0.5 * np.log2(np.e))),
        grid=(b, l // tm),
        in_specs=[
            pl.BlockSpec((1, tm, CB), lambda bi, i: (bi, i, Z_CQ // CB)),
            pl.BlockSpec((1, tm, CB), lambda bi, i: (bi, i, Z_KV // CB)),
            full((1, MLA_Q_RANK)),
            full((1, MLA_KV_RANK)),
            pl.BlockSpec((tm, LANES), lambda bi, i: (i, 0)),
            pl.BlockSpec((tm, LANES), lambda bi, i: (i, 0)),
            wspec(MLA_Q_RANK), wspec(MLA_Q_RANK), wspec(MLA_KV_RANK),
        ],
        out_specs=[ospec, ospec, ospec],
        out_shape=[out, out, out],
        compiler_params=_cparams(("parallel", "parallel")),
        name="mla_up",
    )(z, z, qg, kvg, ct, st, wq, wqp, wkv)


def _flash_kernel(*refs, has_latent):
    if has_latent:
        q_ref, kc_ref, vc_ref, k_ref, v_ref, o_ref, m_scr, acc_scr = refs
    else:
        q_ref, kc_ref, vc_ref, o_ref, m_scr, acc_scr = refs
        k_ref = v_ref = None
    kk = pl.program_id(2)
    nk = pl.num_programs(2)

    def attend(h, kblk, vblk):
        hs = slice(h * HEAD_PAD, (h + 1) * HEAD_PAD)
        q = q_ref[0, :, hs]
        s = lax.dot_general(q, kblk[0, :, hs], (((1,), (1,)), ((), ())), preferred_element_type=F32)
        m_prev = m_scr[h]
        m_new = jnp.maximum(m_prev, jnp.max(s, axis=-1, keepdims=True))
        alpha = jnp.exp2(m_prev - m_new)
        p = jnp.exp2(s - jnp.concatenate([m_new] * (s.shape[1] // LANES), axis=1))
        acc_scr[h] = alpha * acc_scr[h] + jnp.dot(p.astype(BF16), vblk[0, :, hs], preferred_element_type=F32)
        m_scr[h] = m_new

    @pl.when(kk == 0)
    def _():
        m_scr[...] = jnp.full(m_scr.shape, NEG, F32)
        acc_scr[...] = jnp.zeros(acc_scr.shape, F32)
        for h in range(MLA_HEADS):
            attend(h, kc_ref, vc_ref)

    if has_latent:
        for h in range(MLA_HEADS):
            attend(h, k_ref, v_ref)

    @pl.when(kk == nk - 1)
    def _():
        for h in range(MLA_HEADS):
            acc = acc_scr[h]
            o_ref[0, :, h * MLA_V:(h + 1) * MLA_V] = (
                acc[:, HEAD_PAD - MLA_V:] / acc[:, MLA_SUM_LANE:MLA_SUM_LANE + 1]).astype(BF16)


def _flash(q, kc, vc, k, v, tq, tk):
    b, lq, _ = q.shape
    lc = kc.shape[1]
    tq = min(tq, lq)
    has_latent = k is not None
    in_specs = [
        pl.BlockSpec((1, tq, MLA_PAD_W), lambda bi, i, kk: (bi, i, 0)),
        pl.BlockSpec((1, lc, MLA_PAD_W), lambda bi, i, kk: (bi, 0, 0)),
        pl.BlockSpec((1, lc, MLA_PAD_W), lambda bi, i, kk: (bi, 0, 0)),
    ]
    args = [q, kc, vc]
    nk = 1
    if has_latent:
        tk = min(tk, k.shape[1])
        nk = k.shape[1] // tk
        in_specs += [pl.BlockSpec((1, tk, MLA_PAD_W), lambda bi, i, kk: (bi, kk, 0))] * 2
        args += [k, v]
    return pl.pallas_call(
        functools.partial(_flash_kernel, has_latent=has_latent),
        grid=(b, lq // tq, nk),
        in_specs=in_specs,
        out_specs=pl.BlockSpec((1, tq, MLA_HEADS * MLA_V), lambda bi, i, kk: (bi, i, 0)),
        out_shape=jax.ShapeDtypeStruct((b, lq, MLA_HEADS * MLA_V), BF16),
        scratch_shapes=[pltpu.VMEM((MLA_HEADS, tq, LANES), F32), pltpu.VMEM((MLA_HEADS, tq, LANES), F32)],
        compiler_params=_cparams(("parallel", "parallel", "arbitrary")),
        name="mla_flash" if has_latent else "mla_ctx",
    )(*args)


NA_GROUP = 8
NA_GTOK = NA_GROUP * GRID_W


def _pair_softmax_pv(qp, k_list, v_list, bias_fn):
    lane = lax.broadcasted_iota(jnp.int32, qp.shape, 1)
    outs = []
    for sub in range(2):
        in_head = (lane >= sub * NA_HEAD_DIM) & (lane < (sub + 1) * NA_HEAD_DIM)
        qm = jnp.where(in_head, qp, jnp.zeros_like(qp))
        s_list = []
        for idx, kb in enumerate(k_list):
            s = lax.dot_general(qm, kb, (((1,), (1,)), ((), ())), preferred_element_type=F32)
            s_list.append(bias_fn(sub, idx, s))
        m = s_list[0].max(axis=-1, keepdims=True)
        for s in s_list[1:]:
            m = jnp.maximum(m, s.max(axis=-1, keepdims=True))
        o = None
        l = None
        for s, vb in zip(s_list, v_list):
            p = jnp.exp(s - m)
            ls = p.sum(axis=-1, keepdims=True)
            os_ = jnp.dot(p.astype(BF16), vb, preferred_element_type=F32)
            o = os_ if o is None else o + os_
            l = ls if l is None else l + ls
        outs.append(o / l)
    return jnp.where(lane < NA_HEAD_DIM, outs[0], outs[1])


NA_KROWS = 2 * NA_GROUP
NA_KTOK = NA_KROWS * GRID_W
NA_QCHUNK = 512


N_REL_R = 2 * NA_WIN_R - 1
N_REL_C = 2 * NA_WIN_C - 1


def _na_group_geometry(variant, rows):
    ng = rows // NA_GROUP
    g = {0: 0, 1: 1, 2: ng - 1}[variant]
    strip0 = int(np.clip(g * NA_GROUP - NA_WIN_R // 2, 0, rows - NA_KROWS))
    r0 = [int(np.clip(g * NA_GROUP + i - NA_WIN_R // 2, 0, rows - NA_WIN_R)) for i in range(NA_GROUP)]
    return strip0 - g * NA_GROUP, [r - g * NA_GROUP for r in r0]


def _bias_table_kernel(rpb_ref, o_ref, *, rows):
    layer, v, h = pl.program_id(0), pl.program_id(1), pl.program_id(2)
    shape = (GRID_W, LANES)
    qc = lax.broadcasted_iota(jnp.int32, shape, 0)
    lane = lax.broadcasted_iota(jnp.int32, shape, 1)
    kc = lane % GRID_W
    win0 = jnp.clip(qc - NA_WIN_C // 2, 0, GRID_W - NA_WIN_C)
    in_win = (kc >= win0) & (kc < win0 + NA_WIN_C)
    rel = jnp.clip(kc - qc + NA_WIN_C - 1, 0, N_REL_C - 1)
    left = lane < GRID_W
    base = (layer * NA_HEADS + h) * (N_REL_R * N_REL_C)
    neg = jnp.full(shape, NEG, F32)
    vals = []
    for a in range(N_REL_R):
        acc = jnp.zeros(shape, F32)
        for b in range(N_REL_C):
            acc = jnp.where(rel == b, rpb_ref[base + a * N_REL_C + b], acc)
        vals.append(jnp.where(in_win, acc, neg))

    for variant in range(3):
        strip_rel, r0_rel = _na_group_geometry(variant, rows)

        def half(i, j):
            key_rel = strip_rel + j
            if r0_rel[i] <= key_rel < r0_rel[i] + NA_WIN_R:
                return vals[key_rel - i + NA_WIN_R - 1]
            return neg

        @pl.when(v == variant)
        def _():
            for i in range(NA_GROUP):
                for jp in range(NA_KROWS // 2):
                    lft, rgt = half(i, 2 * jp), half(i, 2 * jp + 1)
                    blk = lft if lft is rgt else jnp.where(left, lft, rgt)
                    o_ref[0, 0, 0, i * GRID_W:(i + 1) * GRID_W, jp * LANES:(jp + 1) * LANES] = blk


def _bias_tables(na_rpb, rows):
    depth = na_rpb.shape[0]
    return pl.pallas_call(
        functools.partial(_bias_table_kernel, rows=rows),
        grid=(depth, 3, NA_HEADS),
        in_specs=[pl.BlockSpec(memory_space=pltpu.SMEM)],
        out_specs=pl.BlockSpec((1, 1, 1, NA_GTOK, NA_KTOK), lambda l, v, h: (l, v, h, 0, 0)),
        out_shape=jax.ShapeDtypeStruct((depth, 3, NA_HEADS, NA_GTOK, NA_KTOK), F32),
        compiler_params=_cparams(("arbitrary", "arbitrary", "arbitrary")),
        name="na_bias",
    )(na_rpb.reshape(-1))


def _na_kernel(q_ref, kp_ref, kc_ref, kn_ref, vp_ref, vc_ref, vn_ref, kx_ref, vx_ref, bias_ref, o_ref,
               kcat, vcat, *, rows):
    g = pl.program_id(0)
    kcat[0:NA_GTOK] = kp_ref[0]
    kcat[NA_GTOK:2 * NA_GTOK] = kc_ref[0]
    kcat[2 * NA_GTOK:3 * NA_GTOK] = kn_ref[0]
    vcat[0:NA_GTOK] = vp_ref[0]
    vcat[NA_GTOK:2 * NA_GTOK] = vc_ref[0]
    vcat[2 * NA_GTOK:3 * NA_GTOK] = vn_ref[0]
    strip0 = jnp.clip(g * NA_GROUP - NA_WIN_R // 2, 0, rows - NA_KROWS)
    start = pl.multiple_of((strip0 - g * NA_GROUP + NA_GROUP) * GRID_W, GRID_W)
    for pr in range(NA_HEADS // 2):
        ps = slice(pr * LANES, (pr + 1) * LANES)
        k_w = kcat[pl.ds(start, NA_KTOK), ps]
        v_w = vcat[pl.ds(start, NA_KTOK), ps]
        for c in range(NA_GTOK // NA_QCHUNK):
            rs = slice(c * NA_QCHUNK, (c + 1) * NA_QCHUNK)

            def bias_fn(sub, idx, s, pr=pr, rs=rs):
                return s + bias_ref[0, 0, 2 * pr + sub, rs, :] if idx == 0 else s

            o = _pair_softmax_pv(q_ref[0, rs, ps], [k_w, kx_ref[0, :, ps]], [v_w, vx_ref[0, :, ps]], bias_fn)
            o_ref[0, rs, ps] = o.astype(BF16)


def _na(z, zc, bias, layer):
    b, s, _ = z.shape
    lc = zc.shape[1]
    rows = s // GRID_W
    ng = rows // NA_GROUP
    blk = (1, NA_GTOK, CB)

    def spec(col, off):
        return pl.BlockSpec(blk, lambda g, bi: (bi, jnp.clip(g + off, 0, ng - 1), col // CB))

    def bias_map(g, bi):
        return (layer, jnp.where(g == 0, 0, jnp.where(g == ng - 1, 2, 1)), 0, 0, 0)

    return pl.pallas_call(
        functools.partial(_na_kernel, rows=rows),
        grid=(ng, b),
        in_specs=[
            spec(Z_Q, 0),
            spec(Z_K, -1), spec(Z_K, 0), spec(Z_K, 1),
            spec(Z_V, -1), spec(Z_V, 0), spec(Z_V, 1),
            pl.BlockSpec((1, lc, CB), lambda g, bi: (bi, 0, Z_K // CB)),
            pl.BlockSpec((1, lc, CB), lambda g, bi: (bi, 0, Z_V // CB)),
            pl.BlockSpec((1, 1) + bias.shape[2:], bias_map, pipeline_mode=pl.Buffered(1)),
        ],
        out_specs=pl.BlockSpec(blk, lambda g, bi: (bi, g, 0)),
        out_shape=jax.ShapeDtypeStruct((b, s, NA_WIDTH), BF16),
        scratch_shapes=[pltpu.VMEM((3 * NA_GTOK, CB), BF16), pltpu.VMEM((3 * NA_GTOK, CB), BF16)],
        compiler_params=_cparams(("parallel", "arbitrary")),
        name="na_attn",
    )(z, z, z, z, z, z, z, zc, zc, bias)


def _ctx_na_kernel(q_ref, k_ref, v_ref, o_ref):
    for pr in range(NA_HEADS // 2):
        ps = slice(pr * LANES, (pr + 1) * LANES)
        o = _pair_softmax_pv(q_ref[0, :, ps], [k_ref[0, :, ps]], [v_ref[0, :, ps]], lambda sub, idx, s: s)
        o_ref[0, :, ps] = o.astype(BF16)


def _ctx_na(zc):
    b, lc, _ = zc.shape
    spec = lambda col: pl.BlockSpec((1, lc, CB), lambda bi: (bi, 0, col // CB))
    return pl.pallas_call(
        _ctx_na_kernel,
        grid=(b,),
        in_specs=[spec(Z_Q), spec(Z_K), spec(Z_V)],
        out_specs=pl.BlockSpec((1, lc, NA_WIDTH), lambda bi: (bi, 0, 0)),
        out_shape=jax.ShapeDtypeStruct((b, lc, NA_WIDTH), BF16),
        compiler_params=_cparams(("parallel",)),
        name="ctx_na",
    )(zc, zc, zc)


def _pool_kernel(up_ref, u_ref, un_ref, w_ref, sc_ref, o_ref, *, seq_len):
    i = pl.program_id(1)
    tm = u_ref.shape[1]
    ext = jnp.concatenate([up_ref[0], u_ref[0], un_ref[0]], axis=0).astype(F32)
    n_ext = tm + 2 * POOL_HALO
    tg = i * tm - POOL_HALO + lax.broadcasted_iota(jnp.int32, (n_ext, 1), 0)
    ext = jnp.where((tg >= 0) & (tg < seq_len), ext, 0.0)
    t = i * tm + lax.broadcasted_iota(jnp.int32, (tm, 1), 0)

    def shifted(a, k):
        return pltpu.roll(a, (-k) % n_ext, 0)

    for gi, win in enumerate(POOL_WINDOWS):
        gs = slice(gi * POOL_GROUP, (gi + 1) * POOL_GROUP)
        a = ext[:, gs]
        wsum = a + shifted(a, -1)
        half = 1
        while 2 * half < win:
            wsum = shifted(wsum, -half) + shifted(wsum, half)
            half *= 2
        cnt = (jnp.minimum(t + win // 2, seq_len) - jnp.maximum(t - win // 2, 0)).astype(F32)
        d = wsum[POOL_HALO:POOL_HALO + tm] / cnt - a[POOL_HALO:POOL_HALO + tm]
        y = jnp.dot(d.astype(BF16), w_ref[gi], preferred_element_type=F32)
        o_ref[0, :, gs] = (y * sc_ref[:, gs]).astype(BF16)


def _pool(z, w, sc, layer, tm):
    b, l, _ = z.shape
    tm = min(tm, l)
    hb = tm // POOL_HALO
    nhb = l // POOL_HALO
    col = Z_U // CB
    return pl.pallas_call(
        functools.partial(_pool_kernel, seq_len=l),
        grid=(b, l // tm),
        in_specs=[
            pl.BlockSpec((1, POOL_HALO, CB), lambda bi, i: (bi, jnp.maximum(i * hb - 1, 0), col)),
            pl.BlockSpec((1, tm, CB), lambda bi, i: (bi, i, col)),
            pl.BlockSpec((1, POOL_HALO, CB), lambda bi, i: (bi, jnp.minimum((i + 1) * hb, nhb - 1), col)),
            pl.BlockSpec((None,) + w.shape[1:], lambda bi, i: (layer, 0, 0, 0)),
            pl.BlockSpec((1, POOL_WIDTH), lambda bi, i: (0, 0)),
        ],
        out_specs=pl.BlockSpec((1, tm, POOL_WIDTH), lambda bi, i: (bi, i, 0)),
        out_shape=jax.ShapeDtypeStruct((b, l, POOL_WIDTH), BF16),
        compiler_params=_cparams(("parallel", "parallel")),
        name="pool",
    )(z, z, z, w, sc)


def _merge_kernel(ona_ref, opool_ref, omla_ref, g0_ref, g1_ref, g2_ref, wb_ref, wo_ref, x_ref, gate_ref,
                  gain_ref, o_ref):
    m = None
    for br_ref, g_ref, k in ((ona_ref, g0_ref, 0), (opool_ref, g1_ref, 1), (omla_ref, g2_ref, 2)):
        proj = jnp.dot(br_ref[0], wb_ref[k], preferred_element_type=F32)
        term = g_ref[0].astype(F32) * proj
        m = term if m is None else m + term
    y = jnp.dot(m.astype(BF16), wo_ref[...], preferred_element_type=F32)
    o_ref[0] = x_ref[0] + gate_ref[0] * _rms(y, gain_ref[...])


def _merge(o_na, o_pool, o_mla, z, wb, wo, x, gate, gain, layer, tm):
    b, l, d = x.shape
    tm = min(tm, l)
    per_batch = gate.shape[0] > 1
    mod_map = (lambda bi, i: (bi, 0, 0)) if per_batch else (lambda bi, i: (0, 0, 0))
    br = pl.BlockSpec((1, tm, BRANCH_W), lambda bi, i: (bi, i, 0))
    gspec = lambda k: pl.BlockSpec((1, tm, d), lambda bi, i: (bi, i, k))
    return pl.pallas_call(
        _merge_kernel,
        grid=(b, l // tm),
        in_specs=[
            br, br, br, gspec(0), gspec(1), gspec(2),
            pl.BlockSpec((None,) + wb.shape[1:], lambda bi, i: (layer, 0, 0, 0)),
            pl.BlockSpec((None,) + wo.shape[1:], lambda bi, i: (layer, 0, 0)),
            pl.BlockSpec((1, tm, d), lambda bi, i: (bi, i, 0)),
            pl.BlockSpec((1, 1, d), mod_map),
            pl.BlockSpec((1, d), lambda bi, i: (0, 0)),
        ],
        out_specs=pl.BlockSpec((1, tm, d), lambda bi, i: (bi, i, 0)),
        out_shape=jax.ShapeDtypeStruct((b, l, d), F32),
        compiler_params=_cparams(("parallel", "parallel")),
        name="merge",
    )(o_na, o_pool, o_mla, z, z, z, wb, wo, x, gate, gain)


def _gelu_tanh(x):
    return 0.5 * x * (1.0 + jnp.tanh(np.float32(np.sqrt(2.0 / np.pi)) * (x + np.float32(0.044715) * (x * x * x))))


def _ffn_kernel(xp_ref, x_ref, xn_ref, gain_ref, sc_ref, sh_ref, wa_ref, wb_ref, cwa_ref, cwb_ref, cba_ref,
                cbb_ref, wd_ref, gate_ref, gpost_ref, o_ref, h_scr, acc_scr):
    i = pl.program_id(1)
    j = pl.program_id(2)
    ni = pl.num_programs(1)
    nj = pl.num_programs(2)
    tm = x_ref.shape[1]
    n_ext = tm + 2 * CONV_HALO

    @pl.when(j == 0)
    def _():
        def norm_mod(xv):
            return _rms(xv, gain_ref[...]) * (1.0 + sc_ref[0]) + sh_ref[0]

        hp = jnp.where(i > 0, norm_mod(xp_ref[0]), 0.0)
        hn = jnp.where(i < ni - 1, norm_mod(xn_ref[0]), 0.0)
        h_scr[0:CONV_HALO] = hp.astype(BF16)
        h_scr[CONV_HALO:CONV_HALO + tm] = norm_mod(x_ref[0]).astype(BF16)
        h_scr[CONV_HALO + tm:n_ext] = hn.astype(BF16)
        acc_scr[...] = jnp.zeros(acc_scr.shape, F32)

    h = h_scr[...]

    def conv_half(w_ref, cw_ref, cb_ref):
        u = jnp.dot(h, w_ref[...], preferred_element_type=F32)
        prev = pltpu.roll(u, 1, 0)[CONV_HALO:CONV_HALO + tm]
        nxt = pltpu.roll(u, n_ext - 1, 0)[CONV_HALO:CONV_HALO + tm]
        cur = u[CONV_HALO:CONV_HALO + tm]
        return cb_ref[...] + prev * cw_ref[0:1] + cur * cw_ref[1:2] + nxt * cw_ref[2:3]

    a = conv_half(wa_ref, cwa_ref, cba_ref)
    bgate = conv_half(wb_ref, cwb_ref, cbb_ref)
    act = (_gelu_tanh(a) * bgate).astype(BF16)
    acc_scr[...] += jnp.dot(act, wd_ref[...], preferred_element_type=F32)

    @pl.when(j == nj - 1)
    def _():
        o_ref[0] = x_ref[0] + gate_ref[0] * _rms(acc_scr[...], gpost_ref[...])


def _ffn(x, gain, sc, sh, w_up, conv_w, conv_b, w_down, gate, gpost, layer, tm, tn):
    b, l, d = x.shape
    tm = min(tm, l)
    nch = D_FF // tn
    hb = tm // CONV_HALO
    nhb = l // CONV_HALO
    per_batch = sc.shape[0] > 1
    mod_map = (lambda bi, i, j: (bi, 0, 0)) if per_batch else (lambda bi, i, j: (0, 0, 0))
    mod = pl.BlockSpec((1, 1, d), mod_map)
    vec = pl.BlockSpec((1, d), lambda bi, i, j: (0, 0))
    return pl.pallas_call(
        _ffn_kernel,
        grid=(b, l // tm, nch),
        in_specs=[
            pl.BlockSpec((1, CONV_HALO, d), lambda bi, i, j: (bi, jnp.maximum(i * hb - 1, 0), 0)),
            pl.BlockSpec((1, tm, d), lambda bi, i, j: (bi, i, 0)),
            pl.BlockSpec((1, CONV_HALO, d), lambda bi, i, j: (bi, jnp.minimum((i + 1) * hb, nhb - 1), 0)),
            vec, mod, mod,
            pl.BlockSpec((None, d, tn), lambda bi, i, j: (layer, 0, j)),
            pl.BlockSpec((None, d, tn), lambda bi, i, j: (layer, 0, nch + j)),
            pl.BlockSpec((None, 3, tn), lambda bi, i, j: (layer, 0, j)),
            pl.BlockSpec((None, 3, tn), lambda bi, i, j: (layer, 0, nch + j)),
            pl.BlockSpec((None, 1, tn), lambda bi, i, j: (layer, 0, j)),
            pl.BlockSpec((None, 1, tn), lambda bi, i, j: (layer, 0, nch + j)),
            pl.BlockSpec((None, tn, d), lambda bi, i, j: (layer, j, 0)),
            mod, vec,
        ],
        out_specs=pl.BlockSpec((1, tm, d), lambda bi, i, j: (bi, i, 0)),
        out_shape=jax.ShapeDtypeStruct((b, l, d), F32),
        scratch_shapes=[pltpu.VMEM((tm + 2 * CONV_HALO, d), BF16), pltpu.VMEM((tm, d), F32)],
        compiler_params=_cparams(("parallel", "parallel", "arbitrary")),
        name="ffn",
    )(x, x, x, gain, sc, sh, w_up, w_up, conv_w, conv_w, conv_b, conv_b, w_down, gate, gpost)


_ROPE_PERM = np.concatenate([np.arange(8, 16), np.arange(0, 8), np.arange(24, 32), np.arange(16, 24)])


def _rope_tables(n_tok, rotate):
    f32 = np.float32
    ct = np.ones((n_tok, HEAD_PAD), f32)
    st = np.zeros((n_tok, HEAD_PAD), f32)
    if rotate:
        n_freq = MLA_ROPE // 4
        inv = (f32(ROPE_BASE) ** (-np.arange(n_freq, dtype=f32) / f32(n_freq))).astype(f32)
        t = np.arange(n_tok, dtype=np.int32)
        ang_r = ((t // GRID_W).astype(f32)[:, None] * inv[None, :]).astype(f32)
        ang_c = ((t % GRID_W).astype(f32)[:, None] * inv[None, :]).astype(f32)
        cr, sr, cc, sn = np.cos(ang_r), np.sin(ang_r), np.cos(ang_c), np.sin(ang_c)
        ct[:, MLA_NOPE:MLA_NOPE + MLA_ROPE] = np.concatenate([cr, cr, cc, cc], axis=1)
        st[:, MLA_NOPE:MLA_NOPE + MLA_ROPE] = np.concatenate([-sr, sr, -sn, sn], axis=1)
    return jnp.asarray(ct), jnp.asarray(st)


def _prep_all(w_in, w_uq, w_ukv):
    depth, d, _ = w_in.shape
    o = 0
    parts = []
    for s in (NA_WIDTH, NA_WIDTH, NA_WIDTH, POOL_WIDTH, MLA_Q_RANK, MLA_KV_RANK, MLA_ROPE, N_BRANCH * D_MODEL):
        parts.append(w_in[..., o:o + s])
        o += s
    wq, wk, wv, wu, wcq, wckv, wkr, wg = parts
    z64 = jnp.zeros((depth, d, MLA_NOPE), F32)
    z32 = jnp.zeros((depth, d, HEAD_PAD - MLA_NOPE - MLA_ROPE), F32)
    w_z = jnp.concatenate([wq * (NA_HEAD_DIM ** -0.5), wk, wv, wu, wcq, wckv,
                           z64, wkr, z32, z64, wkr[..., _ROPE_PERM], z32], axis=-1)

    uq = w_uq.reshape(depth, MLA_Q_RANK, MLA_HEADS, MLA_NOPE + MLA_ROPE)
    qz64 = jnp.zeros((depth, MLA_Q_RANK, MLA_HEADS, MLA_NOPE), F32)
    qz32 = jnp.zeros((depth, MLA_Q_RANK, MLA_HEADS, HEAD_PAD - MLA_NOPE - MLA_ROPE), F32)
    wq_pad = jnp.concatenate([uq, qz32], axis=-1)
    wq_perm = jnp.concatenate([qz64, uq[..., MLA_NOPE:][..., _ROPE_PERM], qz32], axis=-1)
    assert MLA_NOPE + MLA_V == HEAD_PAD

    flat = lambda w, k: w.reshape(depth, k, MLA_PAD_W).astype(BF16)
    return dict(
        w_z=w_z.astype(BF16),
        w_g=wg.astype(BF16),
        wq_pad=flat(wq_pad, MLA_Q_RANK),
        wq_perm=flat(wq_perm, MLA_Q_RANK),
        w_kv=w_ukv.astype(BF16),
    )


TM_PROJ = 1024
TN_PROJ = 1024
TM_UP = 512
TQ_FLASH = 1024
TK_FLASH = 2048
TM_POOL = 1024
TM_MERGE = 1024
TM_FFN = 1024
TN_FFN = 1408


def kernel(x, c, ctx, c_ctx, w_ada, b_ada, norm_pre1, norm_post1, norm_pre2, norm_post2, w_in, na_rpb, pool_w,
           pool_scale, mla_q_norm, w_uq, mla_kv_norm, w_ukv, w_branch, w_o, w_up, conv_w, conv_b, w_down):
    b, s, d = x.shape
    lc = ctx.shape[1]
    depth = w_ada.shape[0]
    rows = s // GRID_W
    assert d == D_MODEL and s % NA_KTOK == 0 and lc % POOL_HALO == 0

    n_mod = -(-(b + 1) // 8) * 8
    cvec = jnp.concatenate([c, c_ctx[None, :], jnp.zeros((n_mod - b - 1, d), F32)], axis=0)
    mod = _ada(cvec, w_ada, b_ada).reshape(depth, n_mod, 6, d)

    ct_l, st_l = _rope_tables(s, True)
    ct_c, st_c = _rope_tables(b * lc, False)

    p = _prep_all(w_in, w_uq, w_ukv)
    na_bias = _bias_tables(na_rpb, rows)
    pw, wb, wo = pool_w.astype(BF16), w_branch.astype(BF16), w_o.astype(BF16)
    wup, wdn = w_up.astype(BF16), w_down.astype(BF16)
    cb = conv_b[:, None, :]

    xc = ctx
    for l in range(depth):
        last = l == depth - 1
        lat = [mod[l, :b, k][:, None, :] for k in range(6)]
        cx = [mod[l, b:b + 1, k][:, None, :] for k in range(6)]
        row = lambda v: v.reshape(1, -1)
        g_pre1, g_post1, g_pre2, g_post2 = row(norm_pre1[l]), row(norm_post1[l]), row(norm_pre2[l]), row(norm_post2[l])
        qg, kvg = row(mla_q_norm[l]), row(mla_kv_norm[l])
        psc = row(pool_scale[l])

        z, gz = _in_proj(x, g_pre1, lat[1], lat[0], p["w_z"], p["w_g"], l, TM_PROJ, TN_PROJ)
        flat = lambda a: a.reshape(1, b * lc, a.shape[-1])
        unflat = lambda a: a.reshape(b, lc, a.shape[-1])
        zc_f, gzc_f = _in_proj(flat(xc), g_pre1, cx[1], cx[0], p["w_z"], p["w_g"], l, TM_PROJ, TN_PROJ)
        zc, gzc = unflat(zc_f), unflat(gzc_f)
        mla_w = (p["wq_pad"], p["wq_perm"], p["w_kv"], l)
        q_l, k_l, v_l = _mla_up(z, qg, kvg, ct_l, st_l, *mla_w, TM_UP)
        q_c, k_c, v_c = [unflat(a) for a in _mla_up(zc_f, qg, kvg, ct_c, st_c, *mla_w, TM_UP)]

        o_na = _na(z, zc, na_bias, l)
        o_pool = _pool(z, pw, psc, l, TM_POOL)
        o_mla = _flash(q_l, k_c, v_c, k_l, v_l, TQ_FLASH, TK_FLASH)
        x = _merge(o_na, o_pool, o_mla, gz, wb, wo, x, lat[2], g_post1, l, TM_MERGE)
        x = _ffn(x, g_pre2, lat[4], lat[3], wup, conv_w, cb, wdn, lat[5], g_post2, l, TM_FFN, TN_FFN)

        if not last:
            oc_na = _ctx_na(zc)
            oc_pool = _pool(zc, pw, psc, l, TM_POOL)
            oc_mla = _flash(q_c, k_c, v_c, None, None, TQ_FLASH, TK_FLASH)
            xc = unflat(_merge(flat(oc_na), flat(oc_pool), flat(oc_mla), gzc_f, wb, wo, flat(xc), cx[2], g_post1, l,
                               TM_MERGE))
            xc = _ffn(xc, g_pre2, cx[4], cx[3], wup, conv_w, cb, wdn, cx[5], g_post2, l, TM_FFN, TN_FFN)
    return x
```

```python
import functools

import jax
import jax.numpy as jnp
import numpy as np
from jax import lax
from jax.experimental import pallas as pl
from jax.experimental.pallas import tpu as pltpu

F32 = jnp.float32
BF16 = jnp.bfloat16

D_MODEL = 1024
GRID_W = 64
EPS = 1e-6
NA_HEADS = 8
NA_HEAD_DIM = 64
NA_WIDTH = NA_HEADS * NA_HEAD_DIM
NA_WIN_R = 8
NA_WIN_C = 16
POOL_WINDOWS = (2, 4, 8, 16)
POOL_GROUP = 128
POOL_WIDTH = POOL_GROUP * len(POOL_WINDOWS)
POOL_HALO = 16
MLA_HEADS = 8
MLA_NOPE = 64
MLA_ROPE = 32
MLA_V = 64
MLA_Q_RANK = 512
MLA_KV_RANK = 256
ROPE_BASE = 10000.0
N_BRANCH = 3
BRANCH_W = 512
D_FF = 2816
CONV_HALO = 8

LANES = 128
HEAD_PAD = LANES
MLA_PAD_W = MLA_HEADS * HEAD_PAD
MLA_SUM_LANE = 0

Z_Q = 0
Z_K = Z_Q + NA_WIDTH
Z_V = Z_K + NA_WIDTH
Z_U = Z_V + NA_WIDTH
Z_CQ = Z_U + POOL_WIDTH
Z_KV = Z_CQ + MLA_Q_RANK
Z_COLS = Z_KV + 512
G_COLS = N_BRANCH * D_MODEL
assert G_COLS == Z_COLS
CB = 512

NEG = -1e30
VMEM_LIMIT = 56 * 1024 * 1024


def _cparams(sem):
    return pltpu.CompilerParams(dimension_semantics=sem, vmem_limit_bytes=VMEM_LIMIT)


def _rms(x, gain):
    return x * lax.rsqrt(jnp.mean(x * x, axis=-1, keepdims=True) + EPS) * gain


def _ada_kernel(c_ref, w_ref, b_ref, o_ref):
    c = c_ref[...]
    s = c * jax.nn.sigmoid(c)
    o_ref[0] = jnp.dot(s.astype(BF16), w_ref[0].astype(BF16), preferred_element_type=F32) + b_ref[0]


def _ada(cvec, w_ada, b_ada):
    depth, d, n = w_ada.shape
    rows = cvec.shape[0]
    tn = 1536
    return pl.pallas_call(
        _ada_kernel,
        grid=(depth, n // tn),
        in_specs=[
            pl.BlockSpec((rows, d), lambda l, j: (0, 0)),
            pl.BlockSpec((1, d, tn), lambda l, j: (l, 0, j)),
            pl.BlockSpec((1, 1, tn), lambda l, j: (l, 0, j)),
        ],
        out_specs=pl.BlockSpec((1, rows, tn), lambda l, j: (l, 0, j)),
        out_shape=jax.ShapeDtypeStruct((depth, rows, n), F32),
        compiler_params=_cparams(("arbitrary", "arbitrary")),
        name="ada",
    )(cvec, w_ada, b_ada.reshape(depth, 1, n))


def _in_proj_kernel(x_ref, gain_ref, sc_ref, sh_ref, wz_ref, wg_ref, z_ref, g_ref, h_scr):
    j = pl.program_id(2)

    @pl.when(j == 0)
    def _():
        h = _rms(x_ref[0], gain_ref[...]) * (1.0 + sc_ref[0]) + sh_ref[0]
        h_scr[...] = h.astype(BF16)

    h = h_scr[...]
    g_ref[0] = jax.nn.sigmoid(jnp.dot(h, wg_ref[...], preferred_element_type=F32)).astype(BF16)
    z_ref[0] = jnp.dot(h, wz_ref[...], preferred_element_type=F32).astype(BF16)


def _in_proj(x, gain, sc, sh, wz, wg, layer, tm, tn):
    b, l, d = x.shape
    tm = min(tm, l)
    per_batch = sc.shape[0] > 1
    mod_map = (lambda bi, i, j: (bi, 0, 0)) if per_batch else (lambda bi, i, j: (0, 0, 0))
    wspec = pl.BlockSpec((None, d, tn), lambda bi, i, j: (layer, 0, j))
    ospec = pl.BlockSpec((1, tm, tn), lambda bi, i, j: (bi, i, j))
    oshape = jax.ShapeDtypeStruct((b, l, Z_COLS), BF16)
    return pl.pallas_call(
        _in_proj_kernel,
        grid=(b, l // tm, Z_COLS // tn),
        in_specs=[
            pl.BlockSpec((1, tm, d), lambda bi, i, j: (bi, i, 0)),
            pl.BlockSpec((1, d), lambda bi, i, j: (0, 0)),
            pl.BlockSpec((1, 1, d), mod_map),
            pl.BlockSpec((1, 1, d), mod_map),
            wspec, wspec,
        ],
        out_specs=[ospec, ospec],
        out_shape=[oshape, oshape],
        scratch_shapes=[pltpu.VMEM((tm, d), BF16)],
        compiler_params=_cparams(("parallel", "parallel", "arbitrary")),
        name="in_proj",
    )(x, gain, sc, sh, wz, wg)


def _mla_up_kernel(cq_ref, kv_ref, qg_ref, kvg_ref, ct_ref, st_ref, wq_ref, wqp_ref, wkv_ref,
                   q_out, k_out, v_out, *, scale):
    cqn = _rms(cq_ref[0].astype(F32), qg_ref[...]).astype(BF16)
    kvb = kv_ref[0].astype(F32)
    kvn = _rms(kvb[:, :MLA_KV_RANK], kvg_ref[...]).astype(BF16)
    ct = ct_ref[...]
    st = st_ref[...]
    kr = kvb[:, MLA_KV_RANK:MLA_KV_RANK + LANES] * ct + kvb[:, MLA_KV_RANK + LANES:] * st
    qm = jnp.dot(cqn, wq_ref[...], preferred_element_type=F32)
    qp = jnp.dot(cqn, wqp_ref[...], preferred_element_type=F32)
    kv = jnp.dot(kvn, wkv_ref[...], preferred_element_type=F32)
    cts = ct * scale
    sts = st * scale
    lane = lax.broadcasted_iota(jnp.int32, kr.shape, 1)
    is_k = lane < MLA_NOPE
    row_sum_col = jnp.where(lane == MLA_SUM_LANE, 1.0, 0.0)
    for h in range(MLA_HEADS):
        hs = slice(h * HEAD_PAD, (h + 1) * HEAD_PAD)
        q_out[0, :, hs] = (qm[:, hs] * cts + qp[:, hs] * sts).astype(BF16)
        k_out[0, :, hs] = jnp.where(is_k, kv[:, hs], kr).astype(BF16)
        v_out[0, :, hs] = jnp.where(is_k, row_sum_col, kv[:, hs]).astype(BF16)


def _mla_up(z, qg, kvg, ct, st, wq, wqp, wkv, layer, tm):
    b, l, _ = z.shape
    tm = min(tm, l)
    full = lambda shape: pl.BlockSpec(shape, lambda bi, i: (0,) * len(shape))
    wspec = lambda k: pl.BlockSpec((None, k, MLA_PAD_W), lambda bi, i: (layer, 0, 0))
    out = jax.ShapeDtypeStruct((b, l, MLA_PAD_W), BF16)
    ospec = pl.BlockSpec((1, tm, MLA_PAD_W), lambda bi, i: (bi, i, 0))
    return pl.pallas_call(
        functools.partial(_mla_up_kernel, scale=float((MLA_NOPE + MLA_ROPE) ** -0.5 * np.log2(np.e))),
        grid=(b, l // tm),
        in_specs=[
            pl.BlockSpec((1, tm, CB), lambda bi, i: (bi, i, Z_CQ // CB)),
            pl.BlockSpec((1, tm, CB), lambda bi, i: (bi, i, Z_KV // CB)),
            full((1, MLA_Q_RANK)),
            full((1, MLA_KV_RANK)),
            pl.BlockSpec((tm, LANES), lambda bi, i: (i, 0)),
            pl.BlockSpec((tm, LANES), lambda bi, i: (i, 0)),
            wspec(MLA_Q_RANK), wspec(MLA_Q_RANK), wspec(MLA_KV_RANK),
        ],
        out_specs=[ospec, ospec, ospec],
        out_shape=[out, out, out],
        compiler_params=_cparams(("parallel", "parallel")),
        name="mla_up",
    )(z, z, qg, kvg, ct, st, wq, wqp, wkv)


def _flash_kernel(*refs, has_latent):
    if has_latent:
        q_ref, kc_ref, vc_ref, k_ref, v_ref, o_ref, m_scr, acc_scr = refs
    else:
        q_ref, kc_ref, vc_ref, o_ref, m_scr, acc_scr = refs
        k_ref = v_ref = None
    kk = pl.program_id(2)
    nk = pl.num_programs(2)

    def attend(h, kblk, vblk):
        hs = slice(h * HEAD_PAD, (h + 1) * HEAD_PAD)
        q = q_ref[0, :, hs]
        s = lax.dot_general(q, kblk[0, :, hs], (((1,), (1,)), ((), ())), preferred_element_type=F32)
        m_prev = m_scr[h]
        m_new = jnp.maximum(m_prev, jnp.max(s, axis=-1, keepdims=True))
        alpha = jnp.exp2(m_prev - m_new)
        p = jnp.exp2(s - jnp.concatenate([m_new] * (s.shape[1] // LANES), axis=1))
        acc_scr[h] = alpha * acc_scr[h] + jnp.dot(p.astype(BF16), vblk[0, :, hs], preferred_element_type=F32)
        m_scr[h] = m_new

    @pl.when(kk == 0)
    def _():
        m_scr[...] = jnp.full(m_scr.shape, NEG, F32)
        acc_scr[...] = jnp.zeros(acc_scr.shape, F32)
        for h in range(MLA_HEADS):
            attend(h, kc_ref, vc_ref)

    if has_latent:
        for h in range(MLA_HEADS):
            attend(h, k_ref, v_ref)

    @pl.when(kk == nk - 1)
    def _():
        for h in range(MLA_HEADS):
            acc = acc_scr[h]
            o_ref[0, :, h * MLA_V:(h + 1) * MLA_V] = (
                acc[:, HEAD_PAD - MLA_V:] / acc[:, MLA_SUM_LANE:MLA_SUM_LANE + 1]).astype(BF16)


def _flash(q, kc, vc, k, v, tq, tk):
    b, lq, _ = q.shape
    lc = kc.shape[1]
    tq = min(tq, lq)
    has_latent = k is not None
    in_specs = [
        pl.BlockSpec((1, tq, MLA_PAD_W), lambda bi, i, kk: (bi, i, 0)),
        pl.BlockSpec((1, lc, MLA_PAD_W), lambda bi, i, kk: (bi, 0, 0)),
        pl.BlockSpec((1, lc, MLA_PAD_W), lambda bi, i, kk: (bi, 0, 0)),
    ]
    args = [q, kc, vc]
    nk = 1
    if has_latent:
        tk = min(tk, k.shape[1])
        nk = k.shape[1] // tk
        in_specs += [pl.BlockSpec((1, tk, MLA_PAD_W), lambda bi, i, kk: (bi, kk, 0))] * 2
        args += [k, v]
    return pl.pallas_call(
        functools.partial(_flash_kernel, has_latent=has_latent),
        grid=(b, lq // tq, nk),
        in_specs=in_specs,
        out_specs=pl.BlockSpec((1, tq, MLA_HEADS * MLA_V), lambda bi, i, kk: (bi, i, 0)),
        out_shape=jax.ShapeDtypeStruct((b, lq, MLA_HEADS * MLA_V), BF16),
        scratch_shapes=[pltpu.VMEM((MLA_HEADS, tq, LANES), F32), pltpu.VMEM((MLA_HEADS, tq, LANES), F32)],
        compiler_params=_cparams(("parallel", "parallel", "arbitrary")),
        name="mla_flash" if has_latent else "mla_ctx",
    )(*args)


NA_GROUP = 8
NA_GTOK = NA_GROUP * GRID_W


def _pair_softmax_pv(qp, k_list, v_list, bias_fn):
    lane = lax.broadcasted_iota(jnp.int32, qp.shape, 1)
    outs = []
    for sub in range(2):
        in_head = (lane >= sub * NA_HEAD_DIM) & (lane < (sub + 1) * NA_HEAD_DIM)
        qm = jnp.where(in_head, qp, jnp.zeros_like(qp))
        s_list = []
        for idx, kb in enumerate(k_list):
            s = lax.dot_general(qm, kb, (((1,), (1,)), ((), ())), preferred_element_type=F32)
            s_list.append(bias_fn(sub, idx, s))
        m = s_list[0].max(axis=-1, keepdims=True)
        for s in s_list[1:]:
            m = jnp.maximum(m, s.max(axis=-1, keepdims=True))
        o = None
        l = None
        for s, vb in zip(s_list, v_list):
            p = jnp.exp(s - m)
            ls = p.sum(axis=-1, keepdims=True)
            os_ = jnp.dot(p.astype(BF16), vb, preferred_element_type=F32)
            o = os_ if o is None else o + os_
            l = ls if l is None else l + ls
        outs.append(o / l)
    return jnp.where(lane < NA_HEAD_DIM, outs[0], outs[1])


NA_KROWS = 2 * NA_GROUP
NA_KTOK = NA_KROWS * GRID_W


N_REL_R = 2 * NA_WIN_R - 1
N_REL_C = 2 * NA_WIN_C - 1


def _na_group_geometry(variant, rows):
    ng = rows // NA_GROUP
    g = {0: 0, 1: 1, 2: ng - 1}[variant]
    strip0 = int(np.clip(g * NA_GROUP - NA_WIN_R // 2, 0, rows - NA_KROWS))
    r0 = [int(np.clip(g * NA_GROUP + i - NA_WIN_R // 2, 0, rows - NA_WIN_R)) for i in range(NA_GROUP)]
    return strip0 - g * NA_GROUP, [r - g * NA_GROUP for r in r0]


def _bias_table_kernel(rpb_ref, o_ref, *, rows):
    layer, v, h = pl.program_id(0), pl.program_id(1), pl.program_id(2)
    shape = (GRID_W, LANES)
    qc = lax.broadcasted_iota(jnp.int32, shape, 0)
    lane = lax.broadcasted_iota(jnp.int32, shape, 1)
    kc = lane % GRID_W
    win0 = jnp.clip(qc - NA_WIN_C // 2, 0, GRID_W - NA_WIN_C)
    in_win = (kc >= win0) & (kc < win0 + NA_WIN_C)
    rel = jnp.clip(kc - qc + NA_WIN_C - 1, 0, N_REL_C - 1)
    left = lane < GRID_W
    base = (layer * NA_HEADS + h) * (N_REL_R * N_REL_C)
    neg = jnp.full(shape, NEG, F32)
    vals = []
    for a in range(N_REL_R):
        acc = jnp.zeros(shape, F32)
        for b in range(N_REL_C):
            acc = jnp.where(rel == b, rpb_ref[base + a * N_REL_C + b], acc)
        vals.append(jnp.where(in_win, acc, neg))

    for variant in range(3):
        strip_rel, r0_rel = _na_group_geometry(variant, rows)

        def half(i, j):
            key_rel = strip_rel + j
            if r0_rel[i] <= key_rel < r0_rel[i] + NA_WIN_R:
                return vals[key_rel - i + NA_WIN_R - 1]
            return neg

        @pl.when(v == variant)
        def _():
            for i in range(NA_GROUP):
                for jp in range(NA_KROWS // 2):
                    lft, rgt = half(i, 2 * jp), half(i, 2 * jp + 1)
                    blk = lft if lft is rgt else jnp.where(left, lft, rgt)
                    o_ref[0, 0, 0, i * GRID_W:(i + 1) * GRID_W, jp * LANES:(jp + 1) * LANES] = blk


def _bias_tables(na_rpb, rows):
    depth = na_rpb.shape[0]
    return pl.pallas_call(
        functools.partial(_bias_table_kernel, rows=rows),
        grid=(depth, 3, NA_HEADS),
        in_specs=[pl.BlockSpec(memory_space=pltpu.SMEM)],
        out_specs=pl.BlockSpec((1, 1, 1, NA_GTOK, NA_KTOK), lambda l, v, h: (l, v, h, 0, 0)),
        out_shape=jax.ShapeDtypeStruct((depth, 3, NA_HEADS, NA_GTOK, NA_KTOK), F32),
        compiler_params=_cparams(("arbitrary", "arbitrary", "arbitrary")),
        name="na_bias",
    )(na_rpb.reshape(-1))


def _na_windows(variant, rows):
    strip_rel, r0_rel = _na_group_geometry(variant, rows)
    out = []
    for rp in range(NA_GROUP // 2):
        w0 = [r0_rel[i] - strip_rel for i in (2 * rp, 2 * rp + 1)]
        lo, hi = min(w0) * GRID_W, (max(w0) + NA_WIN_R) * GRID_W
        out.append((lo // LANES * LANES, -(-hi // LANES) * LANES))
    return out


def _na_pair(qp, k_w, v_w, k_x, v_x, bias_ref, pr, windows):
    lane = lax.broadcasted_iota(jnp.int32, qp.shape, 1)
    pair_rows = 2 * GRID_W
    outs = []
    for sub in range(2):
        in_head = (lane >= sub * NA_HEAD_DIM) & (lane < (sub + 1) * NA_HEAD_DIM)
        qm = jnp.where(in_head, qp, jnp.zeros_like(qp))
        s_w = lax.dot_general(qm, k_w, (((1,), (1,)), ((), ())), preferred_element_type=F32)
        s_c = lax.dot_general(qm, k_x, (((1,), (1,)), ((), ())), preferred_element_type=F32)
        p_rows, pc_rows, l_rows = [], [], []
        for rp, (c0, c1) in enumerate(windows):
            rs = slice(rp * pair_rows, (rp + 1) * pair_rows)
            sw = s_w[rs, c0:c1] + bias_ref[0, 0, 2 * pr + sub, rs, c0:c1]
            sc = s_c[rs]
            m = jnp.maximum(sw.max(axis=-1, keepdims=True), sc.max(axis=-1, keepdims=True))
            pw = jnp.exp(sw - m)
            pc = jnp.exp(sc - m)
            l_rows.append(pw.sum(axis=-1, keepdims=True) + pc.sum(axis=-1, keepdims=True))
            parts = []
            if c0 > 0:
                parts.append(jnp.zeros((pair_rows, c0), BF16))
            parts.append(pw.astype(BF16))
            if c1 < NA_KTOK:
                parts.append(jnp.zeros((pair_rows, NA_KTOK - c1), BF16))
            p_rows.append(jnp.concatenate(parts, axis=1) if len(parts) > 1 else parts[0])
            pc_rows.append(pc.astype(BF16))
        o = (jnp.dot(jnp.concatenate(p_rows, axis=0), v_w, preferred_element_type=F32)
             + jnp.dot(jnp.concatenate(pc_rows, axis=0), v_x, preferred_element_type=F32))
        outs.append(o / jnp.concatenate(l_rows, axis=0))
    return jnp.where(lane < NA_HEAD_DIM, outs[0], outs[1])


def _na_kernel(q_ref, kp_ref, kc_ref, kn_ref, vp_ref, vc_ref, vn_ref, kx_ref, vx_ref, bias_ref, o_ref,
               kcat, vcat, *, rows):
    g = pl.program_id(0)
    ng = pl.num_programs(0)
    kcat[0:NA_GTOK] = kp_ref[0]
    kcat[NA_GTOK:2 * NA_GTOK] = kc_ref[0]
    kcat[2 * NA_GTOK:3 * NA_GTOK] = kn_ref[0]
    vcat[0:NA_GTOK] = vp_ref[0]
    vcat[NA_GTOK:2 * NA_GTOK] = vc_ref[0]
    vcat[2 * NA_GTOK:3 * NA_GTOK] = vn_ref[0]
    strip0 = jnp.clip(g * NA_GROUP - NA_WIN_R // 2, 0, rows - NA_KROWS)
    start = pl.multiple_of((strip0 - g * NA_GROUP + NA_GROUP) * GRID_W, GRID_W)
    variant = jnp.where(g == 0, 0, jnp.where(g == ng - 1, 2, 1))
    for vi in range(3):
        @pl.when(variant == vi)
        def _(vi=vi):
            windows = _na_windows(vi, rows)
            for pr in range(NA_HEADS // 2):
                ps = slice(pr * LANES, (pr + 1) * LANES)
                k_w = kcat[pl.ds(start, NA_KTOK), ps]
                v_w = vcat[pl.ds(start, NA_KTOK), ps]
                o = _na_pair(q_ref[0, :, ps], k_w, v_w, kx_ref[0, :, ps], vx_ref[0, :, ps], bias_ref, pr, windows)
                o_ref[0, :, ps] = o.astype(BF16)


def _na(z, zc, bias, layer):
    b, s, _ = z.shape
    lc = zc.shape[1]
    rows = s // GRID_W
    ng = rows // NA_GROUP
    blk = (1, NA_GTOK, CB)

    def spec(col, off):
        return pl.BlockSpec(blk, lambda g, bi: (bi, jnp.clip(g + off, 0, ng - 1), col // CB))

    def bias_map(g, bi):
        return (layer, jnp.where(g == 0, 0, jnp.where(g == ng - 1, 2, 1)), 0, 0, 0)

    return pl.pallas_call(
        functools.partial(_na_kernel, rows=rows),
        grid=(ng, b),
        in_specs=[
            spec(Z_Q, 0),
            spec(Z_K, -1), spec(Z_K, 0), spec(Z_K, 1),
            spec(Z_V, -1), spec(Z_V, 0), spec(Z_V, 1),
            pl.BlockSpec((1, lc, CB), lambda g, bi: (bi, 0, Z_K // CB)),
            pl.BlockSpec((1, lc, CB), lambda g, bi: (bi, 0, Z_V // CB)),
            pl.BlockSpec((1, 1) + bias.shape[2:], bias_map, pipeline_mode=pl.Buffered(1)),
        ],
        out_specs=pl.BlockSpec(blk, lambda g, bi: (bi, g, 0)),
        out_shape=jax.ShapeDtypeStruct((b, s, NA_WIDTH), BF16),
        scratch_shapes=[pltpu.VMEM((3 * NA_GTOK, CB), BF16), pltpu.VMEM((3 * NA_GTOK, CB), BF16)],
        compiler_params=_cparams(("parallel", "arbitrary")),
        name="na_attn",
    )(z, z, z, z, z, z, z, zc, zc, bias)


def _ctx_na_kernel(q_ref, k_ref, v_ref, o_ref):
    for pr in range(NA_HEADS // 2):
        ps = slice(pr * LANES, (pr + 1) * LANES)
        o = _pair_softmax_pv(q_ref[0, :, ps], [k_ref[0, :, ps]], [v_ref[0, :, ps]], lambda sub, idx, s: s)
        o_ref[0, :, ps] = o.astype(BF16)


def _ctx_na(zc):
    b, lc, _ = zc.shape
    spec = lambda col: pl.BlockSpec((1, lc, CB), lambda bi: (bi, 0, col // CB))
    return pl.pallas_call(
        _ctx_na_kernel,
        grid=(b,),
        in_specs=[spec(Z_Q), spec(Z_K), spec(Z_V)],
        out_specs=pl.BlockSpec((1, lc, NA_WIDTH), lambda bi: (bi, 0, 0)),
        out_shape=jax.ShapeDtypeStruct((b, lc, NA_WIDTH), BF16),
        compiler_params=_cparams(("parallel",)),
        name="ctx_na",
    )(zc, zc, zc)


def _pool_tile(up_ref, u_ref, un_ref, w_ref, sc_ref, i, seq_len):
    tm = u_ref.shape[1]
    ext = jnp.concatenate([up_ref[0], u_ref[0], un_ref[0]], axis=0).astype(F32)
    n_ext = tm + 2 * POOL_HALO
    tg = i * tm - POOL_HALO + lax.broadcasted_iota(jnp.int32, (n_ext, 1), 0)
    ext = jnp.where((tg >= 0) & (tg < seq_len), ext, 0.0)
    t = i * tm + lax.broadcasted_iota(jnp.int32, (tm, 1), 0)

    def shifted(a, k):
        return pltpu.roll(a, (-k) % n_ext, 0)

    outs = []
    for gi, win in enumerate(POOL_WINDOWS):
        gs = slice(gi * POOL_GROUP, (gi + 1) * POOL_GROUP)
        a = ext[:, gs]
        wsum = a + shifted(a, -1)
        half = 1
        while 2 * half < win:
            wsum = shifted(wsum, -half) + shifted(wsum, half)
            half *= 2
        cnt = (jnp.minimum(t + win // 2, seq_len) - jnp.maximum(t - win // 2, 0)).astype(F32)
        d = wsum[POOL_HALO:POOL_HALO + tm] / cnt - a[POOL_HALO:POOL_HALO + tm]
        y = jnp.dot(d.astype(BF16), w_ref[gi], preferred_element_type=F32)
        outs.append((y * sc_ref[:, gs]).astype(BF16))
    return jnp.concatenate(outs, axis=1)


def _merge_kernel(ona_ref, up_ref, u_ref, un_ref, omla_ref, g0_ref, g1_ref, g2_ref, pw_ref, psc_ref, wb_ref, wo_ref,
                  x_ref, gate_ref, gain_ref, o_ref, *, seq_len):
    o_pool = _pool_tile(up_ref, u_ref, un_ref, pw_ref, psc_ref, pl.program_id(1), seq_len)
    m = None
    for br, g_ref, k in ((ona_ref[0], g0_ref, 0), (o_pool, g1_ref, 1), (omla_ref[0], g2_ref, 2)):
        proj = jnp.dot(br, wb_ref[k], preferred_element_type=F32)
        term = g_ref[0].astype(F32) * proj
        m = term if m is None else m + term
    y = jnp.dot(m.astype(BF16), wo_ref[...], preferred_element_type=F32)
    o_ref[0] = x_ref[0] + gate_ref[0] * _rms(y, gain_ref[...])


def _merge(o_na, z, o_mla, gz, pw, psc, wb, wo, x, gate, gain, layer, tm):
    b, l, d = x.shape
    tm = min(tm, l)
    hb = tm // POOL_HALO
    nhb = l // POOL_HALO
    ucol = Z_U // CB
    per_batch = gate.shape[0] > 1
    mod_map = (lambda bi, i: (bi, 0, 0)) if per_batch else (lambda bi, i: (0, 0, 0))
    br = pl.BlockSpec((1, tm, BRANCH_W), lambda bi, i: (bi, i, 0))
    gspec = lambda k: pl.BlockSpec((1, tm, d), lambda bi, i: (bi, i, k))
    return pl.pallas_call(
        functools.partial(_merge_kernel, seq_len=l),
        grid=(b, l // tm),
        in_specs=[
            br,
            pl.BlockSpec((1, POOL_HALO, CB), lambda bi, i: (bi, jnp.maximum(i * hb - 1, 0), ucol)),
            pl.BlockSpec((1, tm, CB), lambda bi, i: (bi, i, ucol)),
            pl.BlockSpec((1, POOL_HALO, CB), lambda bi, i: (bi, jnp.minimum((i + 1) * hb, nhb - 1), ucol)),
            br, gspec(0), gspec(1), gspec(2),
            pl.BlockSpec((None,) + pw.shape[1:], lambda bi, i: (layer, 0, 0, 0)),
            pl.BlockSpec((1, POOL_WIDTH), lambda bi, i: (0, 0)),
            pl.BlockSpec((None,) + wb.shape[1:], lambda bi, i: (layer, 0, 0, 0)),
            pl.BlockSpec((None,) + wo.shape[1:], lambda bi, i: (layer, 0, 0)),
            pl.BlockSpec((1, tm, d), lambda bi, i: (bi, i, 0)),
            pl.BlockSpec((1, 1, d), mod_map),
            pl.BlockSpec((1, d), lambda bi, i: (0, 0)),
        ],
        out_specs=pl.BlockSpec((1, tm, d), lambda bi, i: (bi, i, 0)),
        out_shape=jax.ShapeDtypeStruct((b, l, d), F32),
        compiler_params=_cparams(("parallel", "parallel")),
        name="merge",
    )(o_na, z, z, z, o_mla, gz, gz, gz, pw, psc, wb, wo, x, gate, gain)


def _gelu_tanh(x):
    return 0.5 * x * (1.0 + jnp.tanh(np.float32(np.sqrt(2.0 / np.pi)) * (x + np.float32(0.044715) * (x * x * x))))


def _ffn_kernel(xp_ref, x_ref, xn_ref, gain_ref, sc_ref, sh_ref, wa_ref, wb_ref, cwa_ref, cwb_ref, cba_ref,
                cbb_ref, wd_ref, gate_ref, gpost_ref, o_ref, h_scr, acc_scr):
    i = pl.program_id(1)
    j = pl.program_id(2)
    ni = pl.num_programs(1)
    nj = pl.num_programs(2)
    tm = x_ref.shape[1]
    n_ext = tm + 2 * CONV_HALO

    @pl.when(j == 0)
    def _():
        def norm_mod(xv):
            return _rms(xv, gain_ref[...]) * (1.0 + sc_ref[0]) + sh_ref[0]

        hp = jnp.where(i > 0, norm_mod(xp_ref[0]), 0.0)
        hn = jnp.where(i < ni - 1, norm_mod(xn_ref[0]), 0.0)
        h_scr[0:CONV_HALO] = hp.astype(BF16)
        h_scr[CONV_HALO:CONV_HALO + tm] = norm_mod(x_ref[0]).astype(BF16)
        h_scr[CONV_HALO + tm:n_ext] = hn.astype(BF16)
        acc_scr[...] = jnp.zeros(acc_scr.shape, F32)

    h = h_scr[...]

    def conv_half(w_ref, cw_ref, cb_ref):
        u = jnp.dot(h, w_ref[...], preferred_element_type=F32)
        prev = pltpu.roll(u, 1, 0)[CONV_HALO:CONV_HALO + tm]
        nxt = pltpu.roll(u, n_ext - 1, 0)[CONV_HALO:CONV_HALO + tm]
        cur = u[CONV_HALO:CONV_HALO + tm]
        return cb_ref[...] + prev * cw_ref[0:1] + cur * cw_ref[1:2] + nxt * cw_ref[2:3]

    a = conv_half(wa_ref, cwa_ref, cba_ref)
    bgate = conv_half(wb_ref, cwb_ref, cbb_ref)
    act = (_gelu_tanh(a) * bgate).astype(BF16)
    acc_scr[...] += jnp.dot(act, wd_ref[...], preferred_element_type=F32)

    @pl.when(j == nj - 1)
    def _():
        o_ref[0] = x_ref[0] + gate_ref[0] * _rms(acc_scr[...], gpost_ref[...])


def _ffn(x, gain, sc, sh, w_up, conv_w, conv_b, w_down, gate, gpost, layer, tm, tn):
    b, l, d = x.shape
    tm = min(tm, l)
    nch = D_FF // tn
    hb = tm // CONV_HALO
    nhb = l // CONV_HALO
    per_batch = sc.shape[0] > 1
    mod_map = (lambda bi, i, j: (bi, 0, 0)) if per_batch else (lambda bi, i, j: (0, 0, 0))
    mod = pl.BlockSpec((1, 1, d), mod_map)
    vec = pl.BlockSpec((1, d), lambda bi, i, j: (0, 0))
    return pl.pallas_call(
        _ffn_kernel,
        grid=(b, l // tm, nch),
        in_specs=[
            pl.BlockSpec((1, CONV_HALO, d), lambda bi, i, j: (bi, jnp.maximum(i * hb - 1, 0), 0)),
            pl.BlockSpec((1, tm, d), lambda bi, i, j: (bi, i, 0)),
            pl.BlockSpec((1, CONV_HALO, d), lambda bi, i, j: (bi, jnp.minimum((i + 1) * hb, nhb - 1), 0)),
            vec, mod, mod,
            pl.BlockSpec((None, d, tn), lambda bi, i, j: (layer, 0, j)),
            pl.BlockSpec((None, d, tn), lambda bi, i, j: (layer, 0, nch + j)),
            pl.BlockSpec((None, 3, tn), lambda bi, i, j: (layer, 0, j)),
            pl.BlockSpec((None, 3, tn), lambda bi, i, j: (layer, 0, nch + j)),
            pl.BlockSpec((None, 1, tn), lambda bi, i, j: (layer, 0, j)),
            pl.BlockSpec((None, 1, tn), lambda bi, i, j: (layer, 0, nch + j)),
            pl.BlockSpec((None, tn, d), lambda bi, i, j: (layer, j, 0)),
            mod, vec,
        ],
        out_specs=pl.BlockSpec((1, tm, d), lambda bi, i, j: (bi, i, 0)),
        out_shape=jax.ShapeDtypeStruct((b, l, d), F32),
        scratch_shapes=[pltpu.VMEM((tm + 2 * CONV_HALO, d), BF16), pltpu.VMEM((tm, d), F32)],
        compiler_params=_cparams(("parallel", "parallel", "arbitrary")),
        name="ffn",
    )(x, x, x, gain, sc, sh, w_up, w_up, conv_w, conv_w, conv_b, conv_b, w_down, gate, gpost)


_ROPE_PERM = np.concatenate([np.arange(8, 16), np.arange(0, 8), np.arange(24, 32), np.arange(16, 24)])


def _rope_tables(n_tok, rotate):
    f32 = np.float32
    ct = np.ones((n_tok, HEAD_PAD), f32)
    st = np.zeros((n_tok, HEAD_PAD), f32)
    if rotate:
        n_freq = MLA_ROPE // 4
        inv = (f32(ROPE_BASE) ** (-np.arange(n_freq, dtype=f32) / f32(n_freq))).astype(f32)
        t = np.arange(n_tok, dtype=np.int32)
        ang_r = ((t // GRID_W).astype(f32)[:, None] * inv[None, :]).astype(f32)
        ang_c = ((t % GRID_W).astype(f32)[:, None] * inv[None, :]).astype(f32)
        cr, sr, cc, sn = np.cos(ang_r), np.sin(ang_r), np.cos(ang_c), np.sin(ang_c)
        ct[:, MLA_NOPE:MLA_NOPE + MLA_ROPE] = np.concatenate([cr, cr, cc, cc], axis=1)
        st[:, MLA_NOPE:MLA_NOPE + MLA_ROPE] = np.concatenate([-sr, sr, -sn, sn], axis=1)
    return jnp.asarray(ct), jnp.asarray(st)


def _prep_all(w_in, w_uq, w_ukv):
    depth, d, _ = w_in.shape
    o = 0
    parts = []
    for s in (NA_WIDTH, NA_WIDTH, NA_WIDTH, POOL_WIDTH, MLA_Q_RANK, MLA_KV_RANK, MLA_ROPE, N_BRANCH * D_MODEL):
        parts.append(w_in[..., o:o + s])
        o += s
    wq, wk, wv, wu, wcq, wckv, wkr, wg = parts
    z64 = jnp.zeros((depth, d, MLA_NOPE), F32)
    z32 = jnp.zeros((depth, d, HEAD_PAD - MLA_NOPE - MLA_ROPE), F32)
    w_z = jnp.concatenate([wq * (NA_HEAD_DIM ** -0.5), wk, wv, wu, wcq, wckv,
                           z64, wkr, z32, z64, wkr[..., _ROPE_PERM], z32], axis=-1)

    uq = w_uq.reshape(depth, MLA_Q_RANK, MLA_HEADS, MLA_NOPE + MLA_ROPE)
    qz64 = jnp.zeros((depth, MLA_Q_RANK, MLA_HEADS, MLA_NOPE), F32)
    qz32 = jnp.zeros((depth, MLA_Q_RANK, MLA_HEADS, HEAD_PAD - MLA_NOPE - MLA_ROPE), F32)
    wq_pad = jnp.concatenate([uq, qz32], axis=-1)
    wq_perm = jnp.concatenate([qz64, uq[..., MLA_NOPE:][..., _ROPE_PERM], qz32], axis=-1)
    assert MLA_NOPE + MLA_V == HEAD_PAD

    flat = lambda w, k: w.reshape(depth, k, MLA_PAD_W).astype(BF16)
    return dict(
        w_z=w_z.astype(BF16),
        w_g=wg.astype(BF16),
        wq_pad=flat(wq_pad, MLA_Q_RANK),
        wq_perm=flat(wq_perm, MLA_Q_RANK),
        w_kv=w_ukv.astype(BF16),
    )


TM_PROJ = 1024
TN_PROJ = 1024
TM_UP = 512
TQ_FLASH = 1024
TK_FLASH = 2048
TM_MERGE = 1024
TM_FFN = 1024
TN_FFN = 1408


def kernel(x, c, ctx, c_ctx, w_ada, b_ada, norm_pre1, norm_post1, norm_pre2, norm_post2, w_in, na_rpb, pool_w,
           pool_scale, mla_q_norm, w_uq, mla_kv_norm, w_ukv, w_branch, w_o, w_up, conv_w, conv_b, w_down):
    b, s, d = x.shape
    lc = ctx.shape[1]
    depth = w_ada.shape[0]
    rows = s // GRID_W
    assert d == D_MODEL and s % NA_KTOK == 0 and lc % POOL_HALO == 0

    n_mod = -(-(b + 1) // 8) * 8
    cvec = jnp.concatenate([c, c_ctx[None, :], jnp.zeros((n_mod - b - 1, d), F32)], axis=0)
    mod = _ada(cvec, w_ada, b_ada).reshape(depth, n_mod, 6, d)

    ct_l, st_l = _rope_tables(s, True)
    ct_c, st_c = _rope_tables(b * lc, False)

    p = _prep_all(w_in, w_uq, w_ukv)
    na_bias = _bias_tables(na_rpb, rows)
    pw, wb, wo = pool_w.astype(BF16), w_branch.astype(BF16), w_o.astype(BF16)
    wup, wdn = w_up.astype(BF16), w_down.astype(BF16)
    cb = conv_b[:, None, :]

    xc = ctx
    for l in range(depth):
        last = l == depth - 1
        lat = [mod[l, :b, k][:, None, :] for k in range(6)]
        cx = [mod[l, b:b + 1, k][:, None, :] for k in range(6)]
        row = lambda v: v.reshape(1, -1)
        g_pre1, g_post1, g_pre2, g_post2 = row(norm_pre1[l]), row(norm_post1[l]), row(norm_pre2[l]), row(norm_post2[l])
        qg, kvg = row(mla_q_norm[l]), row(mla_kv_norm[l])
        psc = row(pool_scale[l])

        z, gz = _in_proj(x, g_pre1, lat[1], lat[0], p["w_z"], p["w_g"], l, TM_PROJ, TN_PROJ)
        flat = lambda a: a.reshape(1, b * lc, a.shape[-1])
        unflat = lambda a: a.reshape(b, lc, a.shape[-1])
        zc_f, gzc_f = _in_proj(flat(xc), g_pre1, cx[1], cx[0], p["w_z"], p["w_g"], l, TM_PROJ, TN_PROJ)
        zc, gzc = unflat(zc_f), unflat(gzc_f)
        mla_w = (p["wq_pad"], p["wq_perm"], p["w_kv"], l)
        q_l, k_l, v_l = _mla_up(z, qg, kvg, ct_l, st_l, *mla_w, TM_UP)
        q_c, k_c, v_c = [unflat(a) for a in _mla_up(zc_f, qg, kvg, ct_c, st_c, *mla_w, TM_UP)]

        o_na = _na(z, zc, na_bias, l)
        o_mla = _flash(q_l, k_c, v_c, k_l, v_l, TQ_FLASH, TK_FLASH)
        x = _merge(o_na, z, o_mla, gz, pw, psc, wb, wo, x, lat[2], g_post1, l, TM_MERGE)
        x = _ffn(x, g_pre2, lat[4], lat[3], wup, conv_w, cb, wdn, lat[5], g_post2, l, TM_FFN, TN_FFN)

        if not last:
            oc_na = _ctx_na(zc)
            oc_mla = _flash(q_c, k_c, v_c, None, None, TQ_FLASH, TK_FLASH)
            xc = _merge(oc_na, zc, oc_mla, gzc, pw, psc, wb, wo, xc, cx[2], g_post1, l, TM_MERGE)
            xc = _ffn(xc, g_pre2, cx[4], cx[3], wup, conv_w, cb, wdn, cx[5], g_post2, l, TM_FFN, TN_FFN)
    return x
```

```python
import functools

import jax
import jax.numpy as jnp
import numpy as np
from jax import lax
from jax.experimental import pallas as pl
from jax.experimental.pallas import tpu as pltpu

F32 = jnp.float32
BF16 = jnp.bfloat16

D_MODEL = 1024
GRID_W = 64
EPS = 1e-6
NA_HEADS = 8
NA_HEAD_DIM = 64
NA_WIDTH = NA_HEADS * NA_HEAD_DIM
NA_WIN_R = 8
NA_WIN_C = 16
POOL_WINDOWS = (2, 4, 8, 16)
POOL_GROUP = 128
POOL_WIDTH = POOL_GROUP * len(POOL_WINDOWS)
POOL_HALO = 16
MLA_HEADS = 8
MLA_NOPE = 64
MLA_ROPE = 32
MLA_V = 64
MLA_Q_RANK = 512
MLA_KV_RANK = 256
ROPE_BASE = 10000.0
N_BRANCH = 3
BRANCH_W = 512
D_FF = 2816
CONV_HALO = 8

LANES = 128
HEAD_PAD = LANES
MLA_PAD_W = MLA_HEADS * HEAD_PAD
MLA_SUM_LANE = 0

Z_Q = 0
Z_K = Z_Q + NA_WIDTH
Z_V = Z_K + NA_WIDTH
Z_U = Z_V + NA_WIDTH
Z_CQ = Z_U + POOL_WIDTH
Z_KV = Z_CQ + MLA_Q_RANK
Z_COLS = Z_KV + 512
G_COLS = N_BRANCH * D_MODEL
assert G_COLS == Z_COLS
CB = 512

NEG = -1e30
VMEM_LIMIT = 56 * 1024 * 1024


def _cparams(sem):
    return pltpu.CompilerParams(dimension_semantics=sem, vmem_limit_bytes=VMEM_LIMIT)


def _rms(x, gain):
    return x * lax.rsqrt(jnp.mean(x * x, axis=-1, keepdims=True) + EPS) * gain


def _ada_kernel(c_ref, w_ref, b_ref, o_ref):
    c = c_ref[...]
    s = c * jax.nn.sigmoid(c)
    o_ref[0] = jnp.dot(s.astype(BF16), w_ref[0].astype(BF16), preferred_element_type=F32) + b_ref[0]


def _ada(cvec, w_ada, b_ada):
    depth, d, n = w_ada.shape
    rows = cvec.shape[0]
    tn = 1536
    return pl.pallas_call(
        _ada_kernel,
        grid=(depth, n // tn),
        in_specs=[
            pl.BlockSpec((rows, d), lambda l, j: (0, 0)),
            pl.BlockSpec((1, d, tn), lambda l, j: (l, 0, j)),
            pl.BlockSpec((1, 1, tn), lambda l, j: (l, 0, j)),
        ],
        out_specs=pl.BlockSpec((1, rows, tn), lambda l, j: (l, 0, j)),
        out_shape=jax.ShapeDtypeStruct((depth, rows, n), F32),
        compiler_params=_cparams(("arbitrary", "arbitrary")),
        name="ada",
    )(cvec, w_ada, b_ada.reshape(depth, 1, n))


def _in_proj_kernel(x_ref, gain_ref, sc_ref, sh_ref, wz_ref, wg_ref, z_ref, g_ref, h_scr):
    j = pl.program_id(2)

    @pl.when(j == 0)
    def _():
        h = _rms(x_ref[0], gain_ref[...]) * (1.0 + sc_ref[0]) + sh_ref[0]
        h_scr[...] = h.astype(BF16)

    h = h_scr[...]
    g_ref[0] = jax.nn.sigmoid(jnp.dot(h, wg_ref[...], preferred_element_type=F32)).astype(BF16)
    z_ref[0] = jnp.dot(h, wz_ref[...], preferred_element_type=F32).astype(BF16)


def _in_proj(x, gain, sc, sh, wz, wg, layer, tm, tn):
    b, l, d = x.shape
    tm = min(tm, l)
    per_batch = sc.shape[0] > 1
    mod_map = (lambda bi, i, j: (bi, 0, 0)) if per_batch else (lambda bi, i, j: (0, 0, 0))
    wspec = pl.BlockSpec((None, d, tn), lambda bi, i, j: (layer, 0, j))
    ospec = pl.BlockSpec((1, tm, tn), lambda bi, i, j: (bi, i, j))
    oshape = jax.ShapeDtypeStruct((b, l, Z_COLS), BF16)
    return pl.pallas_call(
        _in_proj_kernel,
        grid=(b, l // tm, Z_COLS // tn),
        in_specs=[
            pl.BlockSpec((1, tm, d), lambda bi, i, j: (bi, i, 0)),
            pl.BlockSpec((1, d), lambda bi, i, j: (0, 0)),
            pl.BlockSpec((1, 1, d), mod_map),
            pl.BlockSpec((1, 1, d), mod_map),
            wspec, wspec,
        ],
        out_specs=[ospec, ospec],
        out_shape=[oshape, oshape],
        scratch_shapes=[pltpu.VMEM((tm, d), BF16)],
        compiler_params=_cparams(("parallel", "parallel", "arbitrary")),
        name="in_proj",
    )(x, gain, sc, sh, wz, wg)


def _mla_up_kernel(cq_ref, kv_ref, qg_ref, kvg_ref, ct_ref, st_ref, wq_ref, wqp_ref, wkv_ref,
                   q_out, k_out, v_out, *, scale):
    cqn = _rms(cq_ref[0].astype(F32), qg_ref[...]).astype(BF16)
    kvb = kv_ref[0].astype(F32)
    kvn = _rms(kvb[:, :MLA_KV_RANK], kvg_ref[...]).astype(BF16)
    ct = ct_ref[...]
    st = st_ref[...]
    kr = kvb[:, MLA_KV_RANK:MLA_KV_RANK + LANES] * ct + kvb[:, MLA_KV_RANK + LANES:] * st
    qm = jnp.dot(cqn, wq_ref[...], preferred_element_type=F32)
    qp = jnp.dot(cqn, wqp_ref[...], preferred_element_type=F32)
    kv = jnp.dot(kvn, wkv_ref[...], preferred_element_type=F32)
    cts = ct * scale
    sts = st * scale
    lane = lax.broadcasted_iota(jnp.int32, kr.shape, 1)
    is_k = lane < MLA_NOPE
    row_sum_col = jnp.where(lane == MLA_SUM_LANE, 1.0, 0.0)
    for h in range(MLA_HEADS):
        hs = slice(h * HEAD_PAD, (h + 1) * HEAD_PAD)
        q_out[0, :, hs] = (qm[:, hs] * cts + qp[:, hs] * sts).astype(BF16)
        k_out[0, :, hs] = jnp.where(is_k, kv[:, hs], kr).astype(BF16)
        v_out[0, :, hs] = jnp.where(is_k, row_sum_col, kv[:, hs]).astype(BF16)


def _mla_up(z, qg, kvg, ct, st, wq, wqp, wkv, layer, tm):
    b, l, _ = z.shape
    tm = min(tm, l)
    full = lambda shape: pl.BlockSpec(shape, lambda bi, i: (0,) * len(shape))
    wspec = lambda k: pl.BlockSpec((None, k, MLA_PAD_W), lambda bi, i: (layer, 0, 0))
    out = jax.ShapeDtypeStruct((b, l, MLA_PAD_W), BF16)
    ospec = pl.BlockSpec((1, tm, MLA_PAD_W), lambda bi, i: (bi, i, 0))
    return pl.pallas_call(
        functools.partial(_mla_up_kernel, scale=float((MLA_NOPE + MLA_ROPE) ** -0.5 * np.log2(np.e))),
        grid=(b, l // tm),
        in_specs=[
            pl.BlockSpec((1, tm, CB), lambda bi, i: (bi, i, Z_CQ // CB)),
            pl.BlockSpec((1, tm, CB), lambda bi, i: (bi, i, Z_KV // CB)),
            full((1, MLA_Q_RANK)),
            full((1, MLA_KV_RANK)),
            pl.BlockSpec((tm, LANES), lambda bi, i: (i, 0)),
            pl.BlockSpec((tm, LANES), lambda bi, i: (i, 0)),
            wspec(MLA_Q_RANK), wspec(MLA_Q_RANK), wspec(MLA_KV_RANK),
        ],
        out_specs=[ospec, ospec, ospec],
        out_shape=[out, out, out],
        compiler_params=_cparams(("parallel", "parallel")),
        name="mla_up",
    )(z, z, qg, kvg, ct, st, wq, wqp, wkv)


def _flash_kernel(*refs, has_latent):
    if has_latent:
        q_ref, kc_ref, vc_ref, k_ref, v_ref, o_ref, m_scr, acc_scr = refs
    else:
        q_ref, kc_ref, vc_ref, o_ref, m_scr, acc_scr = refs
        k_ref = v_ref = None
    kk = pl.program_id(2)
    nk = pl.num_programs(2)

    def attend(h, kblk, vblk):
        hs = slice(h * HEAD_PAD, (h + 1) * HEAD_PAD)
        q = q_ref[0, :, hs]
        s = lax.dot_general(q, kblk[0, :, hs], (((1,), (1,)), ((), ())), preferred_element_type=F32)
        m_prev = m_scr[h]
        m_new = jnp.maximum(m_prev, jnp.max(s, axis=-1, keepdims=True))
        alpha = jnp.exp2(m_prev - m_new)
        p = jnp.exp2(s - jnp.concatenate([m_new] * (s.shape[1] // LANES), axis=1))
        acc_scr[h] = alpha * acc_scr[h] + jnp.dot(p.astype(BF16), vblk[0, :, hs], preferred_element_type=F32)
        m_scr[h] = m_new

    @pl.when(kk == 0)
    def _():
        m_scr[...] = jnp.full(m_scr.shape, NEG, F32)
        acc_scr[...] = jnp.zeros(acc_scr.shape, F32)
        for h in range(MLA_HEADS):
            attend(h, kc_ref, vc_ref)

    if has_latent:
        for h in range(MLA_HEADS):
            attend(h, k_ref, v_ref)

    @pl.when(kk == nk - 1)
    def _():
        for h in range(MLA_HEADS):
            acc = acc_scr[h]
            o_ref[0, :, h * MLA_V:(h + 1) * MLA_V] = (
                acc[:, HEAD_PAD - MLA_V:] / acc[:, MLA_SUM_LANE:MLA_SUM_LANE + 1]).astype(BF16)


def _flash(q, kc, vc, k, v, tq, tk):
    b, lq, _ = q.shape
    lc = kc.shape[1]
    tq = min(tq, lq)
    has_latent = k is not None
    in_specs = [
        pl.BlockSpec((1, tq, MLA_PAD_W), lambda bi, i, kk: (bi, i, 0)),
        pl.BlockSpec((1, lc, MLA_PAD_W), lambda bi, i, kk: (bi, 0, 0)),
        pl.BlockSpec((1, lc, MLA_PAD_W), lambda bi, i, kk: (bi, 0, 0)),
    ]
    args = [q, kc, vc]
    nk = 1
    if has_latent:
        tk = min(tk, k.shape[1])
        nk = k.shape[1] // tk
        in_specs += [pl.BlockSpec((1, tk, MLA_PAD_W), lambda bi, i, kk: (bi, kk, 0))] * 2
        args += [k, v]
    return pl.pallas_call(
        functools.partial(_flash_kernel, has_latent=has_latent),
        grid=(b, lq // tq, nk),
        in_specs=in_specs,
        out_specs=pl.BlockSpec((1, tq, MLA_HEADS * MLA_V), lambda bi, i, kk: (bi, i, 0)),
        out_shape=jax.ShapeDtypeStruct((b, lq, MLA_HEADS * MLA_V), BF16),
        scratch_shapes=[pltpu.VMEM((MLA_HEADS, tq, LANES), F32), pltpu.VMEM((MLA_HEADS, tq, LANES), F32)],
        compiler_params=_cparams(("parallel", "parallel", "arbitrary")),
        name="mla_flash" if has_latent else "mla_ctx",
    )(*args)


NA_GROUP = 8
NA_GTOK = NA_GROUP * GRID_W


def _pair_softmax_pv(qp, k_list, v_list, bias_fn):
    lane = lax.broadcasted_iota(jnp.int32, qp.shape, 1)
    outs = []
    for sub in range(2):
        in_head = (lane >= sub * NA_HEAD_DIM) & (lane < (sub + 1) * NA_HEAD_DIM)
        qm = jnp.where(in_head, qp, jnp.zeros_like(qp))
        s_list = []
        for idx, kb in enumerate(k_list):
            s = lax.dot_general(qm, kb, (((1,), (1,)), ((), ())), preferred_element_type=F32)
            s_list.append(bias_fn(sub, idx, s))
        m = s_list[0].max(axis=-1, keepdims=True)
        for s in s_list[1:]:
            m = jnp.maximum(m, s.max(axis=-1, keepdims=True))
        o = None
        l = None
        for s, vb in zip(s_list, v_list):
            p = jnp.exp(s - m)
            ls = p.sum(axis=-1, keepdims=True)
            os_ = jnp.dot(p.astype(BF16), vb, preferred_element_type=F32)
            o = os_ if o is None else o + os_
            l = ls if l is None else l + ls
        outs.append(o / l)
    return jnp.where(lane < NA_HEAD_DIM, outs[0], outs[1])


NA_KROWS = 2 * NA_GROUP
NA_KTOK = NA_KROWS * GRID_W


N_REL_R = 2 * NA_WIN_R - 1
N_REL_C = 2 * NA_WIN_C - 1


def _na_group_geometry(variant, rows):
    ng = rows // NA_GROUP
    g = {0: 0, 1: 1, 2: ng - 1}[variant]
    strip0 = int(np.clip(g * NA_GROUP - NA_WIN_R // 2, 0, rows - NA_KROWS))
    r0 = [int(np.clip(g * NA_GROUP + i - NA_WIN_R // 2, 0, rows - NA_WIN_R)) for i in range(NA_GROUP)]
    return strip0 - g * NA_GROUP, [r - g * NA_GROUP for r in r0]


def _bias_table_kernel(rpb_ref, o_ref, *, rows):
    layer, h = pl.program_id(0), pl.program_id(1)
    shape = (GRID_W, LANES)
    qc = lax.broadcasted_iota(jnp.int32, shape, 0)
    lane = lax.broadcasted_iota(jnp.int32, shape, 1)
    kc = lane % GRID_W
    win0 = jnp.clip(qc - NA_WIN_C // 2, 0, GRID_W - NA_WIN_C)
    in_win = (kc >= win0) & (kc < win0 + NA_WIN_C)
    rel = jnp.clip(kc - qc + NA_WIN_C - 1, 0, N_REL_C - 1)
    left = lane < GRID_W
    base = (layer * NA_HEADS + h) * (N_REL_R * N_REL_C)
    neg = jnp.full(shape, NEG, F32)
    vals = []
    for a in range(N_REL_R):
        acc = jnp.zeros(shape, F32)
        for b in range(N_REL_C):
            acc = jnp.where(rel == b, rpb_ref[base + a * N_REL_C + b], acc)
        vals.append(jnp.where(in_win, acc, neg))

    for variant in range(3):
        strip_rel, r0_rel = _na_group_geometry(variant, rows)

        def half(i, j):
            key_rel = strip_rel + j
            if r0_rel[i] <= key_rel < r0_rel[i] + NA_WIN_R:
                return vals[key_rel - i + NA_WIN_R - 1]
            return neg

        for i in range(NA_GROUP):
            for jp in range(NA_KROWS // 2):
                lft, rgt = half(i, 2 * jp), half(i, 2 * jp + 1)
                blk = lft if lft is rgt else jnp.where(left, lft, rgt)
                o_ref[0, variant, 0, i * GRID_W:(i + 1) * GRID_W, jp * LANES:(jp + 1) * LANES] = blk


def _bias_tables(na_rpb, rows):
    depth = na_rpb.shape[0]
    return pl.pallas_call(
        functools.partial(_bias_table_kernel, rows=rows),
        grid=(depth, NA_HEADS),
        in_specs=[pl.BlockSpec(memory_space=pltpu.SMEM)],
        out_specs=pl.BlockSpec((1, 3, 1, NA_GTOK, NA_KTOK), lambda l, h: (l, 0, h, 0, 0)),
        out_shape=jax.ShapeDtypeStruct((depth, 3, NA_HEADS, NA_GTOK, NA_KTOK), F32),
        compiler_params=_cparams(("arbitrary", "arbitrary")),
        name="na_bias",
    )(na_rpb.reshape(-1))


def _na_windows(variant, rows):
    strip_rel, r0_rel = _na_group_geometry(variant, rows)
    out = []
    for rp in range(NA_GROUP // 2):
        w0 = [r0_rel[i] - strip_rel for i in (2 * rp, 2 * rp + 1)]
        lo, hi = min(w0) * GRID_W, (max(w0) + NA_WIN_R) * GRID_W
        out.append((lo // LANES * LANES, -(-hi // LANES) * LANES))
    return out


def _na_pair(qp, k_w, v_w, k_x, v_x, bias_ref, pr, windows):
    lane = lax.broadcasted_iota(jnp.int32, qp.shape, 1)
    pair_rows = 2 * GRID_W
    outs = []
    for sub in range(2):
        in_head = (lane >= sub * NA_HEAD_DIM) & (lane < (sub + 1) * NA_HEAD_DIM)
        qm = jnp.where(in_head, qp, jnp.zeros_like(qp))
        s_w = lax.dot_general(qm, k_w, (((1,), (1,)), ((), ())), preferred_element_type=F32)
        s_c = lax.dot_general(qm, k_x, (((1,), (1,)), ((), ())), preferred_element_type=F32)
        p_rows, pc_rows, l_rows = [], [], []
        for rp, (c0, c1) in enumerate(windows):
            rs = slice(rp * pair_rows, (rp + 1) * pair_rows)
            sw = s_w[rs, c0:c1] + bias_ref[0, 0, 2 * pr + sub, rs, c0:c1]
            sc = s_c[rs]
            m = jnp.maximum(sw.max(axis=-1, keepdims=True), sc.max(axis=-1, keepdims=True))
            pw = jnp.exp(sw - m)
            pc = jnp.exp(sc - m)
            l_rows.append(pw.sum(axis=-1, keepdims=True) + pc.sum(axis=-1, keepdims=True))
            parts = []
            if c0 > 0:
                parts.append(jnp.zeros((pair_rows, c0), BF16))
            parts.append(pw.astype(BF16))
            if c1 < NA_KTOK:
                parts.append(jnp.zeros((pair_rows, NA_KTOK - c1), BF16))
            p_rows.append(jnp.concatenate(parts, axis=1) if len(parts) > 1 else parts[0])
            pc_rows.append(pc.astype(BF16))
        o = (jnp.dot(jnp.concatenate(p_rows, axis=0), v_w, preferred_element_type=F32)
             + jnp.dot(jnp.concatenate(pc_rows, axis=0), v_x, preferred_element_type=F32))
        outs.append(o / jnp.concatenate(l_rows, axis=0))
    return jnp.where(lane < NA_HEAD_DIM, outs[0], outs[1])


def _na_kernel(q_ref, kp_ref, kc_ref, kn_ref, vp_ref, vc_ref, vn_ref, kx_ref, vx_ref, bias_ref, o_ref,
               kcat, vcat, *, rows):
    g = pl.program_id(0)
    ng = pl.num_programs(0)
    kcat[0:NA_GTOK] = kp_ref[0]
    kcat[NA_GTOK:2 * NA_GTOK] = kc_ref[0]
    kcat[2 * NA_GTOK:3 * NA_GTOK] = kn_ref[0]
    vcat[0:NA_GTOK] = vp_ref[0]
    vcat[NA_GTOK:2 * NA_GTOK] = vc_ref[0]
    vcat[2 * NA_GTOK:3 * NA_GTOK] = vn_ref[0]
    strip0 = jnp.clip(g * NA_GROUP - NA_WIN_R // 2, 0, rows - NA_KROWS)
    start = pl.multiple_of((strip0 - g * NA_GROUP + NA_GROUP) * GRID_W, GRID_W)
    variant = jnp.where(g == 0, 0, jnp.where(g == ng - 1, 2, 1))
    for vi in range(3):
        @pl.when(variant == vi)
        def _(vi=vi):
            windows = _na_windows(vi, rows)
            for pr in range(NA_HEADS // 2):
                ps = slice(pr * LANES, (pr + 1) * LANES)
                k_w = kcat[pl.ds(start, NA_KTOK), ps]
                v_w = vcat[pl.ds(start, NA_KTOK), ps]
                o = _na_pair(q_ref[0, :, ps], k_w, v_w, kx_ref[0, :, ps], vx_ref[0, :, ps], bias_ref, pr, windows)
                o_ref[0, :, ps] = o.astype(BF16)


def _na(z, zc, bias, layer):
    b, s, _ = z.shape
    lc = zc.shape[1]
    rows = s // GRID_W
    ng = rows // NA_GROUP
    blk = (1, NA_GTOK, CB)

    def spec(col, off):
        return pl.BlockSpec(blk, lambda g, bi: (bi, jnp.clip(g + off, 0, ng - 1), col // CB))

    def bias_map(g, bi):
        return (layer, jnp.where(g == 0, 0, jnp.where(g == ng - 1, 2, 1)), 0, 0, 0)

    return pl.pallas_call(
        functools.partial(_na_kernel, rows=rows),
        grid=(ng, b),
        in_specs=[
            spec(Z_Q, 0),
            spec(Z_K, -1), spec(Z_K, 0), spec(Z_K, 1),
            spec(Z_V, -1), spec(Z_V, 0), spec(Z_V, 1),
            pl.BlockSpec((1, lc, CB), lambda g, bi: (bi, 0, Z_K // CB)),
            pl.BlockSpec((1, lc, CB), lambda g, bi: (bi, 0, Z_V // CB)),
            pl.BlockSpec((1, 1) + bias.shape[2:], bias_map, pipeline_mode=pl.Buffered(1)),
        ],
        out_specs=pl.BlockSpec(blk, lambda g, bi: (bi, g, 0)),
        out_shape=jax.ShapeDtypeStruct((b, s, NA_WIDTH), BF16),
        scratch_shapes=[pltpu.VMEM((3 * NA_GTOK, CB), BF16), pltpu.VMEM((3 * NA_GTOK, CB), BF16)],
        compiler_params=_cparams(("arbitrary", "arbitrary")),
        name="na_attn",
    )(z, z, z, z, z, z, z, zc, zc, bias)


def _ctx_na_kernel(q_ref, k_ref, v_ref, o_ref):
    for pr in range(NA_HEADS // 2):
        ps = slice(pr * LANES, (pr + 1) * LANES)
        o = _pair_softmax_pv(q_ref[0, :, ps], [k_ref[0, :, ps]], [v_ref[0, :, ps]], lambda sub, idx, s: s)
        o_ref[0, :, ps] = o.astype(BF16)


def _ctx_na(zc):
    b, lc, _ = zc.shape
    spec = lambda col: pl.BlockSpec((1, lc, CB), lambda bi: (bi, 0, col // CB))
    return pl.pallas_call(
        _ctx_na_kernel,
        grid=(b,),
        in_specs=[spec(Z_Q), spec(Z_K), spec(Z_V)],
        out_specs=pl.BlockSpec((1, lc, NA_WIDTH), lambda bi: (bi, 0, 0)),
        out_shape=jax.ShapeDtypeStruct((b, lc, NA_WIDTH), BF16),
        compiler_params=_cparams(("parallel",)),
        name="ctx_na",
    )(zc, zc, zc)


def _pool_tile(up_ref, u_ref, un_ref, w_ref, sc_ref, i, seq_len):
    tm = u_ref.shape[1]
    ext = jnp.concatenate([up_ref[0], u_ref[0], un_ref[0]], axis=0).astype(F32)
    n_ext = tm + 2 * POOL_HALO
    tg = i * tm - POOL_HALO + lax.broadcasted_iota(jnp.int32, (n_ext, 1), 0)
    ext = jnp.where((tg >= 0) & (tg < seq_len), ext, 0.0)
    t = i * tm + lax.broadcasted_iota(jnp.int32, (tm, 1), 0)

    def shifted(a, k):
        return pltpu.roll(a, (-k) % n_ext, 0)

    outs = []
    for gi, win in enumerate(POOL_WINDOWS):
        gs = slice(gi * POOL_GROUP, (gi + 1) * POOL_GROUP)
        a = ext[:, gs]
        wsum = a + shifted(a, -1)
        half = 1
        while 2 * half < win:
            wsum = shifted(wsum, -half) + shifted(wsum, half)
            half *= 2
        cnt = (jnp.minimum(t + win // 2, seq_len) - jnp.maximum(t - win // 2, 0)).astype(F32)
        d = wsum[POOL_HALO:POOL_HALO + tm] / cnt - a[POOL_HALO:POOL_HALO + tm]
        y = jnp.dot(d.astype(BF16), w_ref[gi], preferred_element_type=F32)
        outs.append((y * sc_ref[:, gs]).astype(BF16))
    return jnp.concatenate(outs, axis=1)


def _merge_kernel(ona_ref, up_ref, u_ref, un_ref, omla_ref, g0_ref, g1_ref, g2_ref, pw_ref, psc_ref, wb_ref, wo_ref,
                  x_ref, gate_ref, gain_ref, o_ref, *, seq_len):
    o_pool = _pool_tile(up_ref, u_ref, un_ref, pw_ref, psc_ref, pl.program_id(1), seq_len)
    m = None
    for br, g_ref, k in ((ona_ref[0], g0_ref, 0), (o_pool, g1_ref, 1), (omla_ref[0], g2_ref, 2)):
        proj = jnp.dot(br, wb_ref[k], preferred_element_type=F32)
        term = g_ref[0].astype(F32) * proj
        m = term if m is None else m + term
    y = jnp.dot(m.astype(BF16), wo_ref[...], preferred_element_type=F32)
    o_ref[0] = x_ref[0] + gate_ref[0] * _rms(y, gain_ref[...])


def _merge(o_na, z, o_mla, gz, pw, psc, wb, wo, x, gate, gain, layer, tm):
    b, l, d = x.shape
    tm = min(tm, l)
    hb = tm // POOL_HALO
    nhb = l // POOL_HALO
    ucol = Z_U // CB
    per_batch = gate.shape[0] > 1
    mod_map = (lambda bi, i: (bi, 0, 0)) if per_batch else (lambda bi, i: (0, 0, 0))
    br = pl.BlockSpec((1, tm, BRANCH_W), lambda bi, i: (bi, i, 0))
    gspec = lambda k: pl.BlockSpec((1, tm, d), lambda bi, i: (bi, i, k))
    return pl.pallas_call(
        functools.partial(_merge_kernel, seq_len=l),
        grid=(b, l // tm),
        in_specs=[
            br,
            pl.BlockSpec((1, POOL_HALO, CB), lambda bi, i: (bi, jnp.maximum(i * hb - 1, 0), ucol)),
            pl.BlockSpec((1, tm, CB), lambda bi, i: (bi, i, ucol)),
            pl.BlockSpec((1, POOL_HALO, CB), lambda bi, i: (bi, jnp.minimum((i + 1) * hb, nhb - 1), ucol)),
            br, gspec(0), gspec(1), gspec(2),
            pl.BlockSpec((None,) + pw.shape[1:], lambda bi, i: (layer, 0, 0, 0)),
            pl.BlockSpec((1, POOL_WIDTH), lambda bi, i: (0, 0)),
            pl.BlockSpec((None,) + wb.shape[1:], lambda bi, i: (layer, 0, 0, 0)),
            pl.BlockSpec((None,) + wo.shape[1:], lambda bi, i: (layer, 0, 0)),
            pl.BlockSpec((1, tm, d), lambda bi, i: (bi, i, 0)),
            pl.BlockSpec((1, 1, d), mod_map),
            pl.BlockSpec((1, d), lambda bi, i: (0, 0)),
        ],
        out_specs=pl.BlockSpec((1, tm, d), lambda bi, i: (bi, i, 0)),
        out_shape=jax.ShapeDtypeStruct((b, l, d), F32),
        compiler_params=_cparams(("parallel", "parallel")),
        name="merge",
    )(o_na, z, z, z, o_mla, gz, gz, gz, pw, psc, wb, wo, x, gate, gain)


def _gelu_tanh(x):
    return 0.5 * x * (1.0 + jnp.tanh(np.float32(np.sqrt(2.0 / np.pi)) * (x + np.float32(0.044715) * (x * x * x))))


def _ffn_kernel(xp_ref, x_ref, xn_ref, gain_ref, sc_ref, sh_ref, wa_ref, wb_ref, cwa_ref, cwb_ref, cba_ref,
                cbb_ref, wd_ref, gate_ref, gpost_ref, o_ref, h_scr, acc_scr):
    i = pl.program_id(1)
    j = pl.program_id(2)
    ni = pl.num_programs(1)
    nj = pl.num_programs(2)
    tm = x_ref.shape[1]
    n_ext = tm + 2 * CONV_HALO

    @pl.when(j == 0)
    def _():
        def norm_mod(xv):
            return _rms(xv, gain_ref[...]) * (1.0 + sc_ref[0]) + sh_ref[0]

        hp = jnp.where(i > 0, norm_mod(xp_ref[0]), 0.0)
        hn = jnp.where(i < ni - 1, norm_mod(xn_ref[0]), 0.0)
        h_scr[0:CONV_HALO] = hp.astype(BF16)
        h_scr[CONV_HALO:CONV_HALO + tm] = norm_mod(x_ref[0]).astype(BF16)
        h_scr[CONV_HALO + tm:n_ext] = hn.astype(BF16)
        acc_scr[...] = jnp.zeros(acc_scr.shape, F32)

    h = h_scr[...]

    def conv_half(w_ref, cw_ref, cb_ref):
        u = jnp.dot(h, w_ref[...], preferred_element_type=F32)
        prev = pltpu.roll(u, 1, 0)[CONV_HALO:CONV_HALO + tm]
        nxt = pltpu.roll(u, n_ext - 1, 0)[CONV_HALO:CONV_HALO + tm]
        cur = u[CONV_HALO:CONV_HALO + tm]
        return cb_ref[...] + prev * cw_ref[0:1] + cur * cw_ref[1:2] + nxt * cw_ref[2:3]

    a = conv_half(wa_ref, cwa_ref, cba_ref)
    bgate = conv_half(wb_ref, cwb_ref, cbb_ref)
    act = (_gelu_tanh(a) * bgate).astype(BF16)
    acc_scr[...] += jnp.dot(act, wd_ref[...], preferred_element_type=F32)

    @pl.when(j == nj - 1)
    def _():
        o_ref[0] = x_ref[0] + gate_ref[0] * _rms(acc_scr[...], gpost_ref[...])


def _ffn(x, gain, sc, sh, w_up, conv_w, conv_b, w_down, gate, gpost, layer, tm, tn):
    b, l, d = x.shape
    tm = min(tm, l)
    nch = D_FF // tn
    hb = tm // CONV_HALO
    nhb = l // CONV_HALO
    per_batch = sc.shape[0] > 1
    mod_map = (lambda bi, i, j: (bi, 0, 0)) if per_batch else (lambda bi, i, j: (0, 0, 0))
    mod = pl.BlockSpec((1, 1, d), mod_map)
    vec = pl.BlockSpec((1, d), lambda bi, i, j: (0, 0))
    return pl.pallas_call(
        _ffn_kernel,
        grid=(b, l // tm, nch),
        in_specs=[
            pl.BlockSpec((1, CONV_HALO, d), lambda bi, i, j: (bi, jnp.maximum(i * hb - 1, 0), 0)),
            pl.BlockSpec((1, tm, d), lambda bi, i, j: (bi, i, 0)),
            pl.BlockSpec((1, CONV_HALO, d), lambda bi, i, j: (bi, jnp.minimum((i + 1) * hb, nhb - 1), 0)),
            vec, mod, mod,
            pl.BlockSpec((None, d, tn), lambda bi, i, j: (layer, 0, j)),
            pl.BlockSpec((None, d, tn), lambda bi, i, j: (layer, 0, nch + j)),
            pl.BlockSpec((None, 3, tn), lambda bi, i, j: (layer, 0, j)),
            pl.BlockSpec((None, 3, tn), lambda bi, i, j: (layer, 0, nch + j)),
            pl.BlockSpec((None, 1, tn), lambda bi, i, j: (layer, 0, j)),
            pl.BlockSpec((None, 1, tn), lambda bi, i, j: (layer, 0, nch + j)),
            pl.BlockSpec((None, tn, d), lambda bi, i, j: (layer, j, 0)),
            mod, vec,
        ],
        out_specs=pl.BlockSpec((1, tm, d), lambda bi, i, j: (bi, i, 0)),
        out_shape=jax.ShapeDtypeStruct((b, l, d), F32),
        scratch_shapes=[pltpu.VMEM((tm + 2 * CONV_HALO, d), BF16), pltpu.VMEM((tm, d), F32)],
        compiler_params=_cparams(("parallel", "parallel", "arbitrary")),
        name="ffn",
    )(x, x, x, gain, sc, sh, w_up, w_up, conv_w, conv_w, conv_b, conv_b, w_down, gate, gpost)


_ROPE_PERM = np.concatenate([np.arange(8, 16), np.arange(0, 8), np.arange(24, 32), np.arange(16, 24)])


def _rope_tables(n_tok, rotate):
    f32 = np.float32
    ct = np.ones((n_tok, HEAD_PAD), f32)
    st = np.zeros((n_tok, HEAD_PAD), f32)
    if rotate:
        n_freq = MLA_ROPE // 4
        inv = (f32(ROPE_BASE) ** (-np.arange(n_freq, dtype=f32) / f32(n_freq))).astype(f32)
        t = np.arange(n_tok, dtype=np.int32)
        ang_r = ((t // GRID_W).astype(f32)[:, None] * inv[None, :]).astype(f32)
        ang_c = ((t % GRID_W).astype(f32)[:, None] * inv[None, :]).astype(f32)
        cr, sr, cc, sn = np.cos(ang_r), np.sin(ang_r), np.cos(ang_c), np.sin(ang_c)
        ct[:, MLA_NOPE:MLA_NOPE + MLA_ROPE] = np.concatenate([cr, cr, cc, cc], axis=1)
        st[:, MLA_NOPE:MLA_NOPE + MLA_ROPE] = np.concatenate([-sr, sr, -sn, sn], axis=1)
    return jnp.asarray(ct), jnp.asarray(st)


def _prep_all(w_in, w_uq, w_ukv):
    depth, d, _ = w_in.shape
    o = 0
    parts = []
    for s in (NA_WIDTH, NA_WIDTH, NA_WIDTH, POOL_WIDTH, MLA_Q_RANK, MLA_KV_RANK, MLA_ROPE, N_BRANCH * D_MODEL):
        parts.append(w_in[..., o:o + s])
        o += s
    wq, wk, wv, wu, wcq, wckv, wkr, wg = parts
    z64 = jnp.zeros((depth, d, MLA_NOPE), F32)
    z32 = jnp.zeros((depth, d, HEAD_PAD - MLA_NOPE - MLA_ROPE), F32)
    w_z = jnp.concatenate([wq * (NA_HEAD_DIM ** -0.5), wk, wv, wu, wcq, wckv,
                           z64, wkr, z32, z64, wkr[..., _ROPE_PERM], z32], axis=-1)

    uq = w_uq.reshape(depth, MLA_Q_RANK, MLA_HEADS, MLA_NOPE + MLA_ROPE)
    qz64 = jnp.zeros((depth, MLA_Q_RANK, MLA_HEADS, MLA_NOPE), F32)
    qz32 = jnp.zeros((depth, MLA_Q_RANK, MLA_HEADS, HEAD_PAD - MLA_NOPE - MLA_ROPE), F32)
    wq_pad = jnp.concatenate([uq, qz32], axis=-1)
    wq_perm = jnp.concatenate([qz64, uq[..., MLA_NOPE:][..., _ROPE_PERM], qz32], axis=-1)
    assert MLA_NOPE + MLA_V == HEAD_PAD

    flat = lambda w, k: w.reshape(depth, k, MLA_PAD_W).astype(BF16)
    return dict(
        w_z=w_z.astype(BF16),
        w_g=wg.astype(BF16),
        wq_pad=flat(wq_pad, MLA_Q_RANK),
        wq_perm=flat(wq_perm, MLA_Q_RANK),
        w_kv=w_ukv.astype(BF16),
    )


TM_PROJ = 1024
TN_PROJ = 1024
TM_UP = 1024
TQ_FLASH = 1024
TK_FLASH = 2048
TM_MERGE = 1024
TM_FFN = 1024
TN_FFN = 1408


def kernel(x, c, ctx, c_ctx, w_ada, b_ada, norm_pre1, norm_post1, norm_pre2, norm_post2, w_in, na_rpb, pool_w,
           pool_scale, mla_q_norm, w_uq, mla_kv_norm, w_ukv, w_branch, w_o, w_up, conv_w, conv_b, w_down):
    b, s, d = x.shape
    lc = ctx.shape[1]
    depth = w_ada.shape[0]
    rows = s // GRID_W
    assert d == D_MODEL and s % NA_KTOK == 0 and lc % POOL_HALO == 0

    n_mod = -(-(b + 1) // 8) * 8
    cvec = jnp.concatenate([c, c_ctx[None, :], jnp.zeros((n_mod - b - 1, d), F32)], axis=0)
    mod = _ada(cvec, w_ada, b_ada).reshape(depth, n_mod, 6, d)

    ct_l, st_l = _rope_tables(s, True)
    ct_c, st_c = _rope_tables(b * lc, False)

    p = _prep_all(w_in, w_uq, w_ukv)
    na_bias = _bias_tables(na_rpb, rows)
    pw, wb, wo = pool_w.astype(BF16), w_branch.astype(BF16), w_o.astype(BF16)
    wup, wdn = w_up.astype(BF16), w_down.astype(BF16)
    cb = conv_b[:, None, :]

    xc = ctx
    for l in range(depth):
        last = l == depth - 1
        lat = [mod[l, :b, k][:, None, :] for k in range(6)]
        cx = [mod[l, b:b + 1, k][:, None, :] for k in range(6)]
        row = lambda v: v.reshape(1, -1)
        g_pre1, g_post1, g_pre2, g_post2 = row(norm_pre1[l]), row(norm_post1[l]), row(norm_pre2[l]), row(norm_post2[l])
        qg, kvg = row(mla_q_norm[l]), row(mla_kv_norm[l])
        psc = row(pool_scale[l])

        z, gz = _in_proj(x, g_pre1, lat[1], lat[0], p["w_z"], p["w_g"], l, TM_PROJ, TN_PROJ)
        flat = lambda a: a.reshape(1, b * lc, a.shape[-1])
        unflat = lambda a: a.reshape(b, lc, a.shape[-1])
        zc_f, gzc_f = _in_proj(flat(xc), g_pre1, cx[1], cx[0], p["w_z"], p["w_g"], l, TM_PROJ, TN_PROJ)
        zc, gzc = unflat(zc_f), unflat(gzc_f)
        mla_w = (p["wq_pad"], p["wq_perm"], p["w_kv"], l)
        q_l, k_l, v_l = _mla_up(z, qg, kvg, ct_l, st_l, *mla_w, TM_UP)
        q_c, k_c, v_c = [unflat(a) for a in _mla_up(zc_f, qg, kvg, ct_c, st_c, *mla_w, TM_UP)]

        o_na = _na(z, zc, na_bias, l)
        o_mla = _flash(q_l, k_c, v_c, k_l, v_l, TQ_FLASH, TK_FLASH)
        x = _merge(o_na, z, o_mla, gz, pw, psc, wb, wo, x, lat[2], g_post1, l, TM_MERGE)
        x = _ffn(x, g_pre2, lat[4], lat[3], wup, conv_w, cb, wdn, lat[5], g_post2, l, TM_FFN, TN_FFN)

        if not last:
            oc_na = _ctx_na(zc)
            oc_mla = _flash(q_c, k_c, v_c, None, None, TQ_FLASH, TK_FLASH)
            xc = _merge(oc_na, zc, oc_mla, gzc, pw, psc, wb, wo, xc, cx[2], g_post1, l, TM_MERGE)
            xc = _ffn(xc, g_pre2, cx[4], cx[3], wup, conv_w, cb, wdn, cx[5], g_post2, l, TM_FFN, TN_FFN)
    return x
```

```python
import functools

import jax
import jax.numpy as jnp
import numpy as np
from jax import lax
from jax.experimental import pallas as pl
from jax.experimental.pallas import tpu as pltpu

F32 = jnp.float32
BF16 = jnp.bfloat16

D_MODEL = 1024
GRID_W = 64
EPS = 1e-6
NA_HEADS = 8
NA_HEAD_DIM = 64
NA_WIDTH = NA_HEADS * NA_HEAD_DIM
NA_WIN_R = 8
NA_WIN_C = 16
POOL_WINDOWS = (2, 4, 8, 16)
POOL_GROUP = 128
POOL_WIDTH = POOL_GROUP * len(POOL_WINDOWS)
POOL_HALO = 16
MLA_HEADS = 8
MLA_NOPE = 64
MLA_ROPE = 32
MLA_V = 64
MLA_Q_RANK = 512
MLA_KV_RANK = 256
ROPE_BASE = 10000.0
N_BRANCH = 3
BRANCH_W = 512
D_FF = 2816
CONV_HALO = 8

LANES = 128
HEAD_PAD = LANES
MLA_PAD_W = MLA_HEADS * HEAD_PAD
MLA_SUM_LANE = 0

Z_Q = 0
Z_K = Z_Q + NA_WIDTH
Z_V = Z_K + NA_WIDTH
Z_U = Z_V + NA_WIDTH
Z_CQ = Z_U + POOL_WIDTH
Z_KV = Z_CQ + MLA_Q_RANK
Z_COLS = Z_KV + 512
G_COLS = N_BRANCH * D_MODEL
assert G_COLS == Z_COLS
CB = 512

NEG = -1e30
VMEM_LIMIT = 56 * 1024 * 1024


def _cparams(sem):
    return pltpu.CompilerParams(dimension_semantics=sem, vmem_limit_bytes=VMEM_LIMIT)


def _rms(x, gain):
    return x * lax.rsqrt(jnp.mean(x * x, axis=-1, keepdims=True) + EPS) * gain


def _ada_kernel(c_ref, w_ref, b_ref, o_ref):
    c = c_ref[...]
    s = c * jax.nn.sigmoid(c)
    o_ref[0] = jnp.dot(s.astype(BF16), w_ref[0].astype(BF16), preferred_element_type=F32) + b_ref[0]


def _ada(cvec, w_ada, b_ada):
    depth, d, n = w_ada.shape
    rows = cvec.shape[0]
    tn = 1536
    return pl.pallas_call(
        _ada_kernel,
        grid=(depth, n // tn),
        in_specs=[
            pl.BlockSpec((rows, d), lambda l, j: (0, 0)),
            pl.BlockSpec((1, d, tn), lambda l, j: (l, 0, j)),
            pl.BlockSpec((1, 1, tn), lambda l, j: (l, 0, j)),
        ],
        out_specs=pl.BlockSpec((1, rows, tn), lambda l, j: (l, 0, j)),
        out_shape=jax.ShapeDtypeStruct((depth, rows, n), F32),
        compiler_params=_cparams(("arbitrary", "arbitrary")),
        name="ada",
    )(cvec, w_ada, b_ada.reshape(depth, 1, n))


def _in_proj_kernel(x_ref, gain_ref, sc_ref, sh_ref, wz_ref, wg_ref, z_ref, g_ref, h_scr):
    j = pl.program_id(2)

    @pl.when(j == 0)
    def _():
        h = _rms(x_ref[0], gain_ref[...]) * (1.0 + sc_ref[0]) + sh_ref[0]
        h_scr[...] = h.astype(BF16)

    h = h_scr[...]
    g_ref[0] = jax.nn.sigmoid(jnp.dot(h, wg_ref[...], preferred_element_type=F32)).astype(BF16)
    z_ref[0] = jnp.dot(h, wz_ref[...], preferred_element_type=F32).astype(BF16)


def _in_proj(x, gain, sc, sh, wz, wg, layer, tm, tn):
    b, l, d = x.shape
    tm = min(tm, l)
    per_batch = sc.shape[0] > 1
    mod_map = (lambda bi, i, j: (bi, 0, 0)) if per_batch else (lambda bi, i, j: (0, 0, 0))
    wspec = pl.BlockSpec((None, d, tn), lambda bi, i, j: (layer, 0, j))
    ospec = pl.BlockSpec((1, tm, tn), lambda bi, i, j: (bi, i, j))
    oshape = jax.ShapeDtypeStruct((b, l, Z_COLS), BF16)
    return pl.pallas_call(
        _in_proj_kernel,
        grid=(b, l // tm, Z_COLS // tn),
        in_specs=[
            pl.BlockSpec((1, tm, d), lambda bi, i, j: (bi, i, 0)),
            pl.BlockSpec((1, d), lambda bi, i, j: (0, 0)),
            pl.BlockSpec((1, 1, d), mod_map),
            pl.BlockSpec((1, 1, d), mod_map),
            wspec, wspec,
        ],
        out_specs=[ospec, ospec],
        out_shape=[oshape, oshape],
        scratch_shapes=[pltpu.VMEM((tm, d), BF16)],
        compiler_params=_cparams(("parallel", "parallel", "arbitrary")),
        name="in_proj",
    )(x, gain, sc, sh, wz, wg)


def _mla_up_kernel(cq_ref, kv_ref, qg_ref, kvg_ref, ct_ref, st_ref, wq_ref, wqp_ref, wkv_ref,
                   q_out, k_out, v_out, *, scale):
    cqn = _rms(cq_ref[0].astype(F32), qg_ref[...]).astype(BF16)
    kvb = kv_ref[0].astype(F32)
    kvn = _rms(kvb[:, :MLA_KV_RANK], kvg_ref[...]).astype(BF16)
    ct = ct_ref[...]
    st = st_ref[...]
    kr = kvb[:, MLA_KV_RANK:MLA_KV_RANK + LANES] * ct + kvb[:, MLA_KV_RANK + LANES:] * st
    qm = jnp.dot(cqn, wq_ref[...], preferred_element_type=F32)
    qp = jnp.dot(cqn, wqp_ref[...], preferred_element_type=F32)
    kv = jnp.dot(kvn, wkv_ref[...], preferred_element_type=F32)
    cts = ct * scale
    sts = st * scale
    lane = lax.broadcasted_iota(jnp.int32, kr.shape, 1)
    is_k = lane < MLA_NOPE
    row_sum_col = jnp.where(lane == MLA_SUM_LANE, 1.0, 0.0)
    for h in range(MLA_HEADS):
        hs = slice(h * HEAD_PAD, (h + 1) * HEAD_PAD)
        q_out[0, :, hs] = (qm[:, hs] * cts + qp[:, hs] * sts).astype(BF16)
        k_out[0, :, hs] = jnp.where(is_k, kv[:, hs], kr).astype(BF16)
        v_out[0, :, hs] = jnp.where(is_k, row_sum_col, kv[:, hs]).astype(BF16)


def _mla_up(z, qg, kvg, ct, st, wq, wqp, wkv, layer, tm):
    b, l, _ = z.shape
    tm = min(tm, l)
    full = lambda shape: pl.BlockSpec(shape, lambda bi, i: (0,) * len(shape))
    wspec = lambda k: pl.BlockSpec((None, k, MLA_PAD_W), lambda bi, i: (layer, 0, 0))
    out = jax.ShapeDtypeStruct((b, l, MLA_PAD_W), BF16)
    ospec = pl.BlockSpec((1, tm, MLA_PAD_W), lambda bi, i: (bi, i, 0))
    return pl.pallas_call(
        functools.partial(_mla_up_kernel, scale=float((MLA_NOPE + MLA_ROPE) ** -0.5 * np.log2(np.e))),
        grid=(b, l // tm),
        in_specs=[
            pl.BlockSpec((1, tm, CB), lambda bi, i: (bi, i, Z_CQ // CB)),
            pl.BlockSpec((1, tm, CB), lambda bi, i: (bi, i, Z_KV // CB)),
            full((1, MLA_Q_RANK)),
            full((1, MLA_KV_RANK)),
            pl.BlockSpec((tm, LANES), lambda bi, i: (i, 0)),
            pl.BlockSpec((tm, LANES), lambda bi, i: (i, 0)),
            wspec(MLA_Q_RANK), wspec(MLA_Q_RANK), wspec(MLA_KV_RANK),
        ],
        out_specs=[ospec, ospec, ospec],
        out_shape=[out, out, out],
        compiler_params=_cparams(("parallel", "parallel")),
        name="mla_up",
    )(z, z, qg, kvg, ct, st, wq, wqp, wkv)


def _flash_kernel(*refs, has_latent):
    if has_latent:
        q_ref, kc_ref, vc_ref, k_ref, v_ref, o_ref, m_scr, acc_scr = refs
    else:
        q_ref, kc_ref, vc_ref, o_ref, m_scr, acc_scr = refs
        k_ref = v_ref = None
    kk = pl.program_id(2)
    nk = pl.num_programs(2)

    def attend(h, kblk, vblk):
        hs = slice(h * HEAD_PAD, (h + 1) * HEAD_PAD)
        q = q_ref[0, :, hs]
        s = lax.dot_general(q, kblk[0, :, hs], (((1,), (1,)), ((), ())), preferred_element_type=F32)
        m_prev = m_scr[h]
        m_new = jnp.maximum(m_prev, jnp.max(s, axis=-1, keepdims=True))
        alpha = jnp.exp2(m_prev - m_new)
        p = jnp.exp2(s - jnp.concatenate([m_new] * (s.shape[1] // LANES), axis=1))
        acc_scr[h] = alpha * acc_scr[h] + jnp.dot(p.astype(BF16), vblk[0, :, hs], preferred_element_type=F32)
        m_scr[h] = m_new

    @pl.when(kk == 0)
    def _():
        m_scr[...] = jnp.full(m_scr.shape, NEG, F32)
        acc_scr[...] = jnp.zeros(acc_scr.shape, F32)
        for h in range(MLA_HEADS):
            attend(h, kc_ref, vc_ref)

    if has_latent:
        for h in range(MLA_HEADS):
            attend(h, k_ref, v_ref)

    @pl.when(kk == nk - 1)
    def _():
        for h in range(MLA_HEADS):
            acc = acc_scr[h]
            o_ref[0, :, h * MLA_V:(h + 1) * MLA_V] = (
                acc[:, HEAD_PAD - MLA_V:] / acc[:, MLA_SUM_LANE:MLA_SUM_LANE + 1]).astype(BF16)


def _flash(q, kc, vc, k, v, tq, tk):
    b, lq, _ = q.shape
    lc = kc.shape[1]
    tq = min(tq, lq)
    has_latent = k is not None
    in_specs = [
        pl.BlockSpec((1, tq, MLA_PAD_W), lambda bi, i, kk: (bi, i, 0)),
        pl.BlockSpec((1, lc, MLA_PAD_W), lambda bi, i, kk: (bi, 0, 0)),
        pl.BlockSpec((1, lc, MLA_PAD_W), lambda bi, i, kk: (bi, 0, 0)),
    ]
    args = [q, kc, vc]
    nk = 1
    if has_latent:
        tk = min(tk, k.shape[1])
        nk = k.shape[1] // tk
        in_specs += [pl.BlockSpec((1, tk, MLA_PAD_W), lambda bi, i, kk: (bi, kk, 0))] * 2
        args += [k, v]
    return pl.pallas_call(
        functools.partial(_flash_kernel, has_latent=has_latent),
        grid=(b, lq // tq, nk),
        in_specs=in_specs,
        out_specs=pl.BlockSpec((1, tq, MLA_HEADS * MLA_V), lambda bi, i, kk: (bi, i, 0)),
        out_shape=jax.ShapeDtypeStruct((b, lq, MLA_HEADS * MLA_V), BF16),
        scratch_shapes=[pltpu.VMEM((MLA_HEADS, tq, LANES), F32), pltpu.VMEM((MLA_HEADS, tq, LANES), F32)],
        compiler_params=_cparams(("parallel", "parallel", "arbitrary")),
        name="mla_flash" if has_latent else "mla_ctx",
    )(*args)


NA_GROUP = 8
NA_GTOK = NA_GROUP * GRID_W


def _pair_softmax_pv(qp, k_list, v_list, bias_fn):
    lane = lax.broadcasted_iota(jnp.int32, qp.shape, 1)
    outs = []
    for sub in range(2):
        in_head = (lane >= sub * NA_HEAD_DIM) & (lane < (sub + 1) * NA_HEAD_DIM)
        qm = jnp.where(in_head, qp, jnp.zeros_like(qp))
        s_list = []
        for idx, kb in enumerate(k_list):
            s = lax.dot_general(qm, kb, (((1,), (1,)), ((), ())), preferred_element_type=F32)
            s_list.append(bias_fn(sub, idx, s))
        m = s_list[0].max(axis=-1, keepdims=True)
        for s in s_list[1:]:
            m = jnp.maximum(m, s.max(axis=-1, keepdims=True))
        o = None
        l = None
        for s, vb in zip(s_list, v_list):
            p = jnp.exp(s - m)
            ls = p.sum(axis=-1, keepdims=True)
            os_ = jnp.dot(p.astype(BF16), vb, preferred_element_type=F32)
            o = os_ if o is None else o + os_
            l = ls if l is None else l + ls
        outs.append(o / l)
    return jnp.where(lane < NA_HEAD_DIM, outs[0], outs[1])


NA_KROWS = 2 * NA_GROUP
NA_KTOK = NA_KROWS * GRID_W


N_REL_R = 2 * NA_WIN_R - 1
N_REL_C = 2 * NA_WIN_C - 1


def _na_group_geometry(variant, rows):
    ng = rows // NA_GROUP
    g = {0: 0, 1: 1, 2: ng - 1}[variant]
    strip0 = int(np.clip(g * NA_GROUP - NA_WIN_R // 2, 0, rows - NA_KROWS))
    r0 = [int(np.clip(g * NA_GROUP + i - NA_WIN_R // 2, 0, rows - NA_WIN_R)) for i in range(NA_GROUP)]
    return strip0 - g * NA_GROUP, [r - g * NA_GROUP for r in r0]


def _bias_table_kernel(rpb_ref, o_ref, *, rows):
    layer, h = pl.program_id(0), pl.program_id(1)
    shape = (GRID_W, LANES)
    qc = lax.broadcasted_iota(jnp.int32, shape, 0)
    lane = lax.broadcasted_iota(jnp.int32, shape, 1)
    kc = lane % GRID_W
    win0 = jnp.clip(qc - NA_WIN_C // 2, 0, GRID_W - NA_WIN_C)
    in_win = (kc >= win0) & (kc < win0 + NA_WIN_C)
    rel = jnp.clip(kc - qc + NA_WIN_C - 1, 0, N_REL_C - 1)
    left = lane < GRID_W
    base = (layer * NA_HEADS + h) * (N_REL_R * N_REL_C)
    neg = jnp.full(shape, NEG, F32)
    vals = []
    for a in range(N_REL_R):
        acc = jnp.zeros(shape, F32)
        for b in range(N_REL_C):
            acc = jnp.where(rel == b, rpb_ref[base + a * N_REL_C + b], acc)
        vals.append(jnp.where(in_win, acc, neg))

    for variant in range(3):
        strip_rel, r0_rel = _na_group_geometry(variant, rows)

        def half(i, j):
            key_rel = strip_rel + j
            if r0_rel[i] <= key_rel < r0_rel[i] + NA_WIN_R:
                return vals[key_rel - i + NA_WIN_R - 1]
            return neg

        for i in range(NA_GROUP):
            for jp in range(NA_KROWS // 2):
                lft, rgt = half(i, 2 * jp), half(i, 2 * jp + 1)
                blk = lft if lft is rgt else jnp.where(left, lft, rgt)
                o_ref[0, variant, 0, i * GRID_W:(i + 1) * GRID_W, jp * LANES:(jp + 1) * LANES] = blk


def _bias_tables(na_rpb, rows):
    depth = na_rpb.shape[0]
    return pl.pallas_call(
        functools.partial(_bias_table_kernel, rows=rows),
        grid=(depth, NA_HEADS),
        in_specs=[pl.BlockSpec(memory_space=pltpu.SMEM)],
        out_specs=pl.BlockSpec((1, 3, 1, NA_GTOK, NA_KTOK), lambda l, h: (l, 0, h, 0, 0)),
        out_shape=jax.ShapeDtypeStruct((depth, 3, NA_HEADS, NA_GTOK, NA_KTOK), F32),
        compiler_params=_cparams(("arbitrary", "arbitrary")),
        name="na_bias",
    )(na_rpb.reshape(-1))


def _na_windows(variant, rows):
    strip_rel, r0_rel = _na_group_geometry(variant, rows)
    out = []
    for rp in range(NA_GROUP // 2):
        w0 = [r0_rel[i] - strip_rel for i in (2 * rp, 2 * rp + 1)]
        lo, hi = min(w0) * GRID_W, (max(w0) + NA_WIN_R) * GRID_W
        out.append((lo // LANES * LANES, -(-hi // LANES) * LANES))
    return out


def _na_pair(qp, k_w, v_w, k_x, v_x, bias_ref, pr, windows):
    lane = lax.broadcasted_iota(jnp.int32, qp.shape, 1)
    pair_rows = 2 * GRID_W
    outs = []
    for sub in range(2):
        in_head = (lane >= sub * NA_HEAD_DIM) & (lane < (sub + 1) * NA_HEAD_DIM)
        qm = jnp.where(in_head, qp, jnp.zeros_like(qp))
        s_w = lax.dot_general(qm, k_w, (((1,), (1,)), ((), ())), preferred_element_type=F32)
        s_c = lax.dot_general(qm, k_x, (((1,), (1,)), ((), ())), preferred_element_type=F32)
        p_rows, pc_rows, l_rows = [], [], []
        for rp, (c0, c1) in enumerate(windows):
            rs = slice(rp * pair_rows, (rp + 1) * pair_rows)
            sw = s_w[rs, c0:c1] + bias_ref[0, 0, 2 * pr + sub, rs, c0:c1]
            sc = s_c[rs]
            m = jnp.maximum(sw.max(axis=-1, keepdims=True), sc.max(axis=-1, keepdims=True))
            pw = jnp.exp(sw - m)
            pc = jnp.exp(sc - m)
            l_rows.append(pw.sum(axis=-1, keepdims=True) + pc.sum(axis=-1, keepdims=True))
            parts = []
            if c0 > 0:
                parts.append(jnp.zeros((pair_rows, c0), BF16))
            parts.append(pw.astype(BF16))
            if c1 < NA_KTOK:
                parts.append(jnp.zeros((pair_rows, NA_KTOK - c1), BF16))
            p_rows.append(jnp.concatenate(parts, axis=1) if len(parts) > 1 else parts[0])
            pc_rows.append(pc.astype(BF16))
        o = (jnp.dot(jnp.concatenate(p_rows, axis=0), v_w, preferred_element_type=F32)
             + jnp.dot(jnp.concatenate(pc_rows, axis=0), v_x, preferred_element_type=F32))
        outs.append(o / jnp.concatenate(l_rows, axis=0))
    return jnp.where(lane < NA_HEAD_DIM, outs[0], outs[1])


def _na_kernel(q_ref, kp_ref, kc_ref, kn_ref, vp_ref, vc_ref, vn_ref, kx_ref, vx_ref, bias_ref, o_ref,
               kcat, vcat, *, rows):
    g = pl.program_id(0)
    ng = pl.num_programs(0)
    kcat[0:NA_GTOK] = kp_ref[0]
    kcat[NA_GTOK:2 * NA_GTOK] = kc_ref[0]
    kcat[2 * NA_GTOK:3 * NA_GTOK] = kn_ref[0]
    vcat[0:NA_GTOK] = vp_ref[0]
    vcat[NA_GTOK:2 * NA_GTOK] = vc_ref[0]
    vcat[2 * NA_GTOK:3 * NA_GTOK] = vn_ref[0]
    strip0 = jnp.clip(g * NA_GROUP - NA_WIN_R // 2, 0, rows - NA_KROWS)
    start = pl.multiple_of((strip0 - g * NA_GROUP + NA_GROUP) * GRID_W, GRID_W)
    variant = jnp.where(g == 0, 0, jnp.where(g == ng - 1, 2, 1))
    for vi in range(3):
        @pl.when(variant == vi)
        def _(vi=vi):
            windows = _na_windows(vi, rows)
            for pr in range(NA_HEADS // 2):
                ps = slice(pr * LANES, (pr + 1) * LANES)
                k_w = kcat[pl.ds(start, NA_KTOK), ps]
                v_w = vcat[pl.ds(start, NA_KTOK), ps]
                o = _na_pair(q_ref[0, :, ps], k_w, v_w, kx_ref[0, :, ps], vx_ref[0, :, ps], bias_ref, pr, windows)
                o_ref[0, :, ps] = o.astype(BF16)


def _na(z, zc, bias, layer):
    b, s, _ = z.shape
    lc = zc.shape[1]
    rows = s // GRID_W
    ng = rows // NA_GROUP
    blk = (1, NA_GTOK, CB)

    def spec(col, off):
        return pl.BlockSpec(blk, lambda g, bi: (bi, jnp.clip(g + off, 0, ng - 1), col // CB))

    def bias_map(g, bi):
        return (layer, jnp.where(g == 0, 0, jnp.where(g == ng - 1, 2, 1)), 0, 0, 0)

    return pl.pallas_call(
        functools.partial(_na_kernel, rows=rows),
        grid=(ng, b),
        in_specs=[
            spec(Z_Q, 0),
            spec(Z_K, -1), spec(Z_K, 0), spec(Z_K, 1),
            spec(Z_V, -1), spec(Z_V, 0), spec(Z_V, 1),
            pl.BlockSpec((1, lc, CB), lambda g, bi: (bi, 0, Z_K // CB)),
            pl.BlockSpec((1, lc, CB), lambda g, bi: (bi, 0, Z_V // CB)),
            pl.BlockSpec((1, 1) + bias.shape[2:], bias_map, pipeline_mode=pl.Buffered(1)),
        ],
        out_specs=pl.BlockSpec(blk, lambda g, bi: (bi, g, 0)),
        out_shape=jax.ShapeDtypeStruct((b, s, NA_WIDTH), BF16),
        scratch_shapes=[pltpu.VMEM((3 * NA_GTOK, CB), BF16), pltpu.VMEM((3 * NA_GTOK, CB), BF16)],
        compiler_params=_cparams(("arbitrary", "arbitrary")),
        name="na_attn",
    )(z, z, z, z, z, z, z, zc, zc, bias)


def _ctx_na_kernel(q_ref, k_ref, v_ref, o_ref):
    for pr in range(NA_HEADS // 2):
        ps = slice(pr * LANES, (pr + 1) * LANES)
        o = _pair_softmax_pv(q_ref[0, :, ps], [k_ref[0, :, ps]], [v_ref[0, :, ps]], lambda sub, idx, s: s)
        o_ref[0, :, ps] = o.astype(BF16)


def _ctx_na(zc):
    b, lc, _ = zc.shape
    spec = lambda col: pl.BlockSpec((1, lc, CB), lambda bi: (bi, 0, col // CB))
    return pl.pallas_call(
        _ctx_na_kernel,
        grid=(b,),
        in_specs=[spec(Z_Q), spec(Z_K), spec(Z_V)],
        out_specs=pl.BlockSpec((1, lc, NA_WIDTH), lambda bi: (bi, 0, 0)),
        out_shape=jax.ShapeDtypeStruct((b, lc, NA_WIDTH), BF16),
        compiler_params=_cparams(("parallel",)),
        name="ctx_na",
    )(zc, zc, zc)


def _pool_tile(up_ref, u_ref, un_ref, w_ref, sc_ref, i, seq_len):
    tm = u_ref.shape[1]
    ext = jnp.concatenate([up_ref[0], u_ref[0], un_ref[0]], axis=0).astype(F32)
    n_ext = tm + 2 * POOL_HALO
    tg = i * tm - POOL_HALO + lax.broadcasted_iota(jnp.int32, (n_ext, 1), 0)
    ext = jnp.where((tg >= 0) & (tg < seq_len), ext, 0.0)
    t = i * tm + lax.broadcasted_iota(jnp.int32, (tm, 1), 0)

    def shifted(a, k):
        return pltpu.roll(a, (-k) % n_ext, 0)

    outs = []
    for gi, win in enumerate(POOL_WINDOWS):
        gs = slice(gi * POOL_GROUP, (gi + 1) * POOL_GROUP)
        a = ext[:, gs]
        wsum = a + shifted(a, -1)
        half = 1
        while 2 * half < win:
            wsum = shifted(wsum, -half) + shifted(wsum, half)
            half *= 2
        cnt = (jnp.minimum(t + win // 2, seq_len) - jnp.maximum(t - win // 2, 0)).astype(F32)
        d = wsum[POOL_HALO:POOL_HALO + tm] / cnt - a[POOL_HALO:POOL_HALO + tm]
        y = jnp.dot(d.astype(BF16), w_ref[gi], preferred_element_type=F32)
        outs.append((y * sc_ref[:, gs]).astype(BF16))
    return jnp.concatenate(outs, axis=1)


def _merge_kernel(ona_ref, up_ref, u_ref, un_ref, omla_ref, g0_ref, g1_ref, g2_ref, pw_ref, psc_ref, wb_ref, wo_ref,
                  x_ref, gate_ref, gain_ref, o_ref, *, seq_len):
    o_pool = _pool_tile(up_ref, u_ref, un_ref, pw_ref, psc_ref, pl.program_id(1), seq_len)
    m = None
    for br, g_ref, k in ((ona_ref[0], g0_ref, 0), (o_pool, g1_ref, 1), (omla_ref[0], g2_ref, 2)):
        proj = jnp.dot(br, wb_ref[k], preferred_element_type=F32)
        term = g_ref[0].astype(F32) * proj
        m = term if m is None else m + term
    y = jnp.dot(m.astype(BF16), wo_ref[...], preferred_element_type=F32)
    o_ref[0] = x_ref[0] + gate_ref[0] * _rms(y, gain_ref[...])


def _merge(o_na, z, o_mla, gz, pw, psc, wb, wo, x, gate, gain, layer, tm):
    b, l, d = x.shape
    tm = min(tm, l)
    hb = tm // POOL_HALO
    nhb = l // POOL_HALO
    ucol = Z_U // CB
    per_batch = gate.shape[0] > 1
    mod_map = (lambda bi, i: (bi, 0, 0)) if per_batch else (lambda bi, i: (0, 0, 0))
    br = pl.BlockSpec((1, tm, BRANCH_W), lambda bi, i: (bi, i, 0))
    gspec = lambda k: pl.BlockSpec((1, tm, d), lambda bi, i: (bi, i, k))
    return pl.pallas_call(
        functools.partial(_merge_kernel, seq_len=l),
        grid=(b, l // tm),
        in_specs=[
            br,
            pl.BlockSpec((1, POOL_HALO, CB), lambda bi, i: (bi, jnp.maximum(i * hb - 1, 0), ucol)),
            pl.BlockSpec((1, tm, CB), lambda bi, i: (bi, i, ucol)),
            pl.BlockSpec((1, POOL_HALO, CB), lambda bi, i: (bi, jnp.minimum((i + 1) * hb, nhb - 1), ucol)),
            br, gspec(0), gspec(1), gspec(2),
            pl.BlockSpec((None,) + pw.shape[1:], lambda bi, i: (layer, 0, 0, 0)),
            pl.BlockSpec((1, POOL_WIDTH), lambda bi, i: (0, 0)),
            pl.BlockSpec((None,) + wb.shape[1:], lambda bi, i: (layer, 0, 0, 0)),
            pl.BlockSpec((None,) + wo.shape[1:], lambda bi, i: (layer, 0, 0)),
            pl.BlockSpec((1, tm, d), lambda bi, i: (bi, i, 0)),
            pl.BlockSpec((1, 1, d), mod_map),
            pl.BlockSpec((1, d), lambda bi, i: (0, 0)),
        ],
        out_specs=pl.BlockSpec((1, tm, d), lambda bi, i: (bi, i, 0)),
        out_shape=jax.ShapeDtypeStruct((b, l, d), F32),
        compiler_params=_cparams(("parallel", "parallel")),
        name="merge",
    )(o_na, z, z, z, o_mla, gz, gz, gz, pw, psc, wb, wo, x, gate, gain)


def _gelu_tanh(x):
    return 0.5 * x * (1.0 + jnp.tanh(np.float32(np.sqrt(2.0 / np.pi)) * (x + np.float32(0.044715) * (x * x * x))))


def _ffn_kernel(xp_ref, x_ref, xn_ref, gain_ref, sc_ref, sh_ref, wa_ref, wb_ref, cwa_ref, cwb_ref, cba_ref,
                cbb_ref, wd_ref, gate_ref, gpost_ref, o_ref, h_scr, acc_scr):
    i = pl.program_id(1)
    j = pl.program_id(2)
    ni = pl.num_programs(1)
    nj = pl.num_programs(2)
    tm = x_ref.shape[1]
    n_ext = tm + 2 * CONV_HALO

    @pl.when(j == 0)
    def _():
        def norm_mod(xv):
            return _rms(xv, gain_ref[...]) * (1.0 + sc_ref[0]) + sh_ref[0]

        hp = jnp.where(i > 0, norm_mod(xp_ref[0]), 0.0)
        hn = jnp.where(i < ni - 1, norm_mod(xn_ref[0]), 0.0)
        h_scr[0:CONV_HALO] = hp.astype(BF16)
        h_scr[CONV_HALO:CONV_HALO + tm] = norm_mod(x_ref[0]).astype(BF16)
        h_scr[CONV_HALO + tm:n_ext] = hn.astype(BF16)
        acc_scr[...] = jnp.zeros(acc_scr.shape, F32)

    h = h_scr[...]

    def conv_half(w_ref, cw_ref, cb_ref):
        u = jnp.dot(h, w_ref[...], preferred_element_type=F32)
        prev = pltpu.roll(u, 1, 0)[CONV_HALO:CONV_HALO + tm]
        nxt = pltpu.roll(u, n_ext - 1, 0)[CONV_HALO:CONV_HALO + tm]
        cur = u[CONV_HALO:CONV_HALO + tm]
        return cb_ref[...] + prev * cw_ref[0:1] + cur * cw_ref[1:2] + nxt * cw_ref[2:3]

    a = conv_half(wa_ref, cwa_ref, cba_ref)
    bgate = conv_half(wb_ref, cwb_ref, cbb_ref)
    act = (_gelu_tanh(a) * bgate).astype(BF16)
    acc_scr[...] += jnp.dot(act, wd_ref[...], preferred_element_type=F32)

    @pl.when(j == nj - 1)
    def _():
        o_ref[0] = x_ref[0] + gate_ref[0] * _rms(acc_scr[...], gpost_ref[...])


def _ffn(x, gain, sc, sh, w_up, conv_w, conv_b, w_down, gate, gpost, layer, tm, tn):
    b, l, d = x.shape
    tm = min(tm, l)
    nch = D_FF // tn
    hb = tm // CONV_HALO
    nhb = l // CONV_HALO
    per_batch = sc.shape[0] > 1
    mod_map = (lambda bi, i, j: (bi, 0, 0)) if per_batch else (lambda bi, i, j: (0, 0, 0))
    mod = pl.BlockSpec((1, 1, d), mod_map)
    vec = pl.BlockSpec((1, d), lambda bi, i, j: (0, 0))
    return pl.pallas_call(
        _ffn_kernel,
        grid=(b, l // tm, nch),
        in_specs=[
            pl.BlockSpec((1, CONV_HALO, d), lambda bi, i, j: (bi, jnp.maximum(i * hb - 1, 0), 0)),
            pl.BlockSpec((1, tm, d), lambda bi, i, j: (bi, i, 0)),
            pl.BlockSpec((1, CONV_HALO, d), lambda bi, i, j: (bi, jnp.minimum((i + 1) * hb, nhb - 1), 0)),
            vec, mod, mod,
            pl.BlockSpec((None, d, tn), lambda bi, i, j: (layer, 0, j)),
            pl.BlockSpec((None, d, tn), lambda bi, i, j: (layer, 0, nch + j)),
            pl.BlockSpec((None, 3, tn), lambda bi, i, j: (layer, 0, j)),
            pl.BlockSpec((None, 3, tn), lambda bi, i, j: (layer, 0, nch + j)),
            pl.BlockSpec((None, 1, tn), lambda bi, i, j: (layer, 0, j)),
            pl.BlockSpec((None, 1, tn), lambda bi, i, j: (layer, 0, nch + j)),
            pl.BlockSpec((None, tn, d), lambda bi, i, j: (layer, j, 0)),
            mod, vec,
        ],
        out_specs=pl.BlockSpec((1, tm, d), lambda bi, i, j: (bi, i, 0)),
        out_shape=jax.ShapeDtypeStruct((b, l, d), F32),
        scratch_shapes=[pltpu.VMEM((tm + 2 * CONV_HALO, d), BF16), pltpu.VMEM((tm, d), F32)],
        compiler_params=_cparams(("parallel", "parallel", "arbitrary")),
        name="ffn",
    )(x, x, x, gain, sc, sh, w_up, w_up, conv_w, conv_w, conv_b, conv_b, w_down, gate, gpost)


_ROPE_PERM = np.concatenate([np.arange(8, 16), np.arange(0, 8), np.arange(24, 32), np.arange(16, 24)])


def _rope_tables(n_tok, rotate):
    f32 = np.float32
    ct = np.ones((n_tok, HEAD_PAD), f32)
    st = np.zeros((n_tok, HEAD_PAD), f32)
    if rotate:
        n_freq = MLA_ROPE // 4
        inv = (f32(ROPE_BASE) ** (-np.arange(n_freq, dtype=f32) / f32(n_freq))).astype(f32)
        t = np.arange(n_tok, dtype=np.int32)
        ang_r = ((t // GRID_W).astype(f32)[:, None] * inv[None, :]).astype(f32)
        ang_c = ((t % GRID_W).astype(f32)[:, None] * inv[None, :]).astype(f32)
        cr, sr, cc, sn = np.cos(ang_r), np.sin(ang_r), np.cos(ang_c), np.sin(ang_c)
        ct[:, MLA_NOPE:MLA_NOPE + MLA_ROPE] = np.concatenate([cr, cr, cc, cc], axis=1)
        st[:, MLA_NOPE:MLA_NOPE + MLA_ROPE] = np.concatenate([-sr, sr, -sn, sn], axis=1)
    return jnp.asarray(ct), jnp.asarray(st)


def _prep_all(w_in, w_uq, w_ukv):
    depth, d, _ = w_in.shape
    w_in = w_in.astype(BF16)
    o = 0
    parts = []
    for s in (NA_WIDTH, NA_WIDTH, NA_WIDTH, POOL_WIDTH, MLA_Q_RANK, MLA_KV_RANK, MLA_ROPE, N_BRANCH * D_MODEL):
        parts.append(w_in[..., o:o + s])
        o += s
    wq, wk, wv, wu, wcq, wckv, wkr, wg = parts
    z64 = jnp.zeros((depth, d, MLA_NOPE), BF16)
    z32 = jnp.zeros((depth, d, HEAD_PAD - MLA_NOPE - MLA_ROPE), BF16)
    w_z = jnp.concatenate([wq * jnp.asarray(NA_HEAD_DIM ** -0.5, BF16), wk, wv, wu, wcq, wckv,
                           z64, wkr, z32, z64, wkr[..., _ROPE_PERM], z32], axis=-1)

    uq = w_uq.reshape(depth, MLA_Q_RANK, MLA_HEADS, MLA_NOPE + MLA_ROPE)
    qz64 = jnp.zeros((depth, MLA_Q_RANK, MLA_HEADS, MLA_NOPE), F32)
    qz32 = jnp.zeros((depth, MLA_Q_RANK, MLA_HEADS, HEAD_PAD - MLA_NOPE - MLA_ROPE), F32)
    wq_pad = jnp.concatenate([uq, qz32], axis=-1)
    wq_perm = jnp.concatenate([qz64, uq[..., MLA_NOPE:][..., _ROPE_PERM], qz32], axis=-1)
    assert MLA_NOPE + MLA_V == HEAD_PAD

    flat = lambda w, k: w.reshape(depth, k, MLA_PAD_W).astype(BF16)
    return dict(
        w_z=w_z.astype(BF16),
        w_g=wg.astype(BF16),
        wq_pad=flat(wq_pad, MLA_Q_RANK),
        wq_perm=flat(wq_perm, MLA_Q_RANK),
        w_kv=w_ukv.astype(BF16),
    )


TM_PROJ = 1024
TN_PROJ = 1536
TM_UP = 1024
TQ_FLASH = 1024
TK_FLASH = 2048
TM_MERGE = 1024
TM_FFN = 1024
TN_FFN = 1408


def kernel(x, c, ctx, c_ctx, w_ada, b_ada, norm_pre1, norm_post1, norm_pre2, norm_post2, w_in, na_rpb, pool_w,
           pool_scale, mla_q_norm, w_uq, mla_kv_norm, w_ukv, w_branch, w_o, w_up, conv_w, conv_b, w_down):
    b, s, d = x.shape
    lc = ctx.shape[1]
    depth = w_ada.shape[0]
    rows = s // GRID_W
    assert d == D_MODEL and s % NA_KTOK == 0 and lc % POOL_HALO == 0

    n_mod = -(-(b + 1) // 8) * 8
    cvec = jnp.concatenate([c, c_ctx[None, :], jnp.zeros((n_mod - b - 1, d), F32)], axis=0)
    mod = _ada(cvec, w_ada, b_ada).reshape(depth, n_mod, 6, d)

    ct_l, st_l = _rope_tables(s, True)
    ct_c, st_c = _rope_tables(b * lc, False)

    p = _prep_all(w_in, w_uq, w_ukv)
    na_bias = _bias_tables(na_rpb, rows)
    pw, wb, wo = pool_w.astype(BF16), w_branch.astype(BF16), w_o.astype(BF16)
    wup, wdn = w_up.astype(BF16), w_down.astype(BF16)
    cb = conv_b[:, None, :]

    xc = ctx
    for l in range(depth):
        last = l == depth - 1
        lat = [mod[l, :b, k][:, None, :] for k in range(6)]
        cx = [mod[l, b:b + 1, k][:, None, :] for k in range(6)]
        row = lambda v: v.reshape(1, -1)
        g_pre1, g_post1, g_pre2, g_post2 = row(norm_pre1[l]), row(norm_post1[l]), row(norm_pre2[l]), row(norm_post2[l])
        qg, kvg = row(mla_q_norm[l]), row(mla_kv_norm[l])
        psc = row(pool_scale[l])

        z, gz = _in_proj(x, g_pre1, lat[1], lat[0], p["w_z"], p["w_g"], l, TM_PROJ, TN_PROJ)
        flat = lambda a: a.reshape(1, b * lc, a.shape[-1])
        unflat = lambda a: a.reshape(b, lc, a.shape[-1])
        zc_f, gzc_f = _in_proj(flat(xc), g_pre1, cx[1], cx[0], p["w_z"], p["w_g"], l, TM_PROJ, TN_PROJ)
        zc, gzc = unflat(zc_f), unflat(gzc_f)
        mla_w = (p["wq_pad"], p["wq_perm"], p["w_kv"], l)
        q_l, k_l, v_l = _mla_up(z, qg, kvg, ct_l, st_l, *mla_w, TM_UP)
        q_c, k_c, v_c = [unflat(a) for a in _mla_up(zc_f, qg, kvg, ct_c, st_c, *mla_w, TM_UP)]

        o_na = _na(z, zc, na_bias, l)
        o_mla = _flash(q_l, k_c, v_c, k_l, v_l, TQ_FLASH, TK_FLASH)
        x = _merge(o_na, z, o_mla, gz, pw, psc, wb, wo, x, lat[2], g_post1, l, TM_MERGE)
        x = _ffn(x, g_pre2, lat[4], lat[3], wup, conv_w, cb, wdn, lat[5], g_post2, l, TM_FFN, TN_FFN)

        if not last:
            oc_na = _ctx_na(zc)
            oc_mla = _flash(q_c, k_c, v_c, None, None, TQ_FLASH, TK_FLASH)
            xc = _merge(oc_na, zc, oc_mla, gzc, pw, psc, wb, wo, xc, cx[2], g_post1, l, TM_MERGE)
            xc = _ffn(xc, g_pre2, cx[4], cx[3], wup, conv_w, cb, wdn, cx[5], g_post2, l, TM_FFN, TN_FFN)
    return x
```

```python
import functools

import jax
import jax.numpy as jnp
import numpy as np
from jax import lax
from jax.experimental import pallas as pl
from jax.experimental.pallas import tpu as pltpu

F32 = jnp.float32
BF16 = jnp.bfloat16

D_MODEL = 1024
GRID_W = 64
EPS = 1e-6
NA_HEADS = 8
NA_HEAD_DIM = 64
NA_WIDTH = NA_HEADS * NA_HEAD_DIM
NA_WIN_R = 8
NA_WIN_C = 16
POOL_WINDOWS = (2, 4, 8, 16)
POOL_GROUP = 128
POOL_WIDTH = POOL_GROUP * len(POOL_WINDOWS)
POOL_HALO = 16
MLA_HEADS = 8
MLA_NOPE = 64
MLA_ROPE = 32
MLA_V = 64
MLA_Q_RANK = 512
MLA_KV_RANK = 256
ROPE_BASE = 10000.0
N_BRANCH = 3
BRANCH_W = 512
D_FF = 2816
CONV_HALO = 8

LANES = 128
HEAD_PAD = LANES
MLA_PAD_W = MLA_HEADS * HEAD_PAD
MLA_SUM_LANE = 0

Z_Q = 0
Z_K = Z_Q + NA_WIDTH
Z_V = Z_K + NA_WIDTH
Z_U = Z_V + NA_WIDTH
Z_CQ = Z_U + POOL_WIDTH
Z_KV = Z_CQ + MLA_Q_RANK
Z_COLS = Z_KV + 512
G_COLS = N_BRANCH * D_MODEL
assert G_COLS == Z_COLS
CB = 512

NEG = -1e30
VMEM_LIMIT = 56 * 1024 * 1024


def _cparams(sem):
    return pltpu.CompilerParams(dimension_semantics=sem, vmem_limit_bytes=VMEM_LIMIT)


def _rms(x, gain):
    return x * lax.rsqrt(jnp.mean(x * x, axis=-1, keepdims=True) + EPS) * gain


def _ada_kernel(c_ref, w_ref, b_ref, o_ref):
    c = c_ref[...]
    s = c * jax.nn.sigmoid(c)
    o_ref[0] = jnp.dot(s.astype(BF16), w_ref[0].astype(BF16), preferred_element_type=F32) + b_ref[0]


def _ada(cvec, w_ada, b_ada):
    depth, d, n = w_ada.shape
    rows = cvec.shape[0]
    tn = 1536
    return pl.pallas_call(
        _ada_kernel,
        grid=(depth, n // tn),
        in_specs=[
            pl.BlockSpec((rows, d), lambda l, j: (0, 0)),
            pl.BlockSpec((1, d, tn), lambda l, j: (l, 0, j)),
            pl.BlockSpec((1, 1, tn), lambda l, j: (l, 0, j)),
        ],
        out_specs=pl.BlockSpec((1, rows, tn), lambda l, j: (l, 0, j)),
        out_shape=jax.ShapeDtypeStruct((depth, rows, n), F32),
        compiler_params=_cparams(("arbitrary", "arbitrary")),
        name="ada",
    )(cvec, w_ada, b_ada.reshape(depth, 1, n))


def _in_proj_kernel(x_ref, gain_ref, sc_ref, sh_ref, wz_ref, wg_ref, z_ref, g_ref, h_scr):
    j = pl.program_id(2)

    @pl.when(j == 0)
    def _():
        h = _rms(x_ref[0], gain_ref[...]) * (1.0 + sc_ref[0]) + sh_ref[0]
        h_scr[...] = h.astype(BF16)

    h = h_scr[...]
    g_ref[0] = jax.nn.sigmoid(jnp.dot(h, wg_ref[...], preferred_element_type=F32)).astype(BF16)
    z_ref[0] = jnp.dot(h, wz_ref[...], preferred_element_type=F32).astype(BF16)


def _in_proj(x, gain, sc, sh, wz, wg, layer, tm, tn):
    b, l, d = x.shape
    tm = min(tm, l)
    per_batch = sc.shape[0] > 1
    mod_map = (lambda bi, i, j: (bi, 0, 0)) if per_batch else (lambda bi, i, j: (0, 0, 0))
    wspec = pl.BlockSpec((None, d, tn), lambda bi, i, j: (layer, 0, j))
    ospec = pl.BlockSpec((1, tm, tn), lambda bi, i, j: (bi, i, j))
    oshape = jax.ShapeDtypeStruct((b, l, Z_COLS), BF16)
    return pl.pallas_call(
        _in_proj_kernel,
        grid=(b, l // tm, Z_COLS // tn),
        in_specs=[
            pl.BlockSpec((1, tm, d), lambda bi, i, j: (bi, i, 0)),
            pl.BlockSpec((1, d), lambda bi, i, j: (0, 0)),
            pl.BlockSpec((1, 1, d), mod_map),
            pl.BlockSpec((1, 1, d), mod_map),
            wspec, wspec,
        ],
        out_specs=[ospec, ospec],
        out_shape=[oshape, oshape],
        scratch_shapes=[pltpu.VMEM((tm, d), BF16)],
        compiler_params=_cparams(("parallel", "parallel", "arbitrary")),
        name="in_proj",
    )(x, gain, sc, sh, wz, wg)


def _mla_up_kernel(cq_ref, kv_ref, qg_ref, kvg_ref, ct_ref, st_ref, wq_ref, wqp_ref, wkv_ref,
                   q_out, k_out, v_out, *, scale):
    cqn = _rms(cq_ref[0].astype(F32), qg_ref[...]).astype(BF16)
    kvb = kv_ref[0].astype(F32)
    kvn = _rms(kvb[:, :MLA_KV_RANK], kvg_ref[...]).astype(BF16)
    ct = ct_ref[...]
    st = st_ref[...]
    kr = kvb[:, MLA_KV_RANK:MLA_KV_RANK + LANES] * ct + kvb[:, MLA_KV_RANK + LANES:] * st
    qm = jnp.dot(cqn, wq_ref[...], preferred_element_type=F32)
    qp = jnp.dot(cqn, wqp_ref[...], preferred_element_type=F32)
    kv = jnp.dot(kvn, wkv_ref[...], preferred_element_type=F32)
    cts = ct * scale
    sts = st * scale
    lane = lax.broadcasted_iota(jnp.int32, kr.shape, 1)
    is_k = lane < MLA_NOPE
    row_sum_col = jnp.where(lane == MLA_SUM_LANE, 1.0, 0.0)
    for h in range(MLA_HEADS):
        hs = slice(h * HEAD_PAD, (h + 1) * HEAD_PAD)
        q_out[0, :, hs] = (qm[:, hs] * cts + qp[:, hs] * sts).astype(BF16)
        k_out[0, :, hs] = jnp.where(is_k, kv[:, hs], kr).astype(BF16)
        v_out[0, :, hs] = jnp.where(is_k, row_sum_col, kv[:, hs]).astype(BF16)


def _mla_up(z, qg, kvg, ct, st, wq, wqp, wkv, layer, tm):
    b, l, _ = z.shape
    tm = min(tm, l)
    full = lambda shape: pl.BlockSpec(shape, lambda bi, i: (0,) * len(shape))
    wspec = lambda k: pl.BlockSpec((None, k, MLA_PAD_W), lambda bi, i: (layer, 0, 0))
    out = jax.ShapeDtypeStruct((b, l, MLA_PAD_W), BF16)
    ospec = pl.BlockSpec((1, tm, MLA_PAD_W), lambda bi, i: (bi, i, 0))
    return pl.pallas_call(
        functools.partial(_mla_up_kernel, scale=float((MLA_NOPE + MLA_ROPE) ** -0.5 * np.log2(np.e))),
        grid=(b, l // tm),
        in_specs=[
            pl.BlockSpec((1, tm, CB), lambda bi, i: (bi, i, Z_CQ // CB)),
            pl.BlockSpec((1, tm, CB), lambda bi, i: (bi, i, Z_KV // CB)),
            full((1, MLA_Q_RANK)),
            full((1, MLA_KV_RANK)),
            pl.BlockSpec((tm, LANES), lambda bi, i: (i, 0)),
            pl.BlockSpec((tm, LANES), lambda bi, i: (i, 0)),
            wspec(MLA_Q_RANK), wspec(MLA_Q_RANK), wspec(MLA_KV_RANK),
        ],
        out_specs=[ospec, ospec, ospec],
        out_shape=[out, out, out],
        compiler_params=_cparams(("parallel", "parallel")),
        name="mla_up",
    )(z, z, qg, kvg, ct, st, wq, wqp, wkv)


def _flash_kernel(*refs, has_latent):
    if has_latent:
        q_ref, kc_ref, vc_ref, k_ref, v_ref, o_ref, m_scr, acc_scr = refs
    else:
        q_ref, kc_ref, vc_ref, o_ref, m_scr, acc_scr = refs
        k_ref = v_ref = None
    kk = pl.program_id(2)
    nk = pl.num_programs(2)

    def attend(h, kblk, vblk):
        hs = slice(h * HEAD_PAD, (h + 1) * HEAD_PAD)
        q = q_ref[0, :, hs]
        s = lax.dot_general(q, kblk[0, :, hs], (((1,), (1,)), ((), ())), preferred_element_type=F32)
        m_prev = m_scr[h]
        m_new = jnp.maximum(m_prev, jnp.max(s, axis=-1, keepdims=True))
        alpha = jnp.exp2(m_prev - m_new)
        p = jnp.exp2(s - jnp.concatenate([m_new] * (s.shape[1] // LANES), axis=1))
        acc_scr[h] = alpha * acc_scr[h] + jnp.dot(p.astype(BF16), vblk[0, :, hs], preferred_element_type=F32)
        m_scr[h] = m_new

    @pl.when(kk == 0)
    def _():
        m_scr[...] = jnp.full(m_scr.shape, NEG, F32)
        acc_scr[...] = jnp.zeros(acc_scr.shape, F32)
        for h in range(MLA_HEADS):
            attend(h, kc_ref, vc_ref)

    if has_latent:
        for h in range(MLA_HEADS):
            attend(h, k_ref, v_ref)

    @pl.when(kk == nk - 1)
    def _():
        for h in range(MLA_HEADS):
            acc = acc_scr[h]
            o_ref[0, :, h * MLA_V:(h + 1) * MLA_V] = (
                acc[:, HEAD_PAD - MLA_V:] / acc[:, MLA_SUM_LANE:MLA_SUM_LANE + 1]).astype(BF16)


def _flash(q, kc, vc, k, v, tq, tk):
    b, lq, _ = q.shape
    lc = kc.shape[1]
    tq = min(tq, lq)
    has_latent = k is not None
    in_specs = [
        pl.BlockSpec((1, tq, MLA_PAD_W), lambda bi, i, kk: (bi, i, 0)),
        pl.BlockSpec((1, lc, MLA_PAD_W), lambda bi, i, kk: (bi, 0, 0)),
        pl.BlockSpec((1, lc, MLA_PAD_W), lambda bi, i, kk: (bi, 0, 0)),
    ]
    args = [q, kc, vc]
    nk = 1
    if has_latent:
        tk = min(tk, k.shape[1])
        nk = k.shape[1] // tk
        in_specs += [pl.BlockSpec((1, tk, MLA_PAD_W), lambda bi, i, kk: (bi, kk, 0))] * 2
        args += [k, v]
    return pl.pallas_call(
        functools.partial(_flash_kernel, has_latent=has_latent),
        grid=(b, lq // tq, nk),
        in_specs=in_specs,
        out_specs=pl.BlockSpec((1, tq, MLA_HEADS * MLA_V), lambda bi, i, kk: (bi, i, 0)),
        out_shape=jax.ShapeDtypeStruct((b, lq, MLA_HEADS * MLA_V), BF16),
        scratch_shapes=[pltpu.VMEM((MLA_HEADS, tq, LANES), F32), pltpu.VMEM((MLA_HEADS, tq, LANES), F32)],
        compiler_params=_cparams(("parallel", "parallel", "arbitrary")),
        name="mla_flash" if has_latent else "mla_ctx",
    )(*args)


NA_GROUP = 8
NA_GTOK = NA_GROUP * GRID_W


def _pair_softmax_pv(qp, k_list, v_list, bias_fn):
    lane = lax.broadcasted_iota(jnp.int32, qp.shape, 1)
    outs = []
    for sub in range(2):
        in_head = (lane >= sub * NA_HEAD_DIM) & (lane < (sub + 1) * NA_HEAD_DIM)
        qm = jnp.where(in_head, qp, jnp.zeros_like(qp))
        s_list = []
        for idx, kb in enumerate(k_list):
            s = lax.dot_general(qm, kb, (((1,), (1,)), ((), ())), preferred_element_type=F32)
            s_list.append(bias_fn(sub, idx, s))
        m = s_list[0].max(axis=-1, keepdims=True)
        for s in s_list[1:]:
            m = jnp.maximum(m, s.max(axis=-1, keepdims=True))
        o = None
        l = None
        for s, vb in zip(s_list, v_list):
            p = jnp.exp(s - m)
            ls = p.sum(axis=-1, keepdims=True)
            os_ = jnp.dot(p.astype(BF16), vb, preferred_element_type=F32)
            o = os_ if o is None else o + os_
            l = ls if l is None else l + ls
        outs.append(o / l)
    return jnp.where(lane < NA_HEAD_DIM, outs[0], outs[1])


NA_KROWS = 2 * NA_GROUP
NA_KTOK = NA_KROWS * GRID_W


N_REL_R = 2 * NA_WIN_R - 1
N_REL_C = 2 * NA_WIN_C - 1


def _na_group_geometry(variant, rows):
    ng = rows // NA_GROUP
    g = {0: 0, 1: 1, 2: ng - 1}[variant]
    strip0 = int(np.clip(g * NA_GROUP - NA_WIN_R // 2, 0, rows - NA_KROWS))
    r0 = [int(np.clip(g * NA_GROUP + i - NA_WIN_R // 2, 0, rows - NA_WIN_R)) for i in range(NA_GROUP)]
    return strip0 - g * NA_GROUP, [r - g * NA_GROUP for r in r0]


def _bias_table_kernel(rpb_ref, o_ref, *, rows):
    layer, h = pl.program_id(0), pl.program_id(1)
    shape = (GRID_W, LANES)
    qc = lax.broadcasted_iota(jnp.int32, shape, 0)
    lane = lax.broadcasted_iota(jnp.int32, shape, 1)
    kc = lane % GRID_W
    win0 = jnp.clip(qc - NA_WIN_C // 2, 0, GRID_W - NA_WIN_C)
    in_win = (kc >= win0) & (kc < win0 + NA_WIN_C)
    rel = jnp.clip(kc - qc + NA_WIN_C - 1, 0, N_REL_C - 1)
    left = lane < GRID_W
    base = (layer * NA_HEADS + h) * (N_REL_R * N_REL_C)
    neg = jnp.full(shape, NEG, F32)
    vals = []
    for a in range(N_REL_R):
        acc = jnp.zeros(shape, F32)
        for b in range(N_REL_C):
            acc = jnp.where(rel == b, rpb_ref[base + a * N_REL_C + b], acc)
        vals.append(jnp.where(in_win, acc, neg))

    for variant in range(3):
        strip_rel, r0_rel = _na_group_geometry(variant, rows)

        def half(i, j):
            key_rel = strip_rel + j
            if r0_rel[i] <= key_rel < r0_rel[i] + NA_WIN_R:
                return vals[key_rel - i + NA_WIN_R - 1]
            return neg

        for i in range(NA_GROUP):
            for jp in range(NA_KROWS // 2):
                lft, rgt = half(i, 2 * jp), half(i, 2 * jp + 1)
                blk = lft if lft is rgt else jnp.where(left, lft, rgt)
                o_ref[0, variant, 0, i * GRID_W:(i + 1) * GRID_W, jp * LANES:(jp + 1) * LANES] = blk


def _bias_tables(na_rpb, rows):
    depth = na_rpb.shape[0]
    return pl.pallas_call(
        functools.partial(_bias_table_kernel, rows=rows),
        grid=(depth, NA_HEADS),
        in_specs=[pl.BlockSpec(memory_space=pltpu.SMEM)],
        out_specs=pl.BlockSpec((1, 3, 1, NA_GTOK, NA_KTOK), lambda l, h: (l, 0, h, 0, 0)),
        out_shape=jax.ShapeDtypeStruct((depth, 3, NA_HEADS, NA_GTOK, NA_KTOK), F32),
        compiler_params=_cparams(("arbitrary", "arbitrary")),
        name="na_bias",
    )(na_rpb.reshape(-1))


def _na_windows(variant, rows):
    strip_rel, r0_rel = _na_group_geometry(variant, rows)
    out = []
    for rp in range(NA_GROUP // 2):
        w0 = [r0_rel[i] - strip_rel for i in (2 * rp, 2 * rp + 1)]
        lo, hi = min(w0) * GRID_W, (max(w0) + NA_WIN_R) * GRID_W
        out.append((lo // LANES * LANES, -(-hi // LANES) * LANES))
    return out


def _na_pair(qp, k_w, v_w, k_x, v_x, bias_ref, pr, windows):
    lane = lax.broadcasted_iota(jnp.int32, qp.shape, 1)
    pair_rows = 2 * GRID_W
    outs = []
    for sub in range(2):
        in_head = (lane >= sub * NA_HEAD_DIM) & (lane < (sub + 1) * NA_HEAD_DIM)
        qm = jnp.where(in_head, qp, jnp.zeros_like(qp))
        s_w = lax.dot_general(qm, k_w, (((1,), (1,)), ((), ())), preferred_element_type=F32)
        s_c = lax.dot_general(qm, k_x, (((1,), (1,)), ((), ())), preferred_element_type=F32)
        p_rows, pc_rows, l_rows = [], [], []
        for rp, (c0, c1) in enumerate(windows):
            rs = slice(rp * pair_rows, (rp + 1) * pair_rows)
            sw = s_w[rs, c0:c1] + bias_ref[0, 0, 2 * pr + sub, rs, c0:c1]
            sc = s_c[rs]
            m = jnp.maximum(sw.max(axis=-1, keepdims=True), sc.max(axis=-1, keepdims=True))
            pw = jnp.exp(sw - m)
            pc = jnp.exp(sc - m)
            l_rows.append(pw.sum(axis=-1, keepdims=True) + pc.sum(axis=-1, keepdims=True))
            parts = []
            if c0 > 0:
                parts.append(jnp.zeros((pair_rows, c0), BF16))
            parts.append(pw.astype(BF16))
            if c1 < NA_KTOK:
                parts.append(jnp.zeros((pair_rows, NA_KTOK - c1), BF16))
            p_rows.append(jnp.concatenate(parts, axis=1) if len(parts) > 1 else parts[0])
            pc_rows.append(pc.astype(BF16))
        o = (jnp.dot(jnp.concatenate(p_rows, axis=0), v_w, preferred_element_type=F32)
             + jnp.dot(jnp.concatenate(pc_rows, axis=0), v_x, preferred_element_type=F32))
        outs.append(o / jnp.concatenate(l_rows, axis=0))
    return jnp.where(lane < NA_HEAD_DIM, outs[0], outs[1])


def _na_kernel(q_ref, kp_ref, kc_ref, kn_ref, vp_ref, vc_ref, vn_ref, kx_ref, vx_ref, bias_ref, o_ref,
               kcat, vcat, *, rows):
    g = pl.program_id(0)
    ng = pl.num_programs(0)
    kcat[0:NA_GTOK] = kp_ref[0]
    kcat[NA_GTOK:2 * NA_GTOK] = kc_ref[0]
    kcat[2 * NA_GTOK:3 * NA_GTOK] = kn_ref[0]
    vcat[0:NA_GTOK] = vp_ref[0]
    vcat[NA_GTOK:2 * NA_GTOK] = vc_ref[0]
    vcat[2 * NA_GTOK:3 * NA_GTOK] = vn_ref[0]
    strip0 = jnp.clip(g * NA_GROUP - NA_WIN_R // 2, 0, rows - NA_KROWS)
    start = pl.multiple_of((strip0 - g * NA_GROUP + NA_GROUP) * GRID_W, GRID_W)
    variant = jnp.where(g == 0, 0, jnp.where(g == ng - 1, 2, 1))
    for vi in range(3):
        @pl.when(variant == vi)
        def _(vi=vi):
            windows = _na_windows(vi, rows)
            for pr in range(NA_HEADS // 2):
                ps = slice(pr * LANES, (pr + 1) * LANES)
                k_w = kcat[pl.ds(start, NA_KTOK), ps]
                v_w = vcat[pl.ds(start, NA_KTOK), ps]
                o = _na_pair(q_ref[0, :, ps], k_w, v_w, kx_ref[0, :, ps], vx_ref[0, :, ps], bias_ref, pr, windows)
                o_ref[0, :, ps] = o.astype(BF16)


def _na(z, zc, bias, layer):
    b, s, _ = z.shape
    lc = zc.shape[1]
    rows = s // GRID_W
    ng = rows // NA_GROUP
    blk = (1, NA_GTOK, CB)

    def spec(col, off):
        return pl.BlockSpec(blk, lambda g, bi: (bi, jnp.clip(g + off, 0, ng - 1), col // CB))

    def bias_map(g, bi):
        return (layer, jnp.where(g == 0, 0, jnp.where(g == ng - 1, 2, 1)), 0, 0, 0)

    return pl.pallas_call(
        functools.partial(_na_kernel, rows=rows),
        grid=(ng, b),
        in_specs=[
            spec(Z_Q, 0),
            spec(Z_K, -1), spec(Z_K, 0), spec(Z_K, 1),
            spec(Z_V, -1), spec(Z_V, 0), spec(Z_V, 1),
            pl.BlockSpec((1, lc, CB), lambda g, bi: (bi, 0, Z_K // CB)),
            pl.BlockSpec((1, lc, CB), lambda g, bi: (bi, 0, Z_V // CB)),
            pl.BlockSpec((1, 1) + bias.shape[2:], bias_map, pipeline_mode=pl.Buffered(1)),
        ],
        out_specs=pl.BlockSpec(blk, lambda g, bi: (bi, g, 0)),
        out_shape=jax.ShapeDtypeStruct((b, s, NA_WIDTH), BF16),
        scratch_shapes=[pltpu.VMEM((3 * NA_GTOK, CB), BF16), pltpu.VMEM((3 * NA_GTOK, CB), BF16)],
        compiler_params=_cparams(("arbitrary", "arbitrary")),
        name="na_attn",
    )(z, z, z, z, z, z, z, zc, zc, bias)


def _ctx_na_kernel(q_ref, k_ref, v_ref, o_ref):
    for pr in range(NA_HEADS // 2):
        ps = slice(pr * LANES, (pr + 1) * LANES)
        o = _pair_softmax_pv(q_ref[0, :, ps], [k_ref[0, :, ps]], [v_ref[0, :, ps]], lambda sub, idx, s: s)
        o_ref[0, :, ps] = o.astype(BF16)


def _ctx_na(zc):
    b, lc, _ = zc.shape
    spec = lambda col: pl.BlockSpec((1, lc, CB), lambda bi: (bi, 0, col // CB))
    return pl.pallas_call(
        _ctx_na_kernel,
        grid=(b,),
        in_specs=[spec(Z_Q), spec(Z_K), spec(Z_V)],
        out_specs=pl.BlockSpec((1, lc, NA_WIDTH), lambda bi: (bi, 0, 0)),
        out_shape=jax.ShapeDtypeStruct((b, lc, NA_WIDTH), BF16),
        compiler_params=_cparams(("parallel",)),
        name="ctx_na",
    )(zc, zc, zc)


def _pool_tile(up_ref, u_ref, un_ref, w_ref, sc_ref, i, seq_len):
    tm = u_ref.shape[1]
    ext = jnp.concatenate([up_ref[0], u_ref[0], un_ref[0]], axis=0).astype(F32)
    n_ext = tm + 2 * POOL_HALO
    tg = i * tm - POOL_HALO + lax.broadcasted_iota(jnp.int32, (n_ext, 1), 0)
    ext = jnp.where((tg >= 0) & (tg < seq_len), ext, 0.0)
    t = i * tm + lax.broadcasted_iota(jnp.int32, (tm, 1), 0)

    def shifted(a, k):
        return pltpu.roll(a, (-k) % n_ext, 0)

    outs = []
    for gi, win in enumerate(POOL_WINDOWS):
        gs = slice(gi * POOL_GROUP, (gi + 1) * POOL_GROUP)
        a = ext[:, gs]
        wsum = a + shifted(a, -1)
        half = 1
        while 2 * half < win:
            wsum = shifted(wsum, -half) + shifted(wsum, half)
            half *= 2
        cnt = (jnp.minimum(t + win // 2, seq_len) - jnp.maximum(t - win // 2, 0)).astype(F32)
        d = wsum[POOL_HALO:POOL_HALO + tm] / cnt - a[POOL_HALO:POOL_HALO + tm]
        y = jnp.dot(d.astype(BF16), w_ref[gi], preferred_element_type=F32)
        outs.append((y * sc_ref[:, gs]).astype(BF16))
    return jnp.concatenate(outs, axis=1)


def _merge_kernel(ona_ref, up_ref, u_ref, un_ref, omla_ref, g0_ref, g1_ref, g2_ref, pw_ref, psc_ref, wb_ref, wo_ref,
                  x_ref, gate_ref, gain_ref, o_ref, *, seq_len):
    o_pool = _pool_tile(up_ref, u_ref, un_ref, pw_ref, psc_ref, pl.program_id(1), seq_len)
    m = None
    for br, g_ref, k in ((ona_ref[0], g0_ref, 0), (o_pool, g1_ref, 1), (omla_ref[0], g2_ref, 2)):
        proj = jnp.dot(br, wb_ref[k], preferred_element_type=F32)
        term = g_ref[0].astype(F32) * proj
        m = term if m is None else m + term
    y = jnp.dot(m.astype(BF16), wo_ref[...], preferred_element_type=F32)
    o_ref[0] = x_ref[0] + gate_ref[0] * _rms(y, gain_ref[...])


def _merge(o_na, z, o_mla, gz, pw, psc, wb, wo, x, gate, gain, layer, tm):
    b, l, d = x.shape
    tm = min(tm, l)
    hb = tm // POOL_HALO
    nhb = l // POOL_HALO
    ucol = Z_U // CB
    per_batch = gate.shape[0] > 1
    mod_map = (lambda bi, i: (bi, 0, 0)) if per_batch else (lambda bi, i: (0, 0, 0))
    br = pl.BlockSpec((1, tm, BRANCH_W), lambda bi, i: (bi, i, 0))
    gspec = lambda k: pl.BlockSpec((1, tm, d), lambda bi, i: (bi, i, k))
    return pl.pallas_call(
        functools.partial(_merge_kernel, seq_len=l),
        grid=(b, l // tm),
        in_specs=[
            br,
            pl.BlockSpec((1, POOL_HALO, CB), lambda bi, i: (bi, jnp.maximum(i * hb - 1, 0), ucol)),
            pl.BlockSpec((1, tm, CB), lambda bi, i: (bi, i, ucol)),
            pl.BlockSpec((1, POOL_HALO, CB), lambda bi, i: (bi, jnp.minimum((i + 1) * hb, nhb - 1), ucol)),
            br, gspec(0), gspec(1), gspec(2),
            pl.BlockSpec((None,) + pw.shape[1:], lambda bi, i: (layer, 0, 0, 0)),
            pl.BlockSpec((1, POOL_WIDTH), lambda bi, i: (0, 0)),
            pl.BlockSpec((None,) + wb.shape[1:], lambda bi, i: (layer, 0, 0, 0)),
            pl.BlockSpec((None,) + wo.shape[1:], lambda bi, i: (layer, 0, 0)),
            pl.BlockSpec((1, tm, d), lambda bi, i: (bi, i, 0)),
            pl.BlockSpec((1, 1, d), mod_map),
            pl.BlockSpec((1, d), lambda bi, i: (0, 0)),
        ],
        out_specs=pl.BlockSpec((1, tm, d), lambda bi, i: (bi, i, 0)),
        out_shape=jax.ShapeDtypeStruct((b, l, d), F32),
        compiler_params=_cparams(("parallel", "parallel")),
        name="merge",
    )(o_na, z, z, z, o_mla, gz, gz, gz, pw, psc, wb, wo, x, gate, gain)


def _gelu_tanh(x):
    return 0.5 * x * (1.0 + jnp.tanh(np.float32(np.sqrt(2.0 / np.pi)) * (x + np.float32(0.044715) * (x * x * x))))


def _ffn_kernel(xp_ref, x_ref, xn_ref, gain_ref, sc_ref, sh_ref, wa_ref, wb_ref, cwa_ref, cwb_ref, cba_ref,
                cbb_ref, wd_ref, gate_ref, gpost_ref, o_ref, h_scr, acc_scr):
    i = pl.program_id(1)
    j = pl.program_id(2)
    ni = pl.num_programs(1)
    nj = pl.num_programs(2)
    tm = x_ref.shape[1]
    n_ext = tm + 2 * CONV_HALO

    @pl.when(j == 0)
    def _():
        def norm_mod(xv):
            return _rms(xv, gain_ref[...]) * (1.0 + sc_ref[0]) + sh_ref[0]

        hp = jnp.where(i > 0, norm_mod(xp_ref[0]), 0.0)
        hn = jnp.where(i < ni - 1, norm_mod(xn_ref[0]), 0.0)
        h_scr[0:CONV_HALO] = hp.astype(BF16)
        h_scr[CONV_HALO:CONV_HALO + tm] = norm_mod(x_ref[0]).astype(BF16)
        h_scr[CONV_HALO + tm:n_ext] = hn.astype(BF16)
        acc_scr[...] = jnp.zeros(acc_scr.shape, F32)

    h = h_scr[...]

    def conv_half(w_ref, cw_ref, cb_ref):
        u = jnp.dot(h, w_ref[...], preferred_element_type=F32)
        prev = pltpu.roll(u, 1, 0)[CONV_HALO:CONV_HALO + tm]
        nxt = pltpu.roll(u, n_ext - 1, 0)[CONV_HALO:CONV_HALO + tm]
        cur = u[CONV_HALO:CONV_HALO + tm]
        return cb_ref[...] + prev * cw_ref[0:1] + cur * cw_ref[1:2] + nxt * cw_ref[2:3]

    a = conv_half(wa_ref, cwa_ref, cba_ref)
    bgate = conv_half(wb_ref, cwb_ref, cbb_ref)
    act = (_gelu_tanh(a) * bgate).astype(BF16)
    acc_scr[...] += jnp.dot(act, wd_ref[...], preferred_element_type=F32)

    @pl.when(j == nj - 1)
    def _():
        o_ref[0] = x_ref[0] + gate_ref[0] * _rms(acc_scr[...], gpost_ref[...])


def _ffn(x, gain, sc, sh, w_up, conv_w, conv_b, w_down, gate, gpost, layer, tm, tn):
    b, l, d = x.shape
    tm = min(tm, l)
    nch = D_FF // tn
    wmode = pl.Buffered(1) if nch == 1 else None
    hb = tm // CONV_HALO
    nhb = l // CONV_HALO
    per_batch = sc.shape[0] > 1
    mod_map = (lambda bi, i, j: (bi, 0, 0)) if per_batch else (lambda bi, i, j: (0, 0, 0))
    mod = pl.BlockSpec((1, 1, d), mod_map)
    vec = pl.BlockSpec((1, d), lambda bi, i, j: (0, 0))
    return pl.pallas_call(
        _ffn_kernel,
        grid=(b, l // tm, nch),
        in_specs=[
            pl.BlockSpec((1, CONV_HALO, d), lambda bi, i, j: (bi, jnp.maximum(i * hb - 1, 0), 0)),
            pl.BlockSpec((1, tm, d), lambda bi, i, j: (bi, i, 0)),
            pl.BlockSpec((1, CONV_HALO, d), lambda bi, i, j: (bi, jnp.minimum((i + 1) * hb, nhb - 1), 0)),
            vec, mod, mod,
            pl.BlockSpec((None, d, tn), lambda bi, i, j: (layer, 0, j), pipeline_mode=wmode),
            pl.BlockSpec((None, d, tn), lambda bi, i, j: (layer, 0, nch + j), pipeline_mode=wmode),
            pl.BlockSpec((None, 3, tn), lambda bi, i, j: (layer, 0, j)),
            pl.BlockSpec((None, 3, tn), lambda bi, i, j: (layer, 0, nch + j)),
            pl.BlockSpec((None, 1, tn), lambda bi, i, j: (layer, 0, j)),
            pl.BlockSpec((None, 1, tn), lambda bi, i, j: (layer, 0, nch + j)),
            pl.BlockSpec((None, tn, d), lambda bi, i, j: (layer, j, 0), pipeline_mode=wmode),
            mod, vec,
        ],
        out_specs=pl.BlockSpec((1, tm, d), lambda bi, i, j: (bi, i, 0)),
        out_shape=jax.ShapeDtypeStruct((b, l, d), F32),
        scratch_shapes=[pltpu.VMEM((tm + 2 * CONV_HALO, d), BF16), pltpu.VMEM((tm, d), F32)],
        compiler_params=_cparams(("parallel", "parallel", "arbitrary")),
        name="ffn",
    )(x, x, x, gain, sc, sh, w_up, w_up, conv_w, conv_w, conv_b, conv_b, w_down, gate, gpost)


_ROPE_PERM = np.concatenate([np.arange(8, 16), np.arange(0, 8), np.arange(24, 32), np.arange(16, 24)])


def _rope_tables(n_tok, rotate):
    f32 = np.float32
    ct = np.ones((n_tok, HEAD_PAD), f32)
    st = np.zeros((n_tok, HEAD_PAD), f32)
    if rotate:
        n_freq = MLA_ROPE // 4
        inv = (f32(ROPE_BASE) ** (-np.arange(n_freq, dtype=f32) / f32(n_freq))).astype(f32)
        t = np.arange(n_tok, dtype=np.int32)
        ang_r = ((t // GRID_W).astype(f32)[:, None] * inv[None, :]).astype(f32)
        ang_c = ((t % GRID_W).astype(f32)[:, None] * inv[None, :]).astype(f32)
        cr, sr, cc, sn = np.cos(ang_r), np.sin(ang_r), np.cos(ang_c), np.sin(ang_c)
        ct[:, MLA_NOPE:MLA_NOPE + MLA_ROPE] = np.concatenate([cr, cr, cc, cc], axis=1)
        st[:, MLA_NOPE:MLA_NOPE + MLA_ROPE] = np.concatenate([-sr, sr, -sn, sn], axis=1)
    return jnp.asarray(ct), jnp.asarray(st)


def _prep_all(w_in, w_uq, w_ukv):
    depth, d, _ = w_in.shape
    w_in = w_in.astype(BF16)
    o = 0
    parts = []
    for s in (NA_WIDTH, NA_WIDTH, NA_WIDTH, POOL_WIDTH, MLA_Q_RANK, MLA_KV_RANK, MLA_ROPE, N_BRANCH * D_MODEL):
        parts.append(w_in[..., o:o + s])
        o += s
    wq, wk, wv, wu, wcq, wckv, wkr, wg = parts
    z64 = jnp.zeros((depth, d, MLA_NOPE), BF16)
    z32 = jnp.zeros((depth, d, HEAD_PAD - MLA_NOPE - MLA_ROPE), BF16)
    w_z = jnp.concatenate([wq * jnp.asarray(NA_HEAD_DIM ** -0.5, BF16), wk, wv, wu, wcq, wckv,
                           z64, wkr, z32, z64, wkr[..., _ROPE_PERM], z32], axis=-1)

    uq = w_uq.reshape(depth, MLA_Q_RANK, MLA_HEADS, MLA_NOPE + MLA_ROPE)
    qz64 = jnp.zeros((depth, MLA_Q_RANK, MLA_HEADS, MLA_NOPE), F32)
    qz32 = jnp.zeros((depth, MLA_Q_RANK, MLA_HEADS, HEAD_PAD - MLA_NOPE - MLA_ROPE), F32)
    wq_pad = jnp.concatenate([uq, qz32], axis=-1)
    wq_perm = jnp.concatenate([qz64, uq[..., MLA_NOPE:][..., _ROPE_PERM], qz32], axis=-1)
    assert MLA_NOPE + MLA_V == HEAD_PAD

    flat = lambda w, k: w.reshape(depth, k, MLA_PAD_W).astype(BF16)
    return dict(
        w_z=w_z.astype(BF16),
        w_g=wg.astype(BF16),
        wq_pad=flat(wq_pad, MLA_Q_RANK),
        wq_perm=flat(wq_perm, MLA_Q_RANK),
        w_kv=w_ukv.astype(BF16),
    )


TM_PROJ = 1024
TN_PROJ = 1536
TM_UP = 1024
TQ_FLASH = 1024
TK_FLASH = 2048
TM_MERGE = 1024
TM_FFN = 512
TN_FFN = D_FF


def kernel(x, c, ctx, c_ctx, w_ada, b_ada, norm_pre1, norm_post1, norm_pre2, norm_post2, w_in, na_rpb, pool_w,
           pool_scale, mla_q_norm, w_uq, mla_kv_norm, w_ukv, w_branch, w_o, w_up, conv_w, conv_b, w_down):
    b, s, d = x.shape
    lc = ctx.shape[1]
    depth = w_ada.shape[0]
    rows = s // GRID_W
    assert d == D_MODEL and s % NA_KTOK == 0 and lc % POOL_HALO == 0

    n_mod = -(-(b + 1) // 8) * 8
    cvec = jnp.concatenate([c, c_ctx[None, :], jnp.zeros((n_mod - b - 1, d), F32)], axis=0)
    mod = _ada(cvec, w_ada, b_ada).reshape(depth, n_mod, 6, d)

    ct_l, st_l = _rope_tables(s, True)
    ct_c, st_c = _rope_tables(b * lc, False)

    p = _prep_all(w_in, w_uq, w_ukv)
    na_bias = _bias_tables(na_rpb, rows)
    pw, wb, wo = pool_w.astype(BF16), w_branch.astype(BF16), w_o.astype(BF16)
    wup, wdn = w_up.astype(BF16), w_down.astype(BF16)
    cb = conv_b[:, None, :]

    xc = ctx
    for l in range(depth):
        last = l == depth - 1
        lat = [mod[l, :b, k][:, None, :] for k in range(6)]
        cx = [mod[l, b:b + 1, k][:, None, :] for k in range(6)]
        row = lambda v: v.reshape(1, -1)
        g_pre1, g_post1, g_pre2, g_post2 = row(norm_pre1[l]), row(norm_post1[l]), row(norm_pre2[l]), row(norm_post2[l])
        qg, kvg = row(mla_q_norm[l]), row(mla_kv_norm[l])
        psc = row(pool_scale[l])

        z, gz = _in_proj(x, g_pre1, lat[1], lat[0], p["w_z"], p["w_g"], l, TM_PROJ, TN_PROJ)
        flat = lambda a: a.reshape(1, b * lc, a.shape[-1])
        unflat = lambda a: a.reshape(b, lc, a.shape[-1])
        zc_f, gzc_f = _in_proj(flat(xc), g_pre1, cx[1], cx[0], p["w_z"], p["w_g"], l, TM_PROJ, TN_PROJ)
        zc, gzc = unflat(zc_f), unflat(gzc_f)
        mla_w = (p["wq_pad"], p["wq_perm"], p["w_kv"], l)
        q_l, k_l, v_l = _mla_up(z, qg, kvg, ct_l, st_l, *mla_w, TM_UP)
        q_c, k_c, v_c = [unflat(a) for a in _mla_up(zc_f, qg, kvg, ct_c, st_c, *mla_w, TM_UP)]

        o_na = _na(z, zc, na_bias, l)
        o_mla = _flash(q_l, k_c, v_c, k_l, v_l, TQ_FLASH, TK_FLASH)
        x = _merge(o_na, z, o_mla, gz, pw, psc, wb, wo, x, lat[2], g_post1, l, TM_MERGE)
        x = _ffn(x, g_pre2, lat[4], lat[3], wup, conv_w, cb, wdn, lat[5], g_post2, l, TM_FFN, TN_FFN)

        if not last:
            oc_na = _ctx_na(zc)
            oc_mla = _flash(q_c, k_c, v_c, None, None, TQ_FLASH, TK_FLASH)
            xc = _merge(oc_na, zc, oc_mla, gzc, pw, psc, wb, wo, xc, cx[2], g_post1, l, TM_MERGE)
            xc = _ffn(xc, g_pre2, cx[4], cx[3], wup, conv_w, cb, wdn, cx[5], g_post2, l, TM_FFN, TN_FFN)
    return x
```

```python
import functools

import jax
import jax.numpy as jnp
import numpy as np
from jax import lax
from jax.experimental import pallas as pl
from jax.experimental.pallas import tpu as pltpu

F32 = jnp.float32
BF16 = jnp.bfloat16

D_MODEL = 1024
GRID_W = 64
EPS = 1e-6
NA_HEADS = 8
NA_HEAD_DIM = 64
NA_WIDTH = NA_HEADS * NA_HEAD_DIM
NA_WIN_R = 8
NA_WIN_C = 16
POOL_WINDOWS = (2, 4, 8, 16)
POOL_GROUP = 128
POOL_WIDTH = POOL_GROUP * len(POOL_WINDOWS)
POOL_HALO = 16
MLA_HEADS = 8
MLA_NOPE = 64
MLA_ROPE = 32
MLA_V = 64
MLA_Q_RANK = 512
MLA_KV_RANK = 256
ROPE_BASE = 10000.0
N_BRANCH = 3
BRANCH_W = 512
D_FF = 2816
CONV_HALO = 8

LANES = 128
HEAD_PAD = LANES
MLA_PAD_W = MLA_HEADS * HEAD_PAD
MLA_SUM_LANE = 0

Z_Q = 0
Z_K = Z_Q + NA_WIDTH
Z_V = Z_K + NA_WIDTH
Z_U = Z_V + NA_WIDTH
Z_CQ = Z_U + POOL_WIDTH
Z_KV = Z_CQ + MLA_Q_RANK
Z_COLS = Z_KV + 512
G_COLS = N_BRANCH * D_MODEL
assert G_COLS == Z_COLS
CB = 512

NEG = -1e30
VMEM_LIMIT = 56 * 1024 * 1024


def _cparams(sem):
    return pltpu.CompilerParams(dimension_semantics=sem, vmem_limit_bytes=VMEM_LIMIT)


def _rms(x, gain):
    return x * lax.rsqrt(jnp.mean(x * x, axis=-1, keepdims=True) + EPS) * gain


def _ada_kernel(c_ref, w_ref, b_ref, o_ref):
    c = c_ref[...]
    s = c * jax.nn.sigmoid(c)
    o_ref[0] = jnp.dot(s.astype(BF16), w_ref[0].astype(BF16), preferred_element_type=F32) + b_ref[0]


def _ada(cvec, w_ada, b_ada):
    depth, d, n = w_ada.shape
    rows = cvec.shape[0]
    tn = 1536
    return pl.pallas_call(
        _ada_kernel,
        grid=(depth, n // tn),
        in_specs=[
            pl.BlockSpec((rows, d), lambda l, j: (0, 0)),
            pl.BlockSpec((1, d, tn), lambda l, j: (l, 0, j)),
            pl.BlockSpec((1, 1, tn), lambda l, j: (l, 0, j)),
        ],
        out_specs=pl.BlockSpec((1, rows, tn), lambda l, j: (l, 0, j)),
        out_shape=jax.ShapeDtypeStruct((depth, rows, n), F32),
        compiler_params=_cparams(("arbitrary", "arbitrary")),
        name="ada",
    )(cvec, w_ada, b_ada.reshape(depth, 1, n))


def _in_proj_kernel(x_ref, gain_ref, sc_ref, sh_ref, wz_ref, wg_ref, z_ref, g_ref, h_scr):
    j = pl.program_id(2)

    @pl.when(j == 0)
    def _():
        h = _rms(x_ref[0], gain_ref[...]) * (1.0 + sc_ref[0]) + sh_ref[0]
        h_scr[...] = h.astype(BF16)

    h = h_scr[...]
    g_ref[0] = jax.nn.sigmoid(jnp.dot(h, wg_ref[...], preferred_element_type=F32)).astype(BF16)
    z_ref[0] = jnp.dot(h, wz_ref[...], preferred_element_type=F32).astype(BF16)


def _in_proj(x, gain, sc, sh, wz, wg, layer, tm, tn):
    b, l, d = x.shape
    tm = min(tm, l)
    per_batch = sc.shape[0] > 1
    mod_map = (lambda bi, i, j: (bi, 0, 0)) if per_batch else (lambda bi, i, j: (0, 0, 0))
    wspec = pl.BlockSpec((None, d, tn), lambda bi, i, j: (layer, 0, j))
    ospec = pl.BlockSpec((1, tm, tn), lambda bi, i, j: (bi, i, j))
    oshape = jax.ShapeDtypeStruct((b, l, Z_COLS), BF16)
    return pl.pallas_call(
        _in_proj_kernel,
        grid=(b, l // tm, Z_COLS // tn),
        in_specs=[
            pl.BlockSpec((1, tm, d), lambda bi, i, j: (bi, i, 0)),
            pl.BlockSpec((1, d), lambda bi, i, j: (0, 0)),
            pl.BlockSpec((1, 1, d), mod_map),
            pl.BlockSpec((1, 1, d), mod_map),
            wspec, wspec,
        ],
        out_specs=[ospec, ospec],
        out_shape=[oshape, oshape],
        scratch_shapes=[pltpu.VMEM((tm, d), BF16)],
        compiler_params=_cparams(("parallel", "parallel", "arbitrary")),
        name="in_proj",
    )(x, gain, sc, sh, wz, wg)


def _mla_up_kernel(cq_ref, kv_ref, qg_ref, kvg_ref, ct_ref, st_ref, wq_ref, wqp_ref, wkv_ref,
                   q_out, k_out, v_out, *, scale):
    cqn = _rms(cq_ref[0].astype(F32), qg_ref[...]).astype(BF16)
    kvb = kv_ref[0].astype(F32)
    kvn = _rms(kvb[:, :MLA_KV_RANK], kvg_ref[...]).astype(BF16)
    ct = ct_ref[...]
    st = st_ref[...]
    kr = kvb[:, MLA_KV_RANK:MLA_KV_RANK + LANES] * ct + kvb[:, MLA_KV_RANK + LANES:] * st
    qm = jnp.dot(cqn, wq_ref[...], preferred_element_type=F32)
    qp = jnp.dot(cqn, wqp_ref[...], preferred_element_type=F32)
    kv = jnp.dot(kvn, wkv_ref[...], preferred_element_type=F32)
    cts = ct * scale
    sts = st * scale
    lane = lax.broadcasted_iota(jnp.int32, kr.shape, 1)
    is_k = lane < MLA_NOPE
    row_sum_col = jnp.where(lane == MLA_SUM_LANE, 1.0, 0.0)
    for h in range(MLA_HEADS):
        hs = slice(h * HEAD_PAD, (h + 1) * HEAD_PAD)
        q_out[0, :, hs] = (qm[:, hs] * cts + qp[:, hs] * sts).astype(BF16)
        k_out[0, :, hs] = jnp.where(is_k, kv[:, hs], kr).astype(BF16)
        v_out[0, :, hs] = jnp.where(is_k, row_sum_col, kv[:, hs]).astype(BF16)


def _mla_up(z, qg, kvg, ct, st, wq, wqp, wkv, layer, tm):
    b, l, _ = z.shape
    tm = min(tm, l)
    full = lambda shape: pl.BlockSpec(shape, lambda bi, i: (0,) * len(shape))
    wspec = lambda k: pl.BlockSpec((None, k, MLA_PAD_W), lambda bi, i: (layer, 0, 0))
    out = jax.ShapeDtypeStruct((b, l, MLA_PAD_W), BF16)
    ospec = pl.BlockSpec((1, tm, MLA_PAD_W), lambda bi, i: (bi, i, 0))
    return pl.pallas_call(
        functools.partial(_mla_up_kernel, scale=float((MLA_NOPE + MLA_ROPE) ** -0.5 * np.log2(np.e))),
        grid=(b, l // tm),
        in_specs=[
            pl.BlockSpec((1, tm, CB), lambda bi, i: (bi, i, Z_CQ // CB)),
            pl.BlockSpec((1, tm, CB), lambda bi, i: (bi, i, Z_KV // CB)),
            full((1, MLA_Q_RANK)),
            full((1, MLA_KV_RANK)),
            pl.BlockSpec((tm, LANES), lambda bi, i: (i, 0)),
            pl.BlockSpec((tm, LANES), lambda bi, i: (i, 0)),
            wspec(MLA_Q_RANK), wspec(MLA_Q_RANK), wspec(MLA_KV_RANK),
        ],
        out_specs=[ospec, ospec, ospec],
        out_shape=[out, out, out],
        compiler_params=_cparams(("parallel", "parallel")),
        name="mla_up",
    )(z, z, qg, kvg, ct, st, wq, wqp, wkv)


def _flash_kernel(*refs, has_latent):
    if has_latent:
        q_ref, kc_ref, vc_ref, k_ref, v_ref, o_ref, m_scr, acc_scr = refs
    else:
        q_ref, kc_ref, vc_ref, o_ref, m_scr, acc_scr = refs
        k_ref = v_ref = None
    kk = pl.program_id(2)
    nk = pl.num_programs(2)

    def attend(h, kblk, vblk):
        hs = slice(h * HEAD_PAD, (h + 1) * HEAD_PAD)
        q = q_ref[0, :, hs]
        s = lax.dot_general(q, kblk[0, :, hs], (((1,), (1,)), ((), ())), preferred_element_type=F32)
        m_prev = m_scr[h]
        m_new = jnp.maximum(m_prev, jnp.max(s, axis=-1, keepdims=True))
        alpha = jnp.exp2(m_prev - m_new)
        p = jnp.exp2(s - jnp.concatenate([m_new] * (s.shape[1] // LANES), axis=1))
        acc_scr[h] = alpha * acc_scr[h] + jnp.dot(p.astype(BF16), vblk[0, :, hs], preferred_element_type=F32)
        m_scr[h] = m_new

    @pl.when(kk == 0)
    def _():
        m_scr[...] = jnp.full(m_scr.shape, NEG, F32)
        acc_scr[...] = jnp.zeros(acc_scr.shape, F32)
        for h in range(MLA_HEADS):
            attend(h, kc_ref, vc_ref)

    if has_latent:
        for h in range(MLA_HEADS):
            attend(h, k_ref, v_ref)

    @pl.when(kk == nk - 1)
    def _():
        for h in range(MLA_HEADS):
            acc = acc_scr[h]
            o_ref[0, :, h * MLA_V:(h + 1) * MLA_V] = (
                acc[:, HEAD_PAD - MLA_V:] / acc[:, MLA_SUM_LANE:MLA_SUM_LANE + 1]).astype(BF16)


def _flash(q, kc, vc, k, v, tq, tk):
    b, lq, _ = q.shape
    lc = kc.shape[1]
    tq = min(tq, lq)
    has_latent = k is not None
    in_specs = [
        pl.BlockSpec((1, tq, MLA_PAD_W), lambda bi, i, kk: (bi, i, 0)),
        pl.BlockSpec((1, lc, MLA_PAD_W), lambda bi, i, kk: (bi, 0, 0)),
        pl.BlockSpec((1, lc, MLA_PAD_W), lambda bi, i, kk: (bi, 0, 0)),
    ]
    args = [q, kc, vc]
    nk = 1
    if has_latent:
        tk = min(tk, k.shape[1])
        nk = k.shape[1] // tk
        in_specs += [pl.BlockSpec((1, tk, MLA_PAD_W), lambda bi, i, kk: (bi, kk, 0))] * 2
        args += [k, v]
    return pl.pallas_call(
        functools.partial(_flash_kernel, has_latent=has_latent),
        grid=(b, lq // tq, nk),
        in_specs=in_specs,
        out_specs=pl.BlockSpec((1, tq, MLA_HEADS * MLA_V), lambda bi, i, kk: (bi, i, 0)),
        out_shape=jax.ShapeDtypeStruct((b, lq, MLA_HEADS * MLA_V), BF16),
        scratch_shapes=[pltpu.VMEM((MLA_HEADS, tq, LANES), F32), pltpu.VMEM((MLA_HEADS, tq, LANES), F32)],
        compiler_params=_cparams(("parallel", "parallel", "arbitrary")),
        name="mla_flash" if has_latent else "mla_ctx",
    )(*args)


NA_GROUP = 8
NA_GTOK = NA_GROUP * GRID_W


def _pair_softmax_pv(qp, k_list, v_list, bias_fn):
    lane = lax.broadcasted_iota(jnp.int32, qp.shape, 1)
    outs = []
    for sub in range(2):
        in_head = (lane >= sub * NA_HEAD_DIM) & (lane < (sub + 1) * NA_HEAD_DIM)
        qm = jnp.where(in_head, qp, jnp.zeros_like(qp))
        s_list = []
        for idx, kb in enumerate(k_list):
            s = lax.dot_general(qm, kb, (((1,), (1,)), ((), ())), preferred_element_type=F32)
            s_list.append(bias_fn(sub, idx, s))
        m = s_list[0].max(axis=-1, keepdims=True)
        for s in s_list[1:]:
            m = jnp.maximum(m, s.max(axis=-1, keepdims=True))
        o = None
        l = None
        for s, vb in zip(s_list, v_list):
            p = jnp.exp(s - m)
            ls = p.sum(axis=-1, keepdims=True)
            os_ = jnp.dot(p.astype(BF16), vb, preferred_element_type=F32)
            o = os_ if o is None else o + os_
            l = ls if l is None else l + ls
        outs.append(o / l)
    return jnp.where(lane < NA_HEAD_DIM, outs[0], outs[1])


NA_KROWS = 2 * NA_GROUP
NA_KTOK = NA_KROWS * GRID_W


N_REL_R = 2 * NA_WIN_R - 1
N_REL_C = 2 * NA_WIN_C - 1


def _na_group_geometry(variant, rows):
    ng = rows // NA_GROUP
    g = {0: 0, 1: 1, 2: ng - 1}[variant]
    strip0 = int(np.clip(g * NA_GROUP - NA_WIN_R // 2, 0, rows - NA_KROWS))
    r0 = [int(np.clip(g * NA_GROUP + i - NA_WIN_R // 2, 0, rows - NA_WIN_R)) for i in range(NA_GROUP)]
    return strip0 - g * NA_GROUP, [r - g * NA_GROUP for r in r0]


def _bias_table_kernel(rpb_ref, o_ref, *, rows):
    layer, h = pl.program_id(0), pl.program_id(1)
    shape = (GRID_W, LANES)
    qc = lax.broadcasted_iota(jnp.int32, shape, 0)
    lane = lax.broadcasted_iota(jnp.int32, shape, 1)
    kc = lane % GRID_W
    win0 = jnp.clip(qc - NA_WIN_C // 2, 0, GRID_W - NA_WIN_C)
    in_win = (kc >= win0) & (kc < win0 + NA_WIN_C)
    rel = jnp.clip(kc - qc + NA_WIN_C - 1, 0, N_REL_C - 1)
    left = lane < GRID_W
    base = (layer * NA_HEADS + h) * (N_REL_R * N_REL_C)
    neg = jnp.full(shape, NEG, F32)
    vals = []
    for a in range(N_REL_R):
        acc = jnp.zeros(shape, F32)
        for b in range(N_REL_C):
            acc = jnp.where(rel == b, rpb_ref[base + a * N_REL_C + b], acc)
        vals.append(jnp.where(in_win, acc, neg))

    for variant in range(3):
        strip_rel, r0_rel = _na_group_geometry(variant, rows)

        def half(i, j):
            key_rel = strip_rel + j
            if r0_rel[i] <= key_rel < r0_rel[i] + NA_WIN_R:
                return vals[key_rel - i + NA_WIN_R - 1]
            return neg

        for i in range(NA_GROUP):
            for jp in range(NA_KROWS // 2):
                lft, rgt = half(i, 2 * jp), half(i, 2 * jp + 1)
                blk = lft if lft is rgt else jnp.where(left, lft, rgt)
                o_ref[0, variant, 0, i * GRID_W:(i + 1) * GRID_W, jp * LANES:(jp + 1) * LANES] = blk


def _bias_tables(na_rpb, rows):
    depth = na_rpb.shape[0]
    return pl.pallas_call(
        functools.partial(_bias_table_kernel, rows=rows),
        grid=(depth, NA_HEADS),
        in_specs=[pl.BlockSpec(memory_space=pltpu.SMEM)],
        out_specs=pl.BlockSpec((1, 3, 1, NA_GTOK, NA_KTOK), lambda l, h: (l, 0, h, 0, 0)),
        out_shape=jax.ShapeDtypeStruct((depth, 3, NA_HEADS, NA_GTOK, NA_KTOK), F32),
        compiler_params=_cparams(("arbitrary", "arbitrary")),
        name="na_bias",
    )(na_rpb.reshape(-1))


def _na_windows(variant, rows):
    strip_rel, r0_rel = _na_group_geometry(variant, rows)
    out = []
    for rp in range(NA_GROUP // 2):
        w0 = [r0_rel[i] - strip_rel for i in (2 * rp, 2 * rp + 1)]
        lo, hi = min(w0) * GRID_W, (max(w0) + NA_WIN_R) * GRID_W
        out.append((lo // LANES * LANES, -(-hi // LANES) * LANES))
    return out


def _na_pair(qp, k_w, v_w, k_x, v_x, bias_ref, pr, windows):
    lane = lax.broadcasted_iota(jnp.int32, qp.shape, 1)
    pair_rows = 2 * GRID_W
    outs = []
    for sub in range(2):
        in_head = (lane >= sub * NA_HEAD_DIM) & (lane < (sub + 1) * NA_HEAD_DIM)
        qm = jnp.where(in_head, qp, jnp.zeros_like(qp))
        s_w = lax.dot_general(qm, k_w, (((1,), (1,)), ((), ())), preferred_element_type=F32)
        s_c = lax.dot_general(qm, k_x, (((1,), (1,)), ((), ())), preferred_element_type=F32)
        p_rows, pc_rows, l_rows = [], [], []
        for rp, (c0, c1) in enumerate(windows):
            rs = slice(rp * pair_rows, (rp + 1) * pair_rows)
            sw = s_w[rs, c0:c1] + bias_ref[0, 0, 2 * pr + sub, rs, c0:c1]
            sc = s_c[rs]
            m = jnp.maximum(sw.max(axis=-1, keepdims=True), sc.max(axis=-1, keepdims=True))
            pw = jnp.exp(sw - m)
            pc = jnp.exp(sc - m)
            l_rows.append(pw.sum(axis=-1, keepdims=True) + pc.sum(axis=-1, keepdims=True))
            parts = []
            if c0 > 0:
                parts.append(jnp.zeros((pair_rows, c0), BF16))
            parts.append(pw.astype(BF16))
            if c1 < NA_KTOK:
                parts.append(jnp.zeros((pair_rows, NA_KTOK - c1), BF16))
            p_rows.append(jnp.concatenate(parts, axis=1) if len(parts) > 1 else parts[0])
            pc_rows.append(pc.astype(BF16))
        o = (jnp.dot(jnp.concatenate(p_rows, axis=0), v_w, preferred_element_type=F32)
             + jnp.dot(jnp.concatenate(pc_rows, axis=0), v_x, preferred_element_type=F32))
        outs.append(o / jnp.concatenate(l_rows, axis=0))
    return jnp.where(lane < NA_HEAD_DIM, outs[0], outs[1])


def _na_kernel(q_ref, kp_ref, kc_ref, kn_ref, vp_ref, vc_ref, vn_ref, kx_ref, vx_ref, bias_ref, o_ref, *, rows):
    g = pl.program_id(0)
    ng = pl.num_programs(0)
    variant = jnp.where(g == 0, 0, jnp.where(g == ng - 1, 2, 1))
    for vi in range(3):
        @pl.when(variant == vi)
        def _(vi=vi):
            windows = _na_windows(vi, rows)
            strip_rel, _ = _na_group_geometry(vi, rows)
            start = (strip_rel + NA_GROUP) * GRID_W

            def strip(refs, ps):
                pieces = []
                for blk, ref in enumerate(refs):
                    lo = max(start, blk * NA_GTOK) - blk * NA_GTOK
                    hi = min(start + NA_KTOK, (blk + 1) * NA_GTOK) - blk * NA_GTOK
                    if hi > lo:
                        pieces.append(ref[0, lo:hi, ps])
                return pieces[0] if len(pieces) == 1 else jnp.concatenate(pieces, axis=0)

            for pr in range(NA_HEADS // 2):
                ps = slice(pr * LANES, (pr + 1) * LANES)
                k_w = strip((kp_ref, kc_ref, kn_ref), ps)
                v_w = strip((vp_ref, vc_ref, vn_ref), ps)
                o = _na_pair(q_ref[0, :, ps], k_w, v_w, kx_ref[0, :, ps], vx_ref[0, :, ps], bias_ref, pr, windows)
                o_ref[0, :, ps] = o.astype(BF16)


def _na(z, zc, bias, layer):
    b, s, _ = z.shape
    lc = zc.shape[1]
    rows = s // GRID_W
    ng = rows // NA_GROUP
    blk = (1, NA_GTOK, CB)

    def spec(col, off):
        return pl.BlockSpec(blk, lambda g, bi: (bi, jnp.clip(g + off, 0, ng - 1), col // CB))

    def bias_map(g, bi):
        return (layer, jnp.where(g == 0, 0, jnp.where(g == ng - 1, 2, 1)), 0, 0, 0)

    return pl.pallas_call(
        functools.partial(_na_kernel, rows=rows),
        grid=(ng, b),
        in_specs=[
            spec(Z_Q, 0),
            spec(Z_K, -1), spec(Z_K, 0), spec(Z_K, 1),
            spec(Z_V, -1), spec(Z_V, 0), spec(Z_V, 1),
            pl.BlockSpec((1, lc, CB), lambda g, bi: (bi, 0, Z_K // CB)),
            pl.BlockSpec((1, lc, CB), lambda g, bi: (bi, 0, Z_V // CB)),
            pl.BlockSpec((1, 1) + bias.shape[2:], bias_map, pipeline_mode=pl.Buffered(1)),
        ],
        out_specs=pl.BlockSpec(blk, lambda g, bi: (bi, g, 0)),
        out_shape=jax.ShapeDtypeStruct((b, s, NA_WIDTH), BF16),
        compiler_params=_cparams(("arbitrary", "arbitrary")),
        name="na_attn",
    )(z, z, z, z, z, z, z, zc, zc, bias)


def _ctx_na_kernel(q_ref, k_ref, v_ref, o_ref):
    for pr in range(NA_HEADS // 2):
        ps = slice(pr * LANES, (pr + 1) * LANES)
        o = _pair_softmax_pv(q_ref[0, :, ps], [k_ref[0, :, ps]], [v_ref[0, :, ps]], lambda sub, idx, s: s)
        o_ref[0, :, ps] = o.astype(BF16)


def _ctx_na(zc):
    b, lc, _ = zc.shape
    spec = lambda col: pl.BlockSpec((1, lc, CB), lambda bi: (bi, 0, col // CB))
    return pl.pallas_call(
        _ctx_na_kernel,
        grid=(b,),
        in_specs=[spec(Z_Q), spec(Z_K), spec(Z_V)],
        out_specs=pl.BlockSpec((1, lc, NA_WIDTH), lambda bi: (bi, 0, 0)),
        out_shape=jax.ShapeDtypeStruct((b, lc, NA_WIDTH), BF16),
        compiler_params=_cparams(("parallel",)),
        name="ctx_na",
    )(zc, zc, zc)


def _pool_tile(up_ref, u_ref, un_ref, w_ref, sc_ref, i, seq_len):
    tm = u_ref.shape[1]
    ext = jnp.concatenate([up_ref[0], u_ref[0], un_ref[0]], axis=0).astype(F32)
    n_ext = tm + 2 * POOL_HALO
    tg = i * tm - POOL_HALO + lax.broadcasted_iota(jnp.int32, (n_ext, 1), 0)
    ext = jnp.where((tg >= 0) & (tg < seq_len), ext, 0.0)
    t = i * tm + lax.broadcasted_iota(jnp.int32, (tm, 1), 0)

    def shifted(a, k):
        return pltpu.roll(a, (-k) % n_ext, 0)

    outs = []
    for gi, win in enumerate(POOL_WINDOWS):
        gs = slice(gi * POOL_GROUP, (gi + 1) * POOL_GROUP)
        a = ext[:, gs]
        wsum = a + shifted(a, -1)
        half = 1
        while 2 * half < win:
            wsum = shifted(wsum, -half) + shifted(wsum, half)
            half *= 2
        cnt = (jnp.minimum(t + win // 2, seq_len) - jnp.maximum(t - win // 2, 0)).astype(F32)
        d = wsum[POOL_HALO:POOL_HALO + tm] / cnt - a[POOL_HALO:POOL_HALO + tm]
        y = jnp.dot(d.astype(BF16), w_ref[gi], preferred_element_type=F32)
        outs.append((y * sc_ref[:, gs]).astype(BF16))
    return jnp.concatenate(outs, axis=1)


def _merge_kernel(ona_ref, up_ref, u_ref, un_ref, omla_ref, g0_ref, g1_ref, g2_ref, pw_ref, psc_ref, wb_ref, wo_ref,
                  x_ref, gate_ref, gain_ref, o_ref, *, seq_len):
    o_pool = _pool_tile(up_ref, u_ref, un_ref, pw_ref, psc_ref, pl.program_id(1), seq_len)
    m = None
    for br, g_ref, k in ((ona_ref[0], g0_ref, 0), (o_pool, g1_ref, 1), (omla_ref[0], g2_ref, 2)):
        proj = jnp.dot(br, wb_ref[k], preferred_element_type=F32)
        term = g_ref[0].astype(F32) * proj
        m = term if m is None else m + term
    y = jnp.dot(m.astype(BF16), wo_ref[...], preferred_element_type=F32)
    o_ref[0] = x_ref[0] + gate_ref[0] * _rms(y, gain_ref[...])


def _merge(o_na, z, o_mla, gz, pw, psc, wb, wo, x, gate, gain, layer, tm):
    b, l, d = x.shape
    tm = min(tm, l)
    hb = tm // POOL_HALO
    nhb = l // POOL_HALO
    ucol = Z_U // CB
    per_batch = gate.shape[0] > 1
    mod_map = (lambda bi, i: (bi, 0, 0)) if per_batch else (lambda bi, i: (0, 0, 0))
    br = pl.BlockSpec((1, tm, BRANCH_W), lambda bi, i: (bi, i, 0))
    gspec = lambda k: pl.BlockSpec((1, tm, d), lambda bi, i: (bi, i, k))
    return pl.pallas_call(
        functools.partial(_merge_kernel, seq_len=l),
        grid=(b, l // tm),
        in_specs=[
            br,
            pl.BlockSpec((1, POOL_HALO, CB), lambda bi, i: (bi, jnp.maximum(i * hb - 1, 0), ucol)),
            pl.BlockSpec((1, tm, CB), lambda bi, i: (bi, i, ucol)),
            pl.BlockSpec((1, POOL_HALO, CB), lambda bi, i: (bi, jnp.minimum((i + 1) * hb, nhb - 1), ucol)),
            br, gspec(0), gspec(1), gspec(2),
            pl.BlockSpec((None,) + pw.shape[1:], lambda bi, i: (layer, 0, 0, 0)),
            pl.BlockSpec((1, POOL_WIDTH), lambda bi, i: (0, 0)),
            pl.BlockSpec((None,) + wb.shape[1:], lambda bi, i: (layer, 0, 0, 0)),
            pl.BlockSpec((None,) + wo.shape[1:], lambda bi, i: (layer, 0, 0)),
            pl.BlockSpec((1, tm, d), lambda bi, i: (bi, i, 0)),
            pl.BlockSpec((1, 1, d), mod_map),
            pl.BlockSpec((1, d), lambda bi, i: (0, 0)),
        ],
        out_specs=pl.BlockSpec((1, tm, d), lambda bi, i: (bi, i, 0)),
        out_shape=jax.ShapeDtypeStruct((b, l, d), F32),
        compiler_params=_cparams(("parallel", "parallel")),
        name="merge",
    )(o_na, z, z, z, o_mla, gz, gz, gz, pw, psc, wb, wo, x, gate, gain)


def _gelu_tanh(x):
    return 0.5 * x * (1.0 + jnp.tanh(np.float32(np.sqrt(2.0 / np.pi)) * (x + np.float32(0.044715) * (x * x * x))))


def _ffn_kernel(xp_ref, x_ref, xn_ref, gain_ref, sc_ref, sh_ref, wa_ref, wb_ref, cwa_ref, cwb_ref, cba_ref,
                cbb_ref, wd_ref, gate_ref, gpost_ref, o_ref, h_scr, acc_scr):
    i = pl.program_id(1)
    j = pl.program_id(2)
    ni = pl.num_programs(1)
    nj = pl.num_programs(2)
    tm = x_ref.shape[1]
    n_ext = tm + 2 * CONV_HALO

    @pl.when(j == 0)
    def _():
        def norm_mod(xv):
            return _rms(xv, gain_ref[...]) * (1.0 + sc_ref[0]) + sh_ref[0]

        hp = jnp.where(i > 0, norm_mod(xp_ref[0]), 0.0)
        hn = jnp.where(i < ni - 1, norm_mod(xn_ref[0]), 0.0)
        h_scr[0:CONV_HALO] = hp.astype(BF16)
        h_scr[CONV_HALO:CONV_HALO + tm] = norm_mod(x_ref[0]).astype(BF16)
        h_scr[CONV_HALO + tm:n_ext] = hn.astype(BF16)
        acc_scr[...] = jnp.zeros(acc_scr.shape, F32)

    h = h_scr[...]

    def conv_half(w_ref, cw_ref, cb_ref):
        u = jnp.dot(h, w_ref[...], preferred_element_type=F32)
        prev = pltpu.roll(u, 1, 0)[CONV_HALO:CONV_HALO + tm]
        nxt = pltpu.roll(u, n_ext - 1, 0)[CONV_HALO:CONV_HALO + tm]
        cur = u[CONV_HALO:CONV_HALO + tm]
        return cb_ref[...] + prev * cw_ref[0:1] + cur * cw_ref[1:2] + nxt * cw_ref[2:3]

    a = conv_half(wa_ref, cwa_ref, cba_ref)
    bgate = conv_half(wb_ref, cwb_ref, cbb_ref)
    act = (_gelu_tanh(a) * bgate).astype(BF16)
    acc_scr[...] += jnp.dot(act, wd_ref[...], preferred_element_type=F32)

    @pl.when(j == nj - 1)
    def _():
        o_ref[0] = x_ref[0] + gate_ref[0] * _rms(acc_scr[...], gpost_ref[...])


def _ffn(x, gain, sc, sh, w_up, conv_w, conv_b, w_down, gate, gpost, layer, tm, tn):
    b, l, d = x.shape
    tm = min(tm, l)
    nch = D_FF // tn
    wmode = pl.Buffered(1) if nch == 1 else None
    hb = tm // CONV_HALO
    nhb = l // CONV_HALO
    per_batch = sc.shape[0] > 1
    mod_map = (lambda bi, i, j: (bi, 0, 0)) if per_batch else (lambda bi, i, j: (0, 0, 0))
    mod = pl.BlockSpec((1, 1, d), mod_map)
    vec = pl.BlockSpec((1, d), lambda bi, i, j: (0, 0))
    return pl.pallas_call(
        _ffn_kernel,
        grid=(b, l // tm, nch),
        in_specs=[
            pl.BlockSpec((1, CONV_HALO, d), lambda bi, i, j: (bi, jnp.maximum(i * hb - 1, 0), 0)),
            pl.BlockSpec((1, tm, d), lambda bi, i, j: (bi, i, 0)),
            pl.BlockSpec((1, CONV_HALO, d), lambda bi, i, j: (bi, jnp.minimum((i + 1) * hb, nhb - 1), 0)),
            vec, mod, mod,
            pl.BlockSpec((None, d, tn), lambda bi, i, j: (layer, 0, j), pipeline_mode=wmode),
            pl.BlockSpec((None, d, tn), lambda bi, i, j: (layer, 0, nch + j), pipeline_mode=wmode),
            pl.BlockSpec((None, 3, tn), lambda bi, i, j: (layer, 0, j)),
            pl.BlockSpec((None, 3, tn), lambda bi, i, j: (layer, 0, nch + j)),
            pl.BlockSpec((None, 1, tn), lambda bi, i, j: (layer, 0, j)),
            pl.BlockSpec((None, 1, tn), lambda bi, i, j: (layer, 0, nch + j)),
            pl.BlockSpec((None, tn, d), lambda bi, i, j: (layer, j, 0), pipeline_mode=wmode),
            mod, vec,
        ],
        out_specs=pl.BlockSpec((1, tm, d), lambda bi, i, j: (bi, i, 0)),
        out_shape=jax.ShapeDtypeStruct((b, l, d), F32),
        scratch_shapes=[pltpu.VMEM((tm + 2 * CONV_HALO, d), BF16), pltpu.VMEM((tm, d), F32)],
        compiler_params=_cparams(("parallel", "parallel", "arbitrary")),
        name="ffn",
    )(x, x, x, gain, sc, sh, w_up, w_up, conv_w, conv_w, conv_b, conv_b, w_down, gate, gpost)


_ROPE_PERM = np.concatenate([np.arange(8, 16), np.arange(0, 8), np.arange(24, 32), np.arange(16, 24)])


def _rope_tables(n_tok, rotate):
    f32 = np.float32
    ct = np.ones((n_tok, HEAD_PAD), f32)
    st = np.zeros((n_tok, HEAD_PAD), f32)
    if rotate:
        n_freq = MLA_ROPE // 4
        inv = (f32(ROPE_BASE) ** (-np.arange(n_freq, dtype=f32) / f32(n_freq))).astype(f32)
        t = np.arange(n_tok, dtype=np.int32)
        ang_r = ((t // GRID_W).astype(f32)[:, None] * inv[None, :]).astype(f32)
        ang_c = ((t % GRID_W).astype(f32)[:, None] * inv[None, :]).astype(f32)
        cr, sr, cc, sn = np.cos(ang_r), np.sin(ang_r), np.cos(ang_c), np.sin(ang_c)
        ct[:, MLA_NOPE:MLA_NOPE + MLA_ROPE] = np.concatenate([cr, cr, cc, cc], axis=1)
        st[:, MLA_NOPE:MLA_NOPE + MLA_ROPE] = np.concatenate([-sr, sr, -sn, sn], axis=1)
    return jnp.asarray(ct), jnp.asarray(st)


def _prep_all(w_in, w_uq, w_ukv):
    depth, d, _ = w_in.shape
    w_in = w_in.astype(BF16)
    o = 0
    parts = []
    for s in (NA_WIDTH, NA_WIDTH, NA_WIDTH, POOL_WIDTH, MLA_Q_RANK, MLA_KV_RANK, MLA_ROPE, N_BRANCH * D_MODEL):
        parts.append(w_in[..., o:o + s])
        o += s
    wq, wk, wv, wu, wcq, wckv, wkr, wg = parts
    z64 = jnp.zeros((depth, d, MLA_NOPE), BF16)
    z32 = jnp.zeros((depth, d, HEAD_PAD - MLA_NOPE - MLA_ROPE), BF16)
    w_z = jnp.concatenate([wq * jnp.asarray(NA_HEAD_DIM ** -0.5, BF16), wk, wv, wu, wcq, wckv,
                           z64, wkr, z32, z64, wkr[..., _ROPE_PERM], z32], axis=-1)

    uq = w_uq.reshape(depth, MLA_Q_RANK, MLA_HEADS, MLA_NOPE + MLA_ROPE)
    qz64 = jnp.zeros((depth, MLA_Q_RANK, MLA_HEADS, MLA_NOPE), F32)
    qz32 = jnp.zeros((depth, MLA_Q_RANK, MLA_HEADS, HEAD_PAD - MLA_NOPE - MLA_ROPE), F32)
    wq_pad = jnp.concatenate([uq, qz32], axis=-1)
    wq_perm = jnp.concatenate([qz64, uq[..., MLA_NOPE:][..., _ROPE_PERM], qz32], axis=-1)
    assert MLA_NOPE + MLA_V == HEAD_PAD

    flat = lambda w, k: w.reshape(depth, k, MLA_PAD_W).astype(BF16)
    return dict(
        w_z=w_z.astype(BF16),
        w_g=wg.astype(BF16),
        wq_pad=flat(wq_pad, MLA_Q_RANK),
        wq_perm=flat(wq_perm, MLA_Q_RANK),
        w_kv=w_ukv.astype(BF16),
    )


TM_PROJ = 1024
TN_PROJ = 1536
TM_UP = 1024
TQ_FLASH = 1024
TK_FLASH = 2048
TM_MERGE = 1024
TM_FFN = 512
TN_FFN = D_FF


def kernel(x, c, ctx, c_ctx, w_ada, b_ada, norm_pre1, norm_post1, norm_pre2, norm_post2, w_in, na_rpb, pool_w,
           pool_scale, mla_q_norm, w_uq, mla_kv_norm, w_ukv, w_branch, w_o, w_up, conv_w, conv_b, w_down):
    b, s, d = x.shape
    lc = ctx.shape[1]
    depth = w_ada.shape[0]
    rows = s // GRID_W
    assert d == D_MODEL and s % NA_KTOK == 0 and lc % POOL_HALO == 0

    n_mod = -(-(b + 1) // 8) * 8
    cvec = jnp.concatenate([c, c_ctx[None, :], jnp.zeros((n_mod - b - 1, d), F32)], axis=0)
    mod = _ada(cvec, w_ada, b_ada).reshape(depth, n_mod, 6, d)

    ct_l, st_l = _rope_tables(s, True)
    ct_c, st_c = _rope_tables(b * lc, False)

    p = _prep_all(w_in, w_uq, w_ukv)
    na_bias = _bias_tables(na_rpb, rows)
    pw, wb, wo = pool_w.astype(BF16), w_branch.astype(BF16), w_o.astype(BF16)
    wup, wdn = w_up.astype(BF16), w_down.astype(BF16)
    cb = conv_b[:, None, :]

    xc = ctx
    for l in range(depth):
        last = l == depth - 1
        lat = [mod[l, :b, k][:, None, :] for k in range(6)]
        cx = [mod[l, b:b + 1, k][:, None, :] for k in range(6)]
        row = lambda v: v.reshape(1, -1)
        g_pre1, g_post1, g_pre2, g_post2 = row(norm_pre1[l]), row(norm_post1[l]), row(norm_pre2[l]), row(norm_post2[l])
        qg, kvg = row(mla_q_norm[l]), row(mla_kv_norm[l])
        psc = row(pool_scale[l])

        z, gz = _in_proj(x, g_pre1, lat[1], lat[0], p["w_z"], p["w_g"], l, TM_PROJ, TN_PROJ)
        flat = lambda a: a.reshape(1, b * lc, a.shape[-1])
        unflat = lambda a: a.reshape(b, lc, a.shape[-1])
        zc_f, gzc_f = _in_proj(flat(xc), g_pre1, cx[1], cx[0], p["w_z"], p["w_g"], l, TM_PROJ, TN_PROJ)
        zc, gzc = unflat(zc_f), unflat(gzc_f)
        mla_w = (p["wq_pad"], p["wq_perm"], p["w_kv"], l)
        q_l, k_l, v_l = _mla_up(z, qg, kvg, ct_l, st_l, *mla_w, TM_UP)
        q_c, k_c, v_c = [unflat(a) for a in _mla_up(zc_f, qg, kvg, ct_c, st_c, *mla_w, TM_UP)]

        o_na = _na(z, zc, na_bias, l)
        o_mla = _flash(q_l, k_c, v_c, k_l, v_l, TQ_FLASH, TK_FLASH)
        x = _merge(o_na, z, o_mla, gz, pw, psc, wb, wo, x, lat[2], g_post1, l, TM_MERGE)
        x = _ffn(x, g_pre2, lat[4], lat[3], wup, conv_w, cb, wdn, lat[5], g_post2, l, TM_FFN, TN_FFN)

        if not last:
            oc_na = _ctx_na(zc)
            oc_mla = _flash(q_c, k_c, v_c, None, None, TQ_FLASH, TK_FLASH)
            xc = _merge(oc_na, zc, oc_mla, gzc, pw, psc, wb, wo, xc, cx[2], g_post1, l, TM_MERGE)
            xc = _ffn(xc, g_pre2, cx[4], cx[3], wup, conv_w, cb, wdn, cx[5], g_post2, l, TM_FFN, TN_FFN)
    return x
```

```python
import functools

import jax
import jax.numpy as jnp
import numpy as np
from jax import lax
from jax.experimental import pallas as pl
from jax.experimental.pallas import tpu as pltpu

F32 = jnp.float32
BF16 = jnp.bfloat16

D_MODEL = 1024
GRID_W = 64
EPS = 1e-6
NA_HEADS = 8
NA_HEAD_DIM = 64
NA_WIDTH = NA_HEADS * NA_HEAD_DIM
NA_WIN_R = 8
NA_WIN_C = 16
POOL_WINDOWS = (2, 4, 8, 16)
POOL_GROUP = 128
POOL_WIDTH = POOL_GROUP * len(POOL_WINDOWS)
POOL_HALO = 16
MLA_HEADS = 8
MLA_NOPE = 64
MLA_ROPE = 32
MLA_V = 64
MLA_Q_RANK = 512
MLA_KV_RANK = 256
ROPE_BASE = 10000.0
N_BRANCH = 3
BRANCH_W = 512
D_FF = 2816
CONV_HALO = 8

LANES = 128
HEAD_PAD = LANES
MLA_PAD_W = MLA_HEADS * HEAD_PAD
MLA_SUM_LANE = 0

Z_Q = 0
Z_K = Z_Q + NA_WIDTH
Z_V = Z_K + NA_WIDTH
Z_U = Z_V + NA_WIDTH
Z_CQ = Z_U + POOL_WIDTH
Z_KV = Z_CQ + MLA_Q_RANK
Z_COLS = Z_KV + 512
G_COLS = N_BRANCH * D_MODEL
assert G_COLS == Z_COLS
CB = 512

NEG = -1e30
VMEM_LIMIT = 56 * 1024 * 1024


def _cparams(sem):
    return pltpu.CompilerParams(dimension_semantics=sem, vmem_limit_bytes=VMEM_LIMIT)


def _rms(x, gain):
    return x * lax.rsqrt(jnp.mean(x * x, axis=-1, keepdims=True) + EPS) * gain


def _ada_kernel(c_ref, w_ref, b_ref, o_ref):
    c = c_ref[...]
    s = c * jax.nn.sigmoid(c)
    o_ref[0] = jnp.dot(s.astype(BF16), w_ref[0].astype(BF16), preferred_element_type=F32) + b_ref[0]


def _ada(cvec, w_ada, b_ada):
    depth, d, n = w_ada.shape
    rows = cvec.shape[0]
    tn = 1536
    return pl.pallas_call(
        _ada_kernel,
        grid=(depth, n // tn),
        in_specs=[
            pl.BlockSpec((rows, d), lambda l, j: (0, 0)),
            pl.BlockSpec((1, d, tn), lambda l, j: (l, 0, j)),
            pl.BlockSpec((1, 1, tn), lambda l, j: (l, 0, j)),
        ],
        out_specs=pl.BlockSpec((1, rows, tn), lambda l, j: (l, 0, j)),
        out_shape=jax.ShapeDtypeStruct((depth, rows, n), F32),
        compiler_params=_cparams(("arbitrary", "arbitrary")),
        name="ada",
    )(cvec, w_ada, b_ada.reshape(depth, 1, n))


def _in_proj_kernel(x_ref, gain_ref, sc_ref, sh_ref, wz_ref, wg_ref, z_ref, g_ref, h_scr):
    j = pl.program_id(2)

    @pl.when(j == 0)
    def _():
        h = _rms(x_ref[0], gain_ref[...]) * (1.0 + sc_ref[0]) + sh_ref[0]
        h_scr[...] = h.astype(BF16)

    h = h_scr[...]
    g_ref[0] = jax.nn.sigmoid(jnp.dot(h, wg_ref[...], preferred_element_type=F32)).astype(BF16)
    z_ref[0] = jnp.dot(h, wz_ref[...], preferred_element_type=F32).astype(BF16)


def _in_proj(x, gain, sc, sh, wz, wg, layer, tm, tn):
    b, l, d = x.shape
    tm = min(tm, l)
    per_batch = sc.shape[0] > 1
    mod_map = (lambda bi, i, j: (bi, 0, 0)) if per_batch else (lambda bi, i, j: (0, 0, 0))
    wspec = pl.BlockSpec((None, d, tn), lambda bi, i, j: (layer, 0, j))
    ospec = pl.BlockSpec((1, tm, tn), lambda bi, i, j: (bi, i, j))
    oshape = jax.ShapeDtypeStruct((b, l, Z_COLS), BF16)
    return pl.pallas_call(
        _in_proj_kernel,
        grid=(b, l // tm, Z_COLS // tn),
        in_specs=[
            pl.BlockSpec((1, tm, d), lambda bi, i, j: (bi, i, 0)),
            pl.BlockSpec((1, d), lambda bi, i, j: (0, 0)),
            pl.BlockSpec((1, 1, d), mod_map),
            pl.BlockSpec((1, 1, d), mod_map),
            wspec, wspec,
        ],
        out_specs=[ospec, ospec],
        out_shape=[oshape, oshape],
        scratch_shapes=[pltpu.VMEM((tm, d), BF16)],
        compiler_params=_cparams(("parallel", "parallel", "arbitrary")),
        name="in_proj",
    )(x, gain, sc, sh, wz, wg)


def _mla_up_kernel(cq_ref, kv_ref, qg_ref, kvg_ref, ct_ref, st_ref, wq_ref, wqp_ref, wkv_ref,
                   q_out, k_out, v_out, *, scale):
    cqn = _rms(cq_ref[0].astype(F32), qg_ref[...]).astype(BF16)
    kvb = kv_ref[0].astype(F32)
    kvn = _rms(kvb[:, :MLA_KV_RANK], kvg_ref[...]).astype(BF16)
    ct = ct_ref[...]
    st = st_ref[...]
    kr = kvb[:, MLA_KV_RANK:MLA_KV_RANK + LANES] * ct + kvb[:, MLA_KV_RANK + LANES:] * st
    qm = jnp.dot(cqn, wq_ref[...], preferred_element_type=F32)
    qp = jnp.dot(cqn, wqp_ref[...], preferred_element_type=F32)
    kv = jnp.dot(kvn, wkv_ref[...], preferred_element_type=F32)
    cts = ct * scale
    sts = st * scale
    for h in range(MLA_HEADS):
        hs = slice(h * HEAD_PAD, (h + 1) * HEAD_PAD)
        q_out[0, :, hs] = (qm[:, hs] * cts + qp[:, hs] * sts).astype(BF16)
    k_out[0] = kv.astype(BF16)
    v_out[0] = kr.astype(BF16)


def _mla_up(z, qg, kvg, ct, st, wq, wqp, wkv, layer, tm):
    b, l, _ = z.shape
    tm = min(tm, l)
    full = lambda shape: pl.BlockSpec(shape, lambda bi, i: (0,) * len(shape))
    wspec = lambda k: pl.BlockSpec((None, k, MLA_PAD_W), lambda bi, i: (layer, 0, 0))
    out = jax.ShapeDtypeStruct((b, l, MLA_PAD_W), BF16)
    ospec = pl.BlockSpec((1, tm, MLA_PAD_W), lambda bi, i: (bi, i, 0))
    return pl.pallas_call(
        functools.partial(_mla_up_kernel, scale=float((MLA_NOPE + MLA_ROPE) ** -0.5 * np.log2(np.e))),
        grid=(b, l // tm),
        in_specs=[
            pl.BlockSpec((1, tm, CB), lambda bi, i: (bi, i, Z_CQ // CB)),
            pl.BlockSpec((1, tm, CB), lambda bi, i: (bi, i, Z_KV // CB)),
            full((1, MLA_Q_RANK)),
            full((1, MLA_KV_RANK)),
            pl.BlockSpec((tm, LANES), lambda bi, i: (i, 0)),
            pl.BlockSpec((tm, LANES), lambda bi, i: (i, 0)),
            wspec(MLA_Q_RANK), wspec(MLA_Q_RANK), wspec(MLA_KV_RANK),
        ],
        out_specs=[ospec, ospec, pl.BlockSpec((1, tm, LANES), lambda bi, i: (bi, i, 0))],
        out_shape=[out, out, jax.ShapeDtypeStruct((b, l, LANES), BF16)],
        compiler_params=_cparams(("parallel", "parallel")),
        name="mla_up",
    )(z, z, qg, kvg, ct, st, wq, wqp, wkv)


def _flash_kernel(*refs, has_latent):
    if has_latent:
        q_ref, kc_ref, vc_ref, k_ref, v_ref, o_ref, m_scr, acc_scr = refs
    else:
        q_ref, kc_ref, vc_ref, o_ref, m_scr, acc_scr = refs
        k_ref = v_ref = None
    kk = pl.program_id(2)
    nk = pl.num_programs(2)

    def attend(h, kblk, vblk):
        hs = slice(h * HEAD_PAD, (h + 1) * HEAD_PAD)
        q = q_ref[0, :, hs]
        kv_h = kblk[0, :, hs]
        lane = lax.broadcasted_iota(jnp.int32, kv_h.shape, 1)
        is_k = lane < MLA_NOPE
        k_h = jnp.where(is_k, kv_h, vblk[0])
        v_h = jnp.where(is_k, jnp.where(lane == MLA_SUM_LANE, 1.0, 0.0).astype(BF16), kv_h)
        s = lax.dot_general(q, k_h, (((1,), (1,)), ((), ())), preferred_element_type=F32)
        m_prev = m_scr[h]
        m_new = jnp.maximum(m_prev, jnp.max(s, axis=-1, keepdims=True))
        alpha = jnp.exp2(m_prev - m_new)
        p = jnp.exp2(s - jnp.concatenate([m_new] * (s.shape[1] // LANES), axis=1))
        acc_scr[h] = alpha * acc_scr[h] + jnp.dot(p.astype(BF16), v_h, preferred_element_type=F32)
        m_scr[h] = m_new

    @pl.when(kk == 0)
    def _():
        m_scr[...] = jnp.full(m_scr.shape, NEG, F32)
        acc_scr[...] = jnp.zeros(acc_scr.shape, F32)
        for h in range(MLA_HEADS):
            attend(h, kc_ref, vc_ref)

    if has_latent:
        for h in range(MLA_HEADS):
            attend(h, k_ref, v_ref)

    @pl.when(kk == nk - 1)
    def _():
        for h in range(MLA_HEADS):
            acc = acc_scr[h]
            o_ref[0, :, h * MLA_V:(h + 1) * MLA_V] = (
                acc[:, HEAD_PAD - MLA_V:] / acc[:, MLA_SUM_LANE:MLA_SUM_LANE + 1]).astype(BF16)


def _flash(q, kc, vc, k, v, tq, tk):
    b, lq, _ = q.shape
    lc = kc.shape[1]
    tq = min(tq, lq)
    has_latent = k is not None
    in_specs = [
        pl.BlockSpec((1, tq, MLA_PAD_W), lambda bi, i, kk: (bi, i, 0)),
        pl.BlockSpec((1, lc, MLA_PAD_W), lambda bi, i, kk: (bi, 0, 0)),
        pl.BlockSpec((1, lc, LANES), lambda bi, i, kk: (bi, 0, 0)),
    ]
    args = [q, kc, vc]
    nk = 1
    if has_latent:
        tk = min(tk, k.shape[1])
        nk = k.shape[1] // tk
        in_specs += [pl.BlockSpec((1, tk, MLA_PAD_W), lambda bi, i, kk: (bi, kk, 0)),
                     pl.BlockSpec((1, tk, LANES), lambda bi, i, kk: (bi, kk, 0))]
        args += [k, v]
    return pl.pallas_call(
        functools.partial(_flash_kernel, has_latent=has_latent),
        grid=(b, lq // tq, nk),
        in_specs=in_specs,
        out_specs=pl.BlockSpec((1, tq, MLA_HEADS * MLA_V), lambda bi, i, kk: (bi, i, 0)),
        out_shape=jax.ShapeDtypeStruct((b, lq, MLA_HEADS * MLA_V), BF16),
        scratch_shapes=[pltpu.VMEM((MLA_HEADS, tq, LANES), F32), pltpu.VMEM((MLA_HEADS, tq, LANES), F32)],
        compiler_params=_cparams(("parallel", "parallel", "arbitrary")),
        name="mla_flash" if has_latent else "mla_ctx",
    )(*args)


NA_GROUP = 8
NA_GTOK = NA_GROUP * GRID_W


def _pair_softmax_pv(qp, k_list, v_list, bias_fn):
    lane = lax.broadcasted_iota(jnp.int32, qp.shape, 1)
    outs = []
    for sub in range(2):
        in_head = (lane >= sub * NA_HEAD_DIM) & (lane < (sub + 1) * NA_HEAD_DIM)
        qm = jnp.where(in_head, qp, jnp.zeros_like(qp))
        s_list = []
        for idx, kb in enumerate(k_list):
            s = lax.dot_general(qm, kb, (((1,), (1,)), ((), ())), preferred_element_type=F32)
            s_list.append(bias_fn(sub, idx, s))
        m = s_list[0].max(axis=-1, keepdims=True)
        for s in s_list[1:]:
            m = jnp.maximum(m, s.max(axis=-1, keepdims=True))
        o = None
        l = None
        for s, vb in zip(s_list, v_list):
            p = jnp.exp(s - m)
            ls = p.sum(axis=-1, keepdims=True)
            os_ = jnp.dot(p.astype(BF16), vb, preferred_element_type=F32)
            o = os_ if o is None else o + os_
            l = ls if l is None else l + ls
        outs.append(o / l)
    return jnp.where(lane < NA_HEAD_DIM, outs[0], outs[1])


NA_KROWS = 2 * NA_GROUP
NA_KTOK = NA_KROWS * GRID_W


N_REL_R = 2 * NA_WIN_R - 1
N_REL_C = 2 * NA_WIN_C - 1


def _na_group_geometry(variant, rows):
    ng = rows // NA_GROUP
    g = {0: 0, 1: 1, 2: ng - 1}[variant]
    strip0 = int(np.clip(g * NA_GROUP - NA_WIN_R // 2, 0, rows - NA_KROWS))
    r0 = [int(np.clip(g * NA_GROUP + i - NA_WIN_R // 2, 0, rows - NA_WIN_R)) for i in range(NA_GROUP)]
    return strip0 - g * NA_GROUP, [r - g * NA_GROUP for r in r0]


def _bias_table_kernel(rpb_ref, o_ref, *, rows):
    layer, h = pl.program_id(0), pl.program_id(1)
    shape = (GRID_W, LANES)
    qc = lax.broadcasted_iota(jnp.int32, shape, 0)
    lane = lax.broadcasted_iota(jnp.int32, shape, 1)
    kc = lane % GRID_W
    win0 = jnp.clip(qc - NA_WIN_C // 2, 0, GRID_W - NA_WIN_C)
    in_win = (kc >= win0) & (kc < win0 + NA_WIN_C)
    rel = jnp.clip(kc - qc + NA_WIN_C - 1, 0, N_REL_C - 1)
    left = lane < GRID_W
    base = (layer * NA_HEADS + h) * (N_REL_R * N_REL_C)
    neg = jnp.full(shape, NEG, F32)
    vals = []
    for a in range(N_REL_R):
        acc = jnp.zeros(shape, F32)
        for b in range(N_REL_C):
            acc = jnp.where(rel == b, rpb_ref[base + a * N_REL_C + b], acc)
        vals.append(jnp.where(in_win, acc, neg))

    for variant in range(3):
        strip_rel, r0_rel = _na_group_geometry(variant, rows)

        def half(i, j):
            key_rel = strip_rel + j
            if r0_rel[i] <= key_rel < r0_rel[i] + NA_WIN_R:
                return vals[key_rel - i + NA_WIN_R - 1]
            return neg

        for i in range(NA_GROUP):
            for jp in range(NA_KROWS // 2):
                lft, rgt = half(i, 2 * jp), half(i, 2 * jp + 1)
                blk = lft if lft is rgt else jnp.where(left, lft, rgt)
                o_ref[0, variant, 0, i * GRID_W:(i + 1) * GRID_W, jp * LANES:(jp + 1) * LANES] = blk


def _bias_tables(na_rpb, rows):
    depth = na_rpb.shape[0]
    return pl.pallas_call(
        functools.partial(_bias_table_kernel, rows=rows),
        grid=(depth, NA_HEADS),
        in_specs=[pl.BlockSpec(memory_space=pltpu.SMEM)],
        out_specs=pl.BlockSpec((1, 3, 1, NA_GTOK, NA_KTOK), lambda l, h: (l, 0, h, 0, 0)),
        out_shape=jax.ShapeDtypeStruct((depth, 3, NA_HEADS, NA_GTOK, NA_KTOK), F32),
        compiler_params=_cparams(("arbitrary", "arbitrary")),
        name="na_bias",
    )(na_rpb.reshape(-1))


def _na_windows(variant, rows):
    strip_rel, r0_rel = _na_group_geometry(variant, rows)
    out = []
    for rp in range(NA_GROUP // 2):
        w0 = [r0_rel[i] - strip_rel for i in (2 * rp, 2 * rp + 1)]
        lo, hi = min(w0) * GRID_W, (max(w0) + NA_WIN_R) * GRID_W
        out.append((lo // LANES * LANES, -(-hi // LANES) * LANES))
    return out


def _na_pair(qp, k_w, v_w, k_x, v_x, bias_ref, pr, windows):
    lane = lax.broadcasted_iota(jnp.int32, qp.shape, 1)
    pair_rows = 2 * GRID_W
    outs = []
    for sub in range(2):
        in_head = (lane >= sub * NA_HEAD_DIM) & (lane < (sub + 1) * NA_HEAD_DIM)
        qm = jnp.where(in_head, qp, jnp.zeros_like(qp))
        s_w = lax.dot_general(qm, k_w, (((1,), (1,)), ((), ())), preferred_element_type=F32)
        s_c = lax.dot_general(qm, k_x, (((1,), (1,)), ((), ())), preferred_element_type=F32)
        p_rows, pc_rows, l_rows = [], [], []
        for rp, (c0, c1) in enumerate(windows):
            rs = slice(rp * pair_rows, (rp + 1) * pair_rows)
            sw = s_w[rs, c0:c1] + bias_ref[0, 0, 2 * pr + sub, rs, c0:c1]
            sc = s_c[rs]
            m = jnp.maximum(sw.max(axis=-1, keepdims=True), sc.max(axis=-1, keepdims=True))
            pw = jnp.exp(sw - m)
            pc = jnp.exp(sc - m)
            l_rows.append(pw.sum(axis=-1, keepdims=True) + pc.sum(axis=-1, keepdims=True))
            parts = []
            if c0 > 0:
                parts.append(jnp.zeros((pair_rows, c0), BF16))
            parts.append(pw.astype(BF16))
            if c1 < NA_KTOK:
                parts.append(jnp.zeros((pair_rows, NA_KTOK - c1), BF16))
            p_rows.append(jnp.concatenate(parts, axis=1) if len(parts) > 1 else parts[0])
            pc_rows.append(pc.astype(BF16))
        o = (jnp.dot(jnp.concatenate(p_rows, axis=0), v_w, preferred_element_type=F32)
             + jnp.dot(jnp.concatenate(pc_rows, axis=0), v_x, preferred_element_type=F32))
        outs.append(o / jnp.concatenate(l_rows, axis=0))
    return jnp.where(lane < NA_HEAD_DIM, outs[0], outs[1])


def _na_kernel(q_ref, kp_ref, kc_ref, kn_ref, vp_ref, vc_ref, vn_ref, kx_ref, vx_ref, bias_ref, o_ref, *, rows):
    g = pl.program_id(0)
    ng = pl.num_programs(0)
    variant = jnp.where(g == 0, 0, jnp.where(g == ng - 1, 2, 1))
    for vi in range(3):
        @pl.when(variant == vi)
        def _(vi=vi):
            windows = _na_windows(vi, rows)
            strip_rel, _ = _na_group_geometry(vi, rows)
            start = (strip_rel + NA_GROUP) * GRID_W

            def strip(refs, ps):
                pieces = []
                for blk, ref in enumerate(refs):
                    lo = max(start, blk * NA_GTOK) - blk * NA_GTOK
                    hi = min(start + NA_KTOK, (blk + 1) * NA_GTOK) - blk * NA_GTOK
                    if hi > lo:
                        pieces.append(ref[0, lo:hi, ps])
                return pieces[0] if len(pieces) == 1 else jnp.concatenate(pieces, axis=0)

            for pr in range(NA_HEADS // 2):
                ps = slice(pr * LANES, (pr + 1) * LANES)
                k_w = strip((kp_ref, kc_ref, kn_ref), ps)
                v_w = strip((vp_ref, vc_ref, vn_ref), ps)
                o = _na_pair(q_ref[0, :, ps], k_w, v_w, kx_ref[0, :, ps], vx_ref[0, :, ps], bias_ref, pr, windows)
                o_ref[0, :, ps] = o.astype(BF16)


def _na(z, zc, bias, layer):
    b, s, _ = z.shape
    lc = zc.shape[1]
    rows = s // GRID_W
    ng = rows // NA_GROUP
    blk = (1, NA_GTOK, CB)

    def spec(col, off):
        return pl.BlockSpec(blk, lambda g, bi: (bi, jnp.clip(g + off, 0, ng - 1), col // CB))

    def bias_map(g, bi):
        return (layer, jnp.where(g == 0, 0, jnp.where(g == ng - 1, 2, 1)), 0, 0, 0)

    return pl.pallas_call(
        functools.partial(_na_kernel, rows=rows),
        grid=(ng, b),
        in_specs=[
            spec(Z_Q, 0),
            spec(Z_K, -1), spec(Z_K, 0), spec(Z_K, 1),
            spec(Z_V, -1), spec(Z_V, 0), spec(Z_V, 1),
            pl.BlockSpec((1, lc, CB), lambda g, bi: (bi, 0, Z_K // CB)),
            pl.BlockSpec((1, lc, CB), lambda g, bi: (bi, 0, Z_V // CB)),
            pl.BlockSpec((1, 1) + bias.shape[2:], bias_map, pipeline_mode=pl.Buffered(1)),
        ],
        out_specs=pl.BlockSpec(blk, lambda g, bi: (bi, g, 0)),
        out_shape=jax.ShapeDtypeStruct((b, s, NA_WIDTH), BF16),
        compiler_params=_cparams(("arbitrary", "arbitrary")),
        name="na_attn",
    )(z, z, z, z, z, z, z, zc, zc, bias)


def _ctx_na_kernel(q_ref, k_ref, v_ref, o_ref):
    for pr in range(NA_HEADS // 2):
        ps = slice(pr * LANES, (pr + 1) * LANES)
        o = _pair_softmax_pv(q_ref[0, :, ps], [k_ref[0, :, ps]], [v_ref[0, :, ps]], lambda sub, idx, s: s)
        o_ref[0, :, ps] = o.astype(BF16)


def _ctx_na(zc):
    b, lc, _ = zc.shape
    spec = lambda col: pl.BlockSpec((1, lc, CB), lambda bi: (bi, 0, col // CB))
    return pl.pallas_call(
        _ctx_na_kernel,
        grid=(b,),
        in_specs=[spec(Z_Q), spec(Z_K), spec(Z_V)],
        out_specs=pl.BlockSpec((1, lc, NA_WIDTH), lambda bi: (bi, 0, 0)),
        out_shape=jax.ShapeDtypeStruct((b, lc, NA_WIDTH), BF16),
        compiler_params=_cparams(("parallel",)),
        name="ctx_na",
    )(zc, zc, zc)


def _pool_tile(up_ref, u_ref, un_ref, w_ref, sc_ref, i, seq_len):
    tm = u_ref.shape[1]
    ext = jnp.concatenate([up_ref[0], u_ref[0], un_ref[0]], axis=0).astype(F32)
    n_ext = tm + 2 * POOL_HALO
    tg = i * tm - POOL_HALO + lax.broadcasted_iota(jnp.int32, (n_ext, 1), 0)
    ext = jnp.where((tg >= 0) & (tg < seq_len), ext, 0.0)
    t = i * tm + lax.broadcasted_iota(jnp.int32, (tm, 1), 0)

    def shifted(a, k):
        return pltpu.roll(a, (-k) % n_ext, 0)

    outs = []
    for gi, win in enumerate(POOL_WINDOWS):
        gs = slice(gi * POOL_GROUP, (gi + 1) * POOL_GROUP)
        a = ext[:, gs]
        wsum = a + shifted(a, -1)
        half = 1
        while 2 * half < win:
            wsum = shifted(wsum, -half) + shifted(wsum, half)
            half *= 2
        cnt = (jnp.minimum(t + win // 2, seq_len) - jnp.maximum(t - win // 2, 0)).astype(F32)
        d = wsum[POOL_HALO:POOL_HALO + tm] / cnt - a[POOL_HALO:POOL_HALO + tm]
        y = jnp.dot(d.astype(BF16), w_ref[gi], preferred_element_type=F32)
        outs.append((y * sc_ref[:, gs]).astype(BF16))
    return jnp.concatenate(outs, axis=1)


def _merge_kernel(ona_ref, up_ref, u_ref, un_ref, omla_ref, g0_ref, g1_ref, g2_ref, pw_ref, psc_ref, wb_ref, wo_ref,
                  x_ref, gate_ref, gain_ref, o_ref, *, seq_len):
    o_pool = _pool_tile(up_ref, u_ref, un_ref, pw_ref, psc_ref, pl.program_id(1), seq_len)
    m = None
    for br, g_ref, k in ((ona_ref[0], g0_ref, 0), (o_pool, g1_ref, 1), (omla_ref[0], g2_ref, 2)):
        proj = jnp.dot(br, wb_ref[k], preferred_element_type=F32)
        term = g_ref[0].astype(F32) * proj
        m = term if m is None else m + term
    y = jnp.dot(m.astype(BF16), wo_ref[...], preferred_element_type=F32)
    o_ref[0] = x_ref[0] + gate_ref[0] * _rms(y, gain_ref[...])


def _merge(o_na, z, o_mla, gz, pw, psc, wb, wo, x, gate, gain, layer, tm):
    b, l, d = x.shape
    tm = min(tm, l)
    hb = tm // POOL_HALO
    nhb = l // POOL_HALO
    ucol = Z_U // CB
    per_batch = gate.shape[0] > 1
    mod_map = (lambda bi, i: (bi, 0, 0)) if per_batch else (lambda bi, i: (0, 0, 0))
    br = pl.BlockSpec((1, tm, BRANCH_W), lambda bi, i: (bi, i, 0))
    gspec = lambda k: pl.BlockSpec((1, tm, d), lambda bi, i: (bi, i, k))
    return pl.pallas_call(
        functools.partial(_merge_kernel, seq_len=l),
        grid=(b, l // tm),
        in_specs=[
            br,
            pl.BlockSpec((1, POOL_HALO, CB), lambda bi, i: (bi, jnp.maximum(i * hb - 1, 0), ucol)),
            pl.BlockSpec((1, tm, CB), lambda bi, i: (bi, i, ucol)),
            pl.BlockSpec((1, POOL_HALO, CB), lambda bi, i: (bi, jnp.minimum((i + 1) * hb, nhb - 1), ucol)),
            br, gspec(0), gspec(1), gspec(2),
            pl.BlockSpec((None,) + pw.shape[1:], lambda bi, i: (layer, 0, 0, 0)),
            pl.BlockSpec((1, POOL_WIDTH), lambda bi, i: (0, 0)),
            pl.BlockSpec((None,) + wb.shape[1:], lambda bi, i: (layer, 0, 0, 0)),
            pl.BlockSpec((None,) + wo.shape[1:], lambda bi, i: (layer, 0, 0)),
            pl.BlockSpec((1, tm, d), lambda bi, i: (bi, i, 0)),
            pl.BlockSpec((1, 1, d), mod_map),
            pl.BlockSpec((1, d), lambda bi, i: (0, 0)),
        ],
        out_specs=pl.BlockSpec((1, tm, d), lambda bi, i: (bi, i, 0)),
        out_shape=jax.ShapeDtypeStruct((b, l, d), F32),
        compiler_params=_cparams(("parallel", "parallel")),
        name="merge",
    )(o_na, z, z, z, o_mla, gz, gz, gz, pw, psc, wb, wo, x, gate, gain)


def _gelu_tanh(x):
    return 0.5 * x * (1.0 + jnp.tanh(np.float32(np.sqrt(2.0 / np.pi)) * (x + np.float32(0.044715) * (x * x * x))))


def _ffn_kernel(xp_ref, x_ref, xn_ref, gain_ref, sc_ref, sh_ref, wa_ref, wb_ref, cwa_ref, cwb_ref, cba_ref,
                cbb_ref, wd_ref, gate_ref, gpost_ref, o_ref, h_scr, acc_scr):
    i = pl.program_id(1)
    j = pl.program_id(2)
    ni = pl.num_programs(1)
    nj = pl.num_programs(2)
    tm = x_ref.shape[1]
    n_ext = tm + 2 * CONV_HALO

    @pl.when(j == 0)
    def _():
        def norm_mod(xv):
            return _rms(xv, gain_ref[...]) * (1.0 + sc_ref[0]) + sh_ref[0]

        hp = jnp.where(i > 0, norm_mod(xp_ref[0]), 0.0)
        hn = jnp.where(i < ni - 1, norm_mod(xn_ref[0]), 0.0)
        h_scr[0:CONV_HALO] = hp.astype(BF16)
        h_scr[CONV_HALO:CONV_HALO + tm] = norm_mod(x_ref[0]).astype(BF16)
        h_scr[CONV_HALO + tm:n_ext] = hn.astype(BF16)
        acc_scr[...] = jnp.zeros(acc_scr.shape, F32)

    h = h_scr[...]

    def conv_half(w_ref, cw_ref, cb_ref):
        u = jnp.dot(h, w_ref[...], preferred_element_type=F32)
        prev = pltpu.roll(u, 1, 0)[CONV_HALO:CONV_HALO + tm]
        nxt = pltpu.roll(u, n_ext - 1, 0)[CONV_HALO:CONV_HALO + tm]
        cur = u[CONV_HALO:CONV_HALO + tm]
        return cb_ref[...] + prev * cw_ref[0:1] + cur * cw_ref[1:2] + nxt * cw_ref[2:3]

    a = conv_half(wa_ref, cwa_ref, cba_ref)
    bgate = conv_half(wb_ref, cwb_ref, cbb_ref)
    act = (_gelu_tanh(a) * bgate).astype(BF16)
    acc_scr[...] += jnp.dot(act, wd_ref[...], preferred_element_type=F32)

    @pl.when(j == nj - 1)
    def _():
        o_ref[0] = x_ref[0] + gate_ref[0] * _rms(acc_scr[...], gpost_ref[...])


def _ffn(x, gain, sc, sh, w_up, conv_w, conv_b, w_down, gate, gpost, layer, tm, tn):
    b, l, d = x.shape
    tm = min(tm, l)
    nch = D_FF // tn
    wmode = pl.Buffered(1) if nch == 1 else None
    hb = tm // CONV_HALO
    nhb = l // CONV_HALO
    per_batch = sc.shape[0] > 1
    mod_map = (lambda bi, i, j: (bi, 0, 0)) if per_batch else (lambda bi, i, j: (0, 0, 0))
    mod = pl.BlockSpec((1, 1, d), mod_map)
    vec = pl.BlockSpec((1, d), lambda bi, i, j: (0, 0))
    return pl.pallas_call(
        _ffn_kernel,
        grid=(b, l // tm, nch),
        in_specs=[
            pl.BlockSpec((1, CONV_HALO, d), lambda bi, i, j: (bi, jnp.maximum(i * hb - 1, 0), 0)),
            pl.BlockSpec((1, tm, d), lambda bi, i, j: (bi, i, 0)),
            pl.BlockSpec((1, CONV_HALO, d), lambda bi, i, j: (bi, jnp.minimum((i + 1) * hb, nhb - 1), 0)),
            vec, mod, mod,
            pl.BlockSpec((None, d, tn), lambda bi, i, j: (layer, 0, j), pipeline_mode=wmode),
            pl.BlockSpec((None, d, tn), lambda bi, i, j: (layer, 0, nch + j), pipeline_mode=wmode),
            pl.BlockSpec((None, 3, tn), lambda bi, i, j: (layer, 0, j)),
            pl.BlockSpec((None, 3, tn), lambda bi, i, j: (layer, 0, nch + j)),
            pl.BlockSpec((None, 1, tn), lambda bi, i, j: (layer, 0, j)),
            pl.BlockSpec((None, 1, tn), lambda bi, i, j: (layer, 0, nch + j)),
            pl.BlockSpec((None, tn, d), lambda bi, i, j: (layer, j, 0), pipeline_mode=wmode),
            mod, vec,
        ],
        out_specs=pl.BlockSpec((1, tm, d), lambda bi, i, j: (bi, i, 0)),
        out_shape=jax.ShapeDtypeStruct((b, l, d), F32),
        scratch_shapes=[pltpu.VMEM((tm + 2 * CONV_HALO, d), BF16), pltpu.VMEM((tm, d), F32)],
        compiler_params=_cparams(("parallel", "parallel", "arbitrary")),
        name="ffn",
    )(x, x, x, gain, sc, sh, w_up, w_up, conv_w, conv_w, conv_b, conv_b, w_down, gate, gpost)


_ROPE_PERM = np.concatenate([np.arange(8, 16), np.arange(0, 8), np.arange(24, 32), np.arange(16, 24)])


def _rope_tables(n_tok, rotate):
    f32 = np.float32
    ct = np.ones((n_tok, HEAD_PAD), f32)
    st = np.zeros((n_tok, HEAD_PAD), f32)
    if rotate:
        n_freq = MLA_ROPE // 4
        inv = (f32(ROPE_BASE) ** (-np.arange(n_freq, dtype=f32) / f32(n_freq))).astype(f32)
        t = np.arange(n_tok, dtype=np.int32)
        ang_r = ((t // GRID_W).astype(f32)[:, None] * inv[None, :]).astype(f32)
        ang_c = ((t % GRID_W).astype(f32)[:, None] * inv[None, :]).astype(f32)
        cr, sr, cc, sn = np.cos(ang_r), np.sin(ang_r), np.cos(ang_c), np.sin(ang_c)
        ct[:, MLA_NOPE:MLA_NOPE + MLA_ROPE] = np.concatenate([cr, cr, cc, cc], axis=1)
        st[:, MLA_NOPE:MLA_NOPE + MLA_ROPE] = np.concatenate([-sr, sr, -sn, sn], axis=1)
    return jnp.asarray(ct), jnp.asarray(st)


def _prep_all(w_in, w_uq, w_ukv):
    depth, d, _ = w_in.shape
    w_in = w_in.astype(BF16)
    o = 0
    parts = []
    for s in (NA_WIDTH, NA_WIDTH, NA_WIDTH, POOL_WIDTH, MLA_Q_RANK, MLA_KV_RANK, MLA_ROPE, N_BRANCH * D_MODEL):
        parts.append(w_in[..., o:o + s])
        o += s
    wq, wk, wv, wu, wcq, wckv, wkr, wg = parts
    z64 = jnp.zeros((depth, d, MLA_NOPE), BF16)
    z32 = jnp.zeros((depth, d, HEAD_PAD - MLA_NOPE - MLA_ROPE), BF16)
    w_z = jnp.concatenate([wq * jnp.asarray(NA_HEAD_DIM ** -0.5, BF16), wk, wv, wu, wcq, wckv,
                           z64, wkr, z32, z64, wkr[..., _ROPE_PERM], z32], axis=-1)

    uq = w_uq.reshape(depth, MLA_Q_RANK, MLA_HEADS, MLA_NOPE + MLA_ROPE)
    qz64 = jnp.zeros((depth, MLA_Q_RANK, MLA_HEADS, MLA_NOPE), F32)
    qz32 = jnp.zeros((depth, MLA_Q_RANK, MLA_HEADS, HEAD_PAD - MLA_NOPE - MLA_ROPE), F32)
    wq_pad = jnp.concatenate([uq, qz32], axis=-1)
    wq_perm = jnp.concatenate([qz64, uq[..., MLA_NOPE:][..., _ROPE_PERM], qz32], axis=-1)
    assert MLA_NOPE + MLA_V == HEAD_PAD

    flat = lambda w, k: w.reshape(depth, k, MLA_PAD_W).astype(BF16)
    return dict(
        w_z=w_z.astype(BF16),
        w_g=wg.astype(BF16),
        wq_pad=flat(wq_pad, MLA_Q_RANK),
        wq_perm=flat(wq_perm, MLA_Q_RANK),
        w_kv=w_ukv.astype(BF16),
    )


TM_PROJ = 1024
TN_PROJ = 1536
TM_UP = 1024
TQ_FLASH = 1024
TK_FLASH = 2048
TM_MERGE = 1024
TM_FFN = 512
TN_FFN = D_FF


def kernel(x, c, ctx, c_ctx, w_ada, b_ada, norm_pre1, norm_post1, norm_pre2, norm_post2, w_in, na_rpb, pool_w,
           pool_scale, mla_q_norm, w_uq, mla_kv_norm, w_ukv, w_branch, w_o, w_up, conv_w, conv_b, w_down):
    b, s, d = x.shape
    lc = ctx.shape[1]
    depth = w_ada.shape[0]
    rows = s // GRID_W
    assert d == D_MODEL and s % NA_KTOK == 0 and lc % POOL_HALO == 0

    n_mod = -(-(b + 1) // 8) * 8
    cvec = jnp.concatenate([c, c_ctx[None, :], jnp.zeros((n_mod - b - 1, d), F32)], axis=0)
    mod = _ada(cvec, w_ada, b_ada).reshape(depth, n_mod, 6, d)

    ct_l, st_l = _rope_tables(s, True)
    ct_c, st_c = _rope_tables(b * lc, False)

    p = _prep_all(w_in, w_uq, w_ukv)
    na_bias = _bias_tables(na_rpb, rows)
    pw, wb, wo = pool_w.astype(BF16), w_branch.astype(BF16), w_o.astype(BF16)
    wup, wdn = w_up.astype(BF16), w_down.astype(BF16)
    cb = conv_b[:, None, :]

    xc = ctx
    for l in range(depth):
        last = l == depth - 1
        lat = [mod[l, :b, k][:, None, :] for k in range(6)]
        cx = [mod[l, b:b + 1, k][:, None, :] for k in range(6)]
        row = lambda v: v.reshape(1, -1)
        g_pre1, g_post1, g_pre2, g_post2 = row(norm_pre1[l]), row(norm_post1[l]), row(norm_pre2[l]), row(norm_post2[l])
        qg, kvg = row(mla_q_norm[l]), row(mla_kv_norm[l])
        psc = row(pool_scale[l])

        z, gz = _in_proj(x, g_pre1, lat[1], lat[0], p["w_z"], p["w_g"], l, TM_PROJ, TN_PROJ)
        flat = lambda a: a.reshape(1, b * lc, a.shape[-1])
        unflat = lambda a: a.reshape(b, lc, a.shape[-1])
        zc_f, gzc_f = _in_proj(flat(xc), g_pre1, cx[1], cx[0], p["w_z"], p["w_g"], l, TM_PROJ, TN_PROJ)
        zc, gzc = unflat(zc_f), unflat(gzc_f)
        mla_w = (p["wq_pad"], p["wq_perm"], p["w_kv"], l)
        q_l, k_l, v_l = _mla_up(z, qg, kvg, ct_l, st_l, *mla_w, TM_UP)
        q_c, k_c, v_c = [unflat(a) for a in _mla_up(zc_f, qg, kvg, ct_c, st_c, *mla_w, TM_UP)]

        o_na = _na(z, zc, na_bias, l)
        o_mla = _flash(q_l, k_c, v_c, k_l, v_l, TQ_FLASH, TK_FLASH)
        x = _merge(o_na, z, o_mla, gz, pw, psc, wb, wo, x, lat[2], g_post1, l, TM_MERGE)
        x = _ffn(x, g_pre2, lat[4], lat[3], wup, conv_w, cb, wdn, lat[5], g_post2, l, TM_FFN, TN_FFN)

        if not last:
            oc_na = _ctx_na(zc)
            oc_mla = _flash(q_c, k_c, v_c, None, None, TQ_FLASH, TK_FLASH)
            xc = _merge(oc_na, zc, oc_mla, gzc, pw, psc, wb, wo, xc, cx[2], g_post1, l, TM_MERGE)
            xc = _ffn(xc, g_pre2, cx[4], cx[3], wup, conv_w, cb, wdn, cx[5], g_post2, l, TM_FFN, TN_FFN)
    return x
```
